```python
import math
import jax, jax.numpy as jnp
from jax import lax
import numpy as np

D_MODEL = 1024
BATCH = 4
SEQ = 8192
DEPTH = 1

CHUNK = 64
Q_BLOCK = 128
MEM_LEN = 256
MIX_WIDTH = D_MODEL
FOX_WIDTH = MIX_WIDTH // 2
FOX_HEAD_DIM = 64
FOX_HEADS = FOX_WIDTH // FOX_HEAD_DIM
GLA_WIDTH = MIX_WIDTH - FOX_WIDTH
GLA_HEADS = 4
GLA_VALUE_DIM = GLA_WIDTH // GLA_HEADS
GLA_KEY_DIM = GLA_VALUE_DIM // 2
GLA_KEY_WIDTH = GLA_HEADS * GLA_KEY_DIM
GLA_GATE_RANK = 16
GLA_TAU = 16.0
XATTN_HEADS = 4
XATTN_HEAD_DIM = D_MODEL // XATTN_HEADS
D_FF = 4 * D_MODEL
EPS = 1e-6
IN_SIZES = (FOX_WIDTH, FOX_WIDTH, FOX_WIDTH, FOX_HEADS,
            GLA_KEY_WIDTH, GLA_KEY_WIDTH, GLA_WIDTH, GLA_GATE_RANK, GLA_WIDTH)
IN_WIDTH = sum(IN_SIZES)

kernel_name = "hybrid_fox_gla_memxattn_block"


def _split_points():
    return [int(v) for v in np.cumsum(np.array(IN_SIZES))[:-1]]


def rms_norm(x, g):
    xf = x.astype(jnp.float32)
    y = xf * lax.rsqrt(jnp.mean(xf * xf, axis=-1, keepdims=True) + EPS)
    return (y * g.astype(jnp.float32)).astype(x.dtype)


def fox_attention(q, k, v, log_f):
    B, S, H, D = q.shape
    scale = 1.0 / math.sqrt(D)
    c = jnp.transpose(jnp.cumsum(log_f, axis=1), (0, 2, 1))
    outs = []
    for i in range(S // Q_BLOCK):
        start, end = i * Q_BLOCK, (i + 1) * Q_BLOCK
        qb = q[:, start:end]
        kb = k[:, :end]
        vb = v[:, :end]
        s = jnp.einsum('bqhd,bkhd->bhqk', qb, kb).astype(jnp.float32) * scale
        bias = c[:, :, start:end, None] - c[:, :, None, :end]
        q_pos = start + jnp.arange(Q_BLOCK)
        k_pos = jnp.arange(end)
        mask = k_pos[None, :] <= q_pos[:, None]
        s = jnp.where(mask, s + bias, -jnp.inf)
        p = jax.nn.softmax(s, axis=-1)
        outs.append(jnp.einsum('bhqk,bkhd->bqhd', p.astype(vb.dtype), vb))
    return jnp.concatenate(outs, axis=1)


def gla_chunked(q, k, v, g_log):
    B, S, H, Dk = q.shape
    Dv = v.shape[-1]
    N = S // CHUNK

    def to_chunks(a):
        return jnp.transpose(a.astype(jnp.float32).reshape(B, N, CHUNK, H, a.shape[-1]), (0, 3, 1, 2, 4))

    qc = to_chunks(q) * (Dk ** -0.5)
    kc = to_chunks(k)
    vc = to_chunks(v)
    bc = jnp.cumsum(to_chunks(g_log), axis=3)
    q_dec = qc * jnp.exp(bc)
    k_dec = kc * jnp.exp(-bc)
    A = jnp.einsum('bhnck,bhnsk->bhncs', q_dec, k_dec)
    tril = jnp.tril(jnp.ones((CHUNK, CHUNK), dtype=bool))
    A = jnp.where(tril, A, 0.0)
    o_intra = jnp.einsum('bhncs,bhnsv->bhncv', A, vc)
    b_last = bc[:, :, :, -1:, :]
    dS = jnp.einsum('bhnck,bhncv->bhnkv', kc * jnp.exp(b_last - bc), vc)
    decay = jnp.exp(b_last[:, :, :, 0, :])

    def step(state, inp):
        d, ds = inp
        return d[..., None] * state + ds, state

    init = jnp.zeros((B, H, Dk, Dv), jnp.float32)
    _, s_prev = lax.scan(step, init, (jnp.moveaxis(decay, 2, 0), jnp.moveaxis(dS, 2, 0)))
    s_prev = jnp.moveaxis(s_prev, 0, 2)
    o_inter = jnp.einsum('bhnck,bhnkv->bhncv', q_dec, s_prev)
    o = o_intra + o_inter
    o = jnp.transpose(o, (0, 2, 3, 1, 4)).reshape(B, S, H, Dv)
    return o.astype(v.dtype)


def parallel_mixer(xn, w_in, fox_b_f, fox_q_norm_g, fox_k_norm_g,
                   gla_w_gate2, gla_b_gate, gla_out_norm_g, w_out):
    B, S, _ = xn.shape
    proj = xn @ w_in
    fq, fk, fv, ff, gq, gk, gv, glr, gr = jnp.split(proj, _split_points(), axis=-1)
    fq = rms_norm(fq.reshape(B, S, FOX_HEADS, FOX_HEAD_DIM), fox_q_norm_g)
    fk = rms_norm(fk.reshape(B, S, FOX_HEADS, FOX_HEAD_DIM), fox_k_norm_g)
    fv = fv.reshape(B, S, FOX_HEADS, FOX_HEAD_DIM)
    log_f = jax.nn.log_sigmoid((ff + fox_b_f).astype(jnp.float32))
    fox_out = fox_attention(fq, fk, fv, log_f).reshape(B, S, FOX_WIDTH)
    g_log = jax.nn.log_sigmoid((glr @ gla_w_gate2 + gla_b_gate).astype(jnp.float32)) / GLA_TAU
    g_log = g_log.reshape(B, S, GLA_HEADS, GLA_KEY_DIM)
    gla_o = gla_chunked(gq.reshape(B, S, GLA_HEADS, GLA_KEY_DIM),
                        gk.reshape(B, S, GLA_HEADS, GLA_KEY_DIM),
                        gv.reshape(B, S, GLA_HEADS, GLA_VALUE_DIM), g_log)
    gla_o = rms_norm(gla_o, gla_out_norm_g.reshape(GLA_HEADS, GLA_VALUE_DIM)).reshape(B, S, GLA_WIDTH)
    gla_out = gla_o * jax.nn.silu(gr)
    return jnp.concatenate([fox_out, gla_out], axis=-1) @ w_out


def memory_cross_attention(hn, memn, wq, wkv, q_norm_g, k_norm_g, wo):
    B, S, _ = hn.shape
    M = memn.shape[1]
    q = rms_norm((hn @ wq).reshape(B, S, XATTN_HEADS, XATTN_HEAD_DIM), q_norm_g)
    k, v = jnp.split(memn @ wkv, 2, axis=-1)
    k = rms_norm(k.reshape(B, M, XATTN_HEADS, XATTN_HEAD_DIM), k_norm_g)
    v = v.reshape(B, M, XATTN_HEADS, XATTN_HEAD_DIM)
    s = jnp.einsum('bshd,bmhd->bhsm', q, k).astype(jnp.float32) / math.sqrt(XATTN_HEAD_DIM)
    p = jax.nn.softmax(s, axis=-1)
    o = jnp.einsum('bhsm,bmhd->bshd', p.astype(v.dtype), v).reshape(B, S, D_MODEL)
    return o @ wo


def setup_inputs(seed: int = 0) -> dict:
    key = jax.random.key(seed)
    ks = jax.random.split(key, 24)
    f32 = jnp.float32

    def nrm(k, shape, scale):
        return jax.random.normal(k, shape, f32) * scale

    def gain(k, shape):
        return jnp.ones(shape, f32) + 0.02 * jax.random.normal(k, shape, f32)

    return {
        "x": nrm(ks[0], (BATCH, SEQ, D_MODEL), 1.0),
        "mem": nrm(ks[1], (BATCH, MEM_LEN, D_MODEL), 1.0),
        "norm_mix_g": gain(ks[2], (D_MODEL,)),
        "w_in": nrm(ks[3], (D_MODEL, IN_WIDTH), D_MODEL ** -0.5),
        "fox_b_f": 2.0 + 0.1 * jax.random.normal(ks[4], (FOX_HEADS,), f32),
        "fox_q_norm_g": gain(ks[5], (FOX_HEAD_DIM,)),
        "fox_k_norm_g": gain(ks[6], (FOX_HEAD_DIM,)),
        "gla_w_gate2": nrm(ks[7], (GLA_GATE_RANK, GLA_KEY_WIDTH), GLA_GATE_RANK ** -0.5),
        "gla_b_gate": nrm(ks[8], (GLA_KEY_WIDTH,), 0.02),
        "gla_out_norm_g": gain(ks[9], (GLA_WIDTH,)),
        "w_out": nrm(ks[10], (MIX_WIDTH, D_MODEL), MIX_WIDTH ** -0.5),
        "norm_xattn_g": gain(ks[11], (D_MODEL,)),
        "norm_mem_g": gain(ks[12], (D_MODEL,)),
        "xattn_wq": nrm(ks[13], (D_MODEL, D_MODEL), D_MODEL ** -0.5),
        "xattn_wkv": nrm(ks[14], (D_MODEL, 2 * D_MODEL), D_MODEL ** -0.5),
        "xattn_q_norm_g": gain(ks[15], (XATTN_HEAD_DIM,)),
        "xattn_k_norm_g": gain(ks[16], (XATTN_HEAD_DIM,)),
        "xattn_wo": nrm(ks[17], (D_MODEL, D_MODEL), D_MODEL ** -0.5),
        "norm_mlp_g": gain(ks[18], (D_MODEL,)),
        "mlp_w1": nrm(ks[19], (D_MODEL, D_FF), D_MODEL ** -0.5),
        "mlp_w2": nrm(ks[20], (D_FF, D_MODEL), D_FF ** -0.5),
    }


def reference(x, mem, norm_mix_g, w_in, fox_b_f, fox_q_norm_g, fox_k_norm_g,
              gla_w_gate2, gla_b_gate, gla_out_norm_g, w_out,
              norm_xattn_g, norm_mem_g, xattn_wq, xattn_wkv, xattn_q_norm_g,
              xattn_k_norm_g, xattn_wo, norm_mlp_g, mlp_w1, mlp_w2):
    h = x
    for _ in range(DEPTH):
        h = h + parallel_mixer(rms_norm(h, norm_mix_g), w_in, fox_b_f, fox_q_norm_g,
                               fox_k_norm_g, gla_w_gate2, gla_b_gate, gla_out_norm_g, w_out)
        h = h + memory_cross_attention(rms_norm(h, norm_xattn_g), rms_norm(mem, norm_mem_g),
                                       xattn_wq, xattn_wkv, xattn_q_norm_g, xattn_k_norm_g, xattn_wo)
        u = jax.nn.relu(rms_norm(h, norm_mlp_g) @ mlp_w1)
        h = h + (u * u) @ mlp_w2
    return h
```

```python
import functools
import math

import numpy as np
import jax
import jax.numpy as jnp
from jax import lax
from jax.experimental import pallas as pl
from jax.experimental.pallas import tpu as pltpu

F32 = jnp.float32
BF16 = jnp.bfloat16

EPS = 1e-6
LOG2E = 1.4426950408889634

GLA_CHUNK = 64
GLA_HEADS = 4
GLA_TAU = 16.0

LANES = 128
VMEM_LIMIT_BYTES = 56 * 1024 * 1024

ROW_TILE = 512
FOX_TILE = 512
AUG0 = 64


def _const_spec(shape):
    return pl.BlockSpec(shape, lambda *_: (0,) * len(shape))


def _params(semantics):
    return pltpu.CompilerParams(dimension_semantics=semantics,
                                vmem_limit_bytes=VMEM_LIMIT_BYTES)


def _split3(c):
    hi = c.astype(BF16)
    r = c - hi.astype(F32)
    mid = r.astype(BF16)
    lo = (r - mid.astype(F32)).astype(BF16)
    return hi, mid, lo


def _log_sigmoid(z):
    return -(jnp.maximum(-z, 0.0) + jnp.log1p(jnp.exp(-jnp.abs(z))))


def _dot(a, b):
    return jnp.dot(a, b, preferred_element_type=F32)


def _dot_nt(a, b):
    return lax.dot_general(a, b, (((1,), (1,)), ((), ())), preferred_element_type=F32)


def _dot_tn(a, b):
    return lax.dot_general(a, b, (((0,), (0,)), ((), ())), preferred_element_type=F32)


def _in_proj_kernel(x_ref, g_ref, w_ref, ones_ref, tri_ref, pq_ref, pk_ref, qg_ref, kg_ref,
                    bf_ref, wg2_ref, bg_ref, e64_ref,
                    qa_ref, ka_ref, va_ref, gq_ref, gk_ref, gv_ref, gr_ref, gl_ref,
                    carry_ref, *, fox_w, gkw, gvw, n_fox, head_dim):
    @pl.when(pl.program_id(1) == 0)
    def _():
        carry_ref[...] = jnp.zeros_like(carry_ref)

    x = x_ref[...]
    xn = (x * lax.rsqrt(jnp.mean(x * x, axis=-1, keepdims=True) + EPS) * g_ref[...]).astype(BF16)

    o_k = fox_w
    o_v = 2 * fox_w
    o_gq = 3 * fox_w
    o_gk = o_gq + gkw
    o_gv = o_gk + gkw
    o_gr = o_gv + gvw
    o_sm = o_gr + gvw

    def proj(lo, width):
        return _dot(xn, w_ref[:, lo:lo + width])

    def head_norm(f, gain_row):
        ssq = _dot((f * f).astype(BF16), ones_ref[...])
        return f * lax.rsqrt(ssq * (1.0 / head_dim) + EPS) * gain_row

    def spread(f):
        cols = []
        for h in range(n_fox):
            c = f[:, (h // 2) * LANES:(h // 2 + 1) * LANES]
            if h % 2:
                c = pltpu.roll(c, LANES // 2, axis=1)
            cols.append(c)
        return jnp.concatenate(cols, axis=1)

    sm = proj(o_sm, LANES)
    lane = lax.broadcasted_iota(jnp.int32, sm.shape, 1)
    lf = jnp.where(lane < n_fox, _log_sigmoid(sm + bf_ref[...]), 0.0)
    hi, mid, lo = _split3(lf)
    tri = tri_ref[...]
    c = _dot(tri, hi) + _dot(tri, mid) + _dot(tri, lo) + carry_ref[...]
    tm = c.shape[0]
    carry_ref[...] = c[tm - 1:tm, :]
    hi, mid, lo = _split3(c * LOG2E)
    cpk = (hi.astype(F32) + pltpu.roll(mid.astype(F32), n_fox, axis=1)
           + pltpu.roll(lo.astype(F32), 2 * n_fox, axis=1))
    cpk = jnp.where(lane == 3 * n_fox, 1.0, cpk).astype(BF16)

    wide_lane = lax.broadcasted_iota(jnp.int32, (1, n_fox * LANES), 1)
    is_head_dim = (wide_lane & (LANES - 1)) < head_dim

    fq = head_norm(proj(0, fox_w), qg_ref[...])
    qa_ref[...] = jnp.where(is_head_dim, spread(fq), _dot(cpk, pq_ref[...])).astype(BF16)
    fk = head_norm(proj(o_k, fox_w), kg_ref[...])
    ka_ref[...] = jnp.where(is_head_dim, spread(fk), _dot(cpk, pk_ref[...])).astype(BF16)
    fv = proj(o_v, fox_w)
    va_ref[...] = jnp.where(is_head_dim, spread(fv), e64_ref[...]).astype(BF16)

    gq_ref[...] = proj(o_gq, gkw).astype(BF16)
    gk_ref[...] = proj(o_gk, gkw).astype(BF16)
    gv_ref[...] = proj(o_gv, gvw).astype(BF16)
    gr_ref[...] = proj(o_gr, gvw).astype(BF16)
    gate = _dot(sm.astype(BF16), wg2_ref[...]) + bg_ref[...]
    gl_ref[...] = _log_sigmoid(gate) * (1.0 / GLA_TAU)


def _in_proj(x, norm_g, w_in, fox_b_f, fox_q_norm_g, fox_k_norm_g, gla_w_gate2, gla_b_gate, gvw):
    B, S, D = x.shape
    n_fox = fox_b_f.shape[0]
    head_dim = fox_q_norm_g.shape[0]
    fox_w = n_fox * head_dim
    rank, gkw = gla_w_gate2.shape
    tm = min(ROW_TILE, S)
    assert S % tm == 0 and head_dim == LANES // 2 and n_fox % 2 == 0
    assert 3 * n_fox + 1 <= LANES and n_fox + rank <= LANES

    sizes = (fox_w, fox_w, fox_w, n_fox, gkw, gkw, gvw, rank, gvw)
    offs = np.concatenate([[0], np.cumsum(sizes)])
    wq, wk, wv, wf, wgq, wgk, wgv, wlr, wgr = [w_in[:, offs[i]:offs[i + 1]] for i in range(9)]
    w_small = jnp.zeros((D, LANES), F32).at[:, :n_fox].set(wf).at[:, n_fox:n_fox + rank].set(wlr)
    w_all = jnp.concatenate([wq, wk, wv, wgq, wgk, wgv, wgr, w_small], axis=1).astype(BF16)
    width = w_all.shape[1]

    scale = (1.0 / math.sqrt(head_dim)) * LOG2E
    qg = (jnp.tile(fox_q_norm_g, n_fox) * scale).reshape(1, fox_w)
    kg = jnp.tile(fox_k_norm_g, n_fox).reshape(1, fox_w)
    bf = jnp.zeros((1, LANES), F32).at[0, :n_fox].set(fox_b_f)
    wg2 = jnp.zeros((LANES, gkw), F32).at[n_fox:n_fox + rank].set(gla_w_gate2).astype(BF16)
    bg = gla_b_gate.reshape(1, gkw)

    grp = np.arange(fox_w) // head_dim
    ones_blk = jnp.asarray(grp[:, None] == grp[None, :], BF16)
    r = np.arange(tm)
    tri = jnp.asarray(r[None, :] <= r[:, None], BF16)
    pq = np.zeros((LANES, n_fox * LANES), np.float32)
    pk = np.zeros((LANES, n_fox * LANES), np.float32)
    e64 = np.zeros((1, n_fox * LANES), np.float32)
    for h in range(n_fox):
        base = h * LANES + AUG0
        for part in range(3):
            pq[part * n_fox + h, base + part] = 1.0
            pk[part * n_fox + h, base + 3 + part] = -1.0
            pq[3 * n_fox, base + 3 + part] = 1.0
            pk[3 * n_fox, base + part] = 1.0
        e64[0, base] = 1.0
    pq = jnp.asarray(pq, BF16)
    pk = jnp.asarray(pk, BF16)
    e64 = jnp.asarray(e64)

    row = lambda c: pl.BlockSpec((None, tm, c), lambda b, i: (b, i, 0))
    kern = functools.partial(_in_proj_kernel, fox_w=fox_w, gkw=gkw, gvw=gvw, n_fox=n_fox,
                             head_dim=head_dim)
    aug_w = n_fox * LANES
    out_shapes = [jax.ShapeDtypeStruct((B, S, aug_w), BF16)] * 3 + [
        jax.ShapeDtypeStruct((B, S, gkw), BF16), jax.ShapeDtypeStruct((B, S, gkw), BF16),
        jax.ShapeDtypeStruct((B, S, gvw), BF16), jax.ShapeDtypeStruct((B, S, gvw), BF16),
        jax.ShapeDtypeStruct((B, S, gkw), F32)]
    return pl.pallas_call(
        kern,
        out_shape=out_shapes,
        grid=(B, S // tm),
        in_specs=[row(D), _const_spec((1, D)), _const_spec((D, width)),
                  _const_spec((fox_w, fox_w)), _const_spec((tm, tm)),
                  _const_spec((LANES, aug_w)), _const_spec((LANES, aug_w)),
                  _const_spec((1, fox_w)), _const_spec((1, fox_w)), _const_spec((1, LANES)),
                  _const_spec((LANES, gkw)), _const_spec((1, gkw)), _const_spec((1, aug_w))],
        out_specs=[row(aug_w), row(aug_w), row(aug_w), row(gkw), row(gkw), row(gvw), row(gvw),
                   row(gkw)],
        scratch_shapes=[pltpu.VMEM((1, LANES), F32)],
        compiler_params=_params(("arbitrary", "arbitrary")),
        name="in_proj",
    )(x, norm_g.reshape(1, D), w_all, ones_blk, tri, pq, pk, qg, kg, bf, wg2, bg, e64)


def _gla_kernel(gq_ref, gk_ref, gv_ref, gr_ref, gl_ref, trib_ref, gain_ref, out_ref,
                st_ref, o_ref, *, dk, dv):
    @pl.when(pl.program_id(1) == 0)
    def _():
        st_ref[...] = jnp.zeros_like(st_ref)

    T, kw = gl_ref.shape
    vw = gv_ref.shape[1]
    n_heads = kw // dk
    C = GLA_CHUNK

    g = gl_ref[...]
    hi, mid, lo = _split3(g)
    trib = trib_ref[...]
    bc = _dot(trib, hi) + _dot(trib, mid) + _dot(trib, lo)
    k = gk_ref[...].astype(F32)
    q_dec = gq_ref[...].astype(F32) * (dk ** -0.5) * jnp.exp(bc)
    k_dec = (k * jnp.exp(-bc)).astype(BF16)

    klane = lax.broadcasted_iota(jnp.int32, (1, kw), 1) // dk
    row_h = lax.broadcasted_iota(jnp.int32, (n_heads * C, C), 0)
    col = lax.broadcasted_iota(jnp.int32, (n_heads * C, C), 1)
    tril = col <= (row_h & (C - 1))
    bd = (lax.broadcasted_iota(jnp.int32, (vw, kw), 0) // dv
          == lax.broadcasted_iota(jnp.int32, (vw, kw), 1) // dk)

    for c in range(T // C):
        r0 = c * C
        bcc = bc[r0:r0 + C]
        b_last = bcc[C - 1:C]
        qd = q_dec[r0:r0 + C]
        lhs = jnp.concatenate([jnp.where(klane == h, qd, 0.0) for h in range(n_heads)],
                              axis=0).astype(BF16)
        a = jnp.where(tril, _dot_nt(lhs, k_dec[r0:r0 + C]), 0.0).astype(BF16)
        v = gv_ref[r0:r0 + C, :]
        o_intra = jnp.concatenate(
            [_dot(a[h * C:(h + 1) * C], v[:, h * dv:(h + 1) * dv]) for h in range(n_heads)],
            axis=1)
        st = st_ref[...]
        o_inter = _dot_nt(qd.astype(BF16), st.astype(BF16))
        ks = (k[r0:r0 + C] * jnp.exp(b_last - bcc)).astype(BF16)
        d_st = _dot_tn(v, ks)
        st_ref[...] = st * jnp.exp(b_last) + jnp.where(bd, d_st, 0.0)
        o_ref[r0:r0 + C, :] = o_intra + o_inter

    o = o_ref[...]
    gr = gr_ref[...].astype(F32)
    gain = gain_ref[...]
    outs = []
    for h in range(n_heads):
        oh = o[:, h * dv:(h + 1) * dv]
        on = oh * lax.rsqrt(jnp.mean(oh * oh, axis=-1, keepdims=True) + EPS)
        outs.append(on * gain[:, h * dv:(h + 1) * dv])
    out_ref[...] = (jnp.concatenate(outs, axis=1) * (gr * jax.nn.sigmoid(gr))).astype(BF16)


def _gla(gq, gk, gv, gr, gl, out_norm_g):
    B, S, kw = gq.shape
    vw = gv.shape[2]
    dk, dv = kw // GLA_HEADS, vw // GLA_HEADS
    T = min(ROW_TILE, S)
    assert S % T == 0 and T % GLA_CHUNK == 0
    r = np.arange(T)
    trib = jnp.asarray((r[None, :] <= r[:, None]) & (r[None, :] // GLA_CHUNK == r[:, None] // GLA_CHUNK),
                       BF16)
    row = lambda c: pl.BlockSpec((None, T, c), lambda b, i: (b, i, 0))
    return pl.pallas_call(
        functools.partial(_gla_kernel, dk=dk, dv=dv),
        out_shape=jax.ShapeDtypeStruct((B, S, vw), BF16),
        grid=(B, S // T),
        in_specs=[row(kw), row(kw), row(vw), row(vw), row(kw), _const_spec((T, T)),
                  _const_spec((1, vw))],
        out_specs=row(vw),
        scratch_shapes=[pltpu.VMEM((vw, kw), F32), pltpu.VMEM((T, vw), F32)],
        compiler_params=_params(("arbitrary", "arbitrary")),
        name="gla",
    )(gq, gk, gv, gr, gl, trib, out_norm_g.reshape(1, vw))


def _fox_kernel(q_ref, k_ref, v_ref, o_ref, m_ref, acc_ref, *, head_dim):
    T = q_ref.shape[0]
    i = pl.program_id(2)
    q = q_ref[...]
    m_ref[...] = jnp.full_like(m_ref, -jnp.inf)
    acc_ref[...] = jnp.zeros_like(acc_ref)

    def step(j, masked):
        start = pl.multiple_of(j * T, T)
        s = _dot_nt(q, k_ref[pl.ds(start, T), :])
        if masked:
            row = lax.broadcasted_iota(jnp.int32, (T, T), 0)
            col = lax.broadcasted_iota(jnp.int32, (T, T), 1)
            s = jnp.where(col <= row, s, -jnp.inf)
        m_prev = m_ref[...]
        m_new = jnp.maximum(m_prev, jnp.max(s, axis=-1, keepdims=True))
        p = jnp.exp2(s - m_new).astype(BF16)
        acc_ref[...] = jnp.exp2(m_prev - m_new) * acc_ref[...] + _dot(p, v_ref[pl.ds(start, T), :])
        m_ref[...] = m_new

    def body(j, carry):
        step(j, False)
        return carry

    lax.fori_loop(0, i, body, 0)
    step(i, True)

    acc = acc_ref[...]
    lane = lax.broadcasted_iota(jnp.int32, acc.shape, 1)
    inv_l = 1.0 / acc[:, head_dim:head_dim + 1]
    o_ref[...] = jnp.where(lane < head_dim, acc * inv_l, 0.0).astype(BF16)


def _fox(qa, ka, va, head_dim):
    B, S, W = qa.shape
    H = W // LANES
    T = min(FOX_TILE, S)
    assert S % T == 0
    return pl.pallas_call(
        functools.partial(_fox_kernel, head_dim=head_dim),
        out_shape=jax.ShapeDtypeStruct((B, S, W), BF16),
        grid=(B, H, S // T),
        in_specs=[pl.BlockSpec((None, T, LANES), lambda b, h, i: (b, i, h)),
                  pl.BlockSpec((None, S, LANES), lambda b, h, i: (b, 0, h)),
                  pl.BlockSpec((None, S, LANES), lambda b, h, i: (b, 0, h))],
        out_specs=pl.BlockSpec((None, T, LANES), lambda b, h, i: (b, i, h)),
        scratch_shapes=[pltpu.VMEM((T, 1), F32), pltpu.VMEM((T, LANES), F32)],
        compiler_params=_params(("arbitrary", "arbitrary", "arbitrary")),
        name="fox",
    )(qa, ka, va)


def _mem_kv_kernel(mem_ref, g_ref, w_ref, kg_ref, k_ref, v_ref, *, xd):
    x = mem_ref[...]
    D = x.shape[1]
    xn = (x * lax.rsqrt(jnp.mean(x * x, axis=-1, keepdims=True) + EPS) * g_ref[...]).astype(BF16)
    kg = kg_ref[...]
    for h in range(D // xd):
        kh = _dot(xn, w_ref[:, h * xd:(h + 1) * xd])
        kn = kh * lax.rsqrt(jnp.mean(kh * kh, axis=-1, keepdims=True) + EPS) * kg
        k_ref[:, h * xd:(h + 1) * xd] = kn.astype(BF16)
    v_ref[...] = _dot(xn, w_ref[:, D:]).astype(BF16)


def _mem_kv(mem, norm_mem_g, wkv, k_norm_g):
    B, M, D = mem.shape
    xd = k_norm_g.shape[0]
    kg = (k_norm_g * (LOG2E / math.sqrt(xd))).reshape(1, xd)
    blk = pl.BlockSpec((None, M, D), lambda b: (b, 0, 0))
    return pl.pallas_call(
        functools.partial(_mem_kv_kernel, xd=xd),
        out_shape=[jax.ShapeDtypeStruct((B, M, D), BF16)] * 2,
        grid=(B,),
        in_specs=[blk, _const_spec((1, D)), _const_spec((D, 2 * D)), _const_spec((1, xd))],
        out_specs=[blk, blk],
        compiler_params=_params(("arbitrary",)),
        name="mem_kv",
    )(mem, norm_mem_g.reshape(1, D), wkv.astype(BF16), kg)


def _mix_kernel(x_ref, fox_ref, gla_ref, wof_ref, wog_ref, g_ref, wq_ref, qg_ref, km_ref, vm_ref,
                wo_ref, h_ref, *, xd):
    h = x_ref[...] + _dot(fox_ref[...], wof_ref[...]) + _dot(gla_ref[...], wog_ref[...])
    hn = (h * lax.rsqrt(jnp.mean(h * h, axis=-1, keepdims=True) + EPS) * g_ref[...]).astype(BF16)
    D = h.shape[1]
    qg = qg_ref[...]
    outs = []
    for hd in range(D // xd):
        sl = slice(hd * xd, (hd + 1) * xd)
        q = _dot(hn, wq_ref[:, sl])
        qn = (q * lax.rsqrt(jnp.mean(q * q, axis=-1, keepdims=True) + EPS) * qg).astype(BF16)
        s = _dot_nt(qn, km_ref[:, sl])
        p = jnp.exp2(s - jnp.max(s, axis=-1, keepdims=True))
        inv_l = 1.0 / jnp.sum(p, axis=-1, keepdims=True)
        outs.append((_dot(p.astype(BF16), vm_ref[:, sl]) * inv_l).astype(BF16))
    h_ref[...] = h + _dot(jnp.concatenate(outs, axis=1), wo_ref[...])


def _mix(x, fox, gla, w_out, head_dim, norm_g, wq, q_norm_g, km, vm, wo):
    B, S, D = x.shape
    fw = fox.shape[2]
    gw = gla.shape[2]
    M = km.shape[1]
    xd = q_norm_g.shape[0]
    n_fox = fw // LANES
    tm = min(ROW_TILE, S)
    wof = jnp.zeros((n_fox, LANES, D), F32).at[:, :head_dim].set(
        w_out[:n_fox * head_dim].reshape(n_fox, head_dim, D)).reshape(fw, D).astype(BF16)
    wog = w_out[n_fox * head_dim:].astype(BF16)
    row = lambda c: pl.BlockSpec((None, tm, c), lambda b, i: (b, i, 0))
    mem = pl.BlockSpec((None, M, D), lambda b, i: (b, 0, 0))
    return pl.pallas_call(
        functools.partial(_mix_kernel, xd=xd),
        out_shape=jax.ShapeDtypeStruct((B, S, D), F32),
        grid=(B, S // tm),
        in_specs=[row(D), row(fw), row(gw), _const_spec((fw, D)), _const_spec((gw, D)),
                  _const_spec((1, D)), _const_spec((D, D)), _const_spec((1, xd)), mem, mem,
                  _const_spec((D, D))],
        out_specs=row(D),
        compiler_params=_params(("arbitrary", "arbitrary")),
        name="mix",
    )(x, fox, gla, wof, wog, norm_g.reshape(1, D), wq.astype(BF16), q_norm_g.reshape(1, xd),
      km, vm, wo.astype(BF16))


def _mlp_kernel(h_ref, g_ref, w1_ref, w2_ref, y_ref, *, slab):
    h = h_ref[...]
    hn = (h * lax.rsqrt(jnp.mean(h * h, axis=-1, keepdims=True) + EPS) * g_ref[...]).astype(BF16)
    y = h
    for j in range(w1_ref.shape[1] // slab):
        u = jnp.maximum(_dot(hn, w1_ref[:, j * slab:(j + 1) * slab]), 0.0)
        y = y + _dot((u * u).astype(BF16), w2_ref[j * slab:(j + 1) * slab, :])
    y_ref[...] = y


def _mlp(h, norm_g, w1, w2):
    B, S, D = h.shape
    F = w1.shape[1]
    tm = min(ROW_TILE, S)
    row = pl.BlockSpec((None, tm, D), lambda b, i: (b, i, 0))
    return pl.pallas_call(
        functools.partial(_mlp_kernel, slab=min(F, 1024)),
        out_shape=jax.ShapeDtypeStruct((B, S, D), F32),
        grid=(B, S // tm),
        in_specs=[row, _const_spec((1, D)), _const_spec((D, F)), _const_spec((F, D))],
        out_specs=row,
        compiler_params=_params(("arbitrary", "arbitrary")),
        name="mlp",
    )(h, norm_g.reshape(1, D), w1.astype(BF16), w2.astype(BF16))


def kernel(x, mem, norm_mix_g, w_in, fox_b_f, fox_q_norm_g, fox_k_norm_g, gla_w_gate2, gla_b_gate,
           gla_out_norm_g, w_out, norm_xattn_g, norm_mem_g, xattn_wq, xattn_wkv, xattn_q_norm_g,
           xattn_k_norm_g, xattn_wo, norm_mlp_g, mlp_w1, mlp_w2):
    head_dim = fox_q_norm_g.shape[0]
    qa, ka, va, gq, gk, gv, gr, gl = _in_proj(
        x, norm_mix_g, w_in, fox_b_f, fox_q_norm_g, fox_k_norm_g, gla_w_gate2, gla_b_gate,
        gla_out_norm_g.shape[0])
    gla = _gla(gq, gk, gv, gr, gl, gla_out_norm_g)
    fox = _fox(qa, ka, va, head_dim)
    km, vm = _mem_kv(mem, norm_mem_g, xattn_wkv, xattn_k_norm_g)
    h = _mix(x, fox, gla, w_out, head_dim, norm_xattn_g, xattn_wq, xattn_q_norm_g, km, vm, xattn_wo)
    return _mlp(h, norm_mlp_g, mlp_w1, mlp_w2)
```

```python
import functools
import math

import numpy as np
import jax
import jax.numpy as jnp
from jax import lax
from jax.experimental import pallas as pl
from jax.experimental.pallas import tpu as pltpu

F32 = jnp.float32
BF16 = jnp.bfloat16

EPS = 1e-6
LOG2E = 1.4426950408889634

GLA_CHUNK = 64
GLA_HEADS = 4
GLA_TAU = 16.0

LANES = 128
VMEM_LIMIT_BYTES = 56 * 1024 * 1024

ROW_TILE = 512
FOX_TILE = 512
FOX_SUB = 4
AUG0 = 64


def _const_spec(shape):
    return pl.BlockSpec(shape, lambda *_: (0,) * len(shape))


def _params(semantics):
    return pltpu.CompilerParams(dimension_semantics=semantics,
                                vmem_limit_bytes=VMEM_LIMIT_BYTES)


def _split3(c):
    hi = c.astype(BF16)
    r = c - hi.astype(F32)
    mid = r.astype(BF16)
    lo = (r - mid.astype(F32)).astype(BF16)
    return hi, mid, lo


def _log_sigmoid(z):
    return -(jnp.maximum(-z, 0.0) + jnp.log1p(jnp.exp(-jnp.abs(z))))


def _dot(a, b):
    return jnp.dot(a, b, preferred_element_type=F32)


def _dot_nt(a, b):
    return lax.dot_general(a, b, (((1,), (1,)), ((), ())), preferred_element_type=F32)


def _dot_tn(a, b):
    return lax.dot_general(a, b, (((0,), (0,)), ((), ())), preferred_element_type=F32)


def _in_proj_kernel(x_ref, g_ref, w_ref, ones_ref, tri_ref, pq_ref, pk_ref, qg_ref, kg_ref,
                    bf_ref, wg2_ref, bg_ref, e64_ref,
                    qa_ref, ka_ref, va_ref, gq_ref, gk_ref, gv_ref, gr_ref, gl_ref,
                    carry_ref, *, fox_w, gkw, gvw, n_fox, head_dim):
    @pl.when(pl.program_id(1) == 0)
    def _():
        carry_ref[...] = jnp.zeros_like(carry_ref)

    x = x_ref[...]
    xn = (x * lax.rsqrt(jnp.mean(x * x, axis=-1, keepdims=True) + EPS) * g_ref[...]).astype(BF16)

    o_k = fox_w
    o_v = 2 * fox_w
    o_gq = 3 * fox_w
    o_gk = o_gq + gkw
    o_gv = o_gk + gkw
    o_gr = o_gv + gvw
    o_sm = o_gr + gvw

    def proj(lo, width):
        return _dot(xn, w_ref[:, lo:lo + width])

    def head_norm(f, gain_row):
        ssq = _dot((f * f).astype(BF16), ones_ref[...])
        return f * lax.rsqrt(ssq * (1.0 / head_dim) + EPS) * gain_row

    def spread(f):
        cols = []
        for h in range(n_fox):
            c = f[:, (h // 2) * LANES:(h // 2 + 1) * LANES]
            if h % 2:
                c = pltpu.roll(c, LANES // 2, axis=1)
            cols.append(c)
        return jnp.concatenate(cols, axis=1)

    sm = proj(o_sm, LANES)
    lane = lax.broadcasted_iota(jnp.int32, sm.shape, 1)
    lf = jnp.where(lane < n_fox, _log_sigmoid(sm + bf_ref[...]), 0.0)
    hi, mid, lo = _split3(lf)
    tri = tri_ref[...]
    c = _dot(tri, hi) + _dot(tri, mid) + _dot(tri, lo) + carry_ref[...]
    tm = c.shape[0]
    carry_ref[...] = c[tm - 1:tm, :]
    hi, mid, lo = _split3(c * LOG2E)
    cpk = (hi.astype(F32) + pltpu.roll(mid.astype(F32), n_fox, axis=1)
           + pltpu.roll(lo.astype(F32), 2 * n_fox, axis=1))
    cpk = jnp.where(lane == 3 * n_fox, 1.0, cpk).astype(BF16)

    wide_lane = lax.broadcasted_iota(jnp.int32, (1, n_fox * LANES), 1)
    is_head_dim = (wide_lane & (LANES - 1)) < head_dim

    fq = head_norm(proj(0, fox_w), qg_ref[...])
    qa_ref[...] = jnp.where(is_head_dim, spread(fq), _dot(cpk, pq_ref[...])).astype(BF16)
    fk = head_norm(proj(o_k, fox_w), kg_ref[...])
    ka_ref[...] = jnp.where(is_head_dim, spread(fk), _dot(cpk, pk_ref[...])).astype(BF16)
    fv = proj(o_v, fox_w)
    va_ref[...] = jnp.where(is_head_dim, spread(fv), e64_ref[...]).astype(BF16)

    gq_ref[...] = proj(o_gq, gkw).astype(BF16)
    gk_ref[...] = proj(o_gk, gkw).astype(BF16)
    gv_ref[...] = proj(o_gv, gvw).astype(BF16)
    gr_ref[...] = proj(o_gr, gvw).astype(BF16)
    gate = _dot(sm.astype(BF16), wg2_ref[...]) + bg_ref[...]
    gl_ref[...] = _log_sigmoid(gate) * (1.0 / GLA_TAU)


def _in_proj(x, norm_g, w_in, fox_b_f, fox_q_norm_g, fox_k_norm_g, gla_w_gate2, gla_b_gate, gvw):
    B, S, D = x.shape
    n_fox = fox_b_f.shape[0]
    head_dim = fox_q_norm_g.shape[0]
    fox_w = n_fox * head_dim
    rank, gkw = gla_w_gate2.shape
    tm = min(ROW_TILE, S)
    assert S % tm == 0 and head_dim == LANES // 2 and n_fox % 2 == 0
    assert 3 * n_fox + 1 <= LANES and n_fox + rank <= LANES

    sizes = (fox_w, fox_w, fox_w, n_fox, gkw, gkw, gvw, rank, gvw)
    offs = np.concatenate([[0], np.cumsum(sizes)])
    wq, wk, wv, wf, wgq, wgk, wgv, wlr, wgr = [w_in[:, offs[i]:offs[i + 1]] for i in range(9)]
    w_small = jnp.zeros((D, LANES), F32).at[:, :n_fox].set(wf).at[:, n_fox:n_fox + rank].set(wlr)
    w_all = jnp.concatenate([wq, wk, wv, wgq, wgk, wgv, wgr, w_small], axis=1).astype(BF16)
    width = w_all.shape[1]

    scale = (1.0 / math.sqrt(head_dim)) * LOG2E
    qg = (jnp.tile(fox_q_norm_g, n_fox) * scale).reshape(1, fox_w)
    kg = jnp.tile(fox_k_norm_g, n_fox).reshape(1, fox_w)
    bf = jnp.zeros((1, LANES), F32).at[0, :n_fox].set(fox_b_f)
    wg2 = jnp.zeros((LANES, gkw), F32).at[n_fox:n_fox + rank].set(gla_w_gate2).astype(BF16)
    bg = gla_b_gate.reshape(1, gkw)

    grp = np.arange(fox_w) // head_dim
    ones_blk = jnp.asarray(grp[:, None] == grp[None, :], BF16)
    r = np.arange(tm)
    tri = jnp.asarray(r[None, :] <= r[:, None], BF16)
    pq = np.zeros((LANES, n_fox * LANES), np.float32)
    pk = np.zeros((LANES, n_fox * LANES), np.float32)
    e64 = np.zeros((1, n_fox * LANES), np.float32)
    for h in range(n_fox):
        base = h * LANES + AUG0
        for part in range(3):
            pq[part * n_fox + h, base + part] = 1.0
            pk[part * n_fox + h, base + 3 + part] = -1.0
            pq[3 * n_fox, base + 3 + part] = 1.0
            pk[3 * n_fox, base + part] = 1.0
        e64[0, base] = 1.0
    pq = jnp.asarray(pq, BF16)
    pk = jnp.asarray(pk, BF16)
    e64 = jnp.asarray(e64)

    row = lambda c: pl.BlockSpec((None, tm, c), lambda b, i: (b, i, 0))
    kern = functools.partial(_in_proj_kernel, fox_w=fox_w, gkw=gkw, gvw=gvw, n_fox=n_fox,
                             head_dim=head_dim)
    aug_w = n_fox * LANES
    out_shapes = [jax.ShapeDtypeStruct((B, S, aug_w), BF16)] * 3 + [
        jax.ShapeDtypeStruct((B, S, gkw), BF16), jax.ShapeDtypeStruct((B, S, gkw), BF16),
        jax.ShapeDtypeStruct((B, S, gvw), BF16), jax.ShapeDtypeStruct((B, S, gvw), BF16),
        jax.ShapeDtypeStruct((B, S, gkw), F32)]
    return pl.pallas_call(
        kern,
        out_shape=out_shapes,
        grid=(B, S // tm),
        in_specs=[row(D), _const_spec((1, D)), _const_spec((D, width)),
                  _const_spec((fox_w, fox_w)), _const_spec((tm, tm)),
                  _const_spec((LANES, aug_w)), _const_spec((LANES, aug_w)),
                  _const_spec((1, fox_w)), _const_spec((1, fox_w)), _const_spec((1, LANES)),
                  _const_spec((LANES, gkw)), _const_spec((1, gkw)), _const_spec((1, aug_w))],
        out_specs=[row(aug_w), row(aug_w), row(aug_w), row(gkw), row(gkw), row(gvw), row(gvw),
                   row(gkw)],
        scratch_shapes=[pltpu.VMEM((1, LANES), F32)],
        compiler_params=_params(("arbitrary", "arbitrary")),
        name="in_proj",
    )(x, norm_g.reshape(1, D), w_all, ones_blk, tri, pq, pk, qg, kg, bf, wg2, bg, e64)


def _gla_kernel(gq_ref, gk_ref, gv_ref, gr_ref, gl_ref, trib_ref, gain_ref, out_ref,
                st_ref, o_ref, *, dk, dv):
    @pl.when(pl.program_id(1) == 0)
    def _():
        st_ref[...] = jnp.zeros_like(st_ref)

    T, kw = gl_ref.shape
    vw = gv_ref.shape[1]
    n_heads = kw // dk
    C = GLA_CHUNK

    g = gl_ref[...]
    hi, mid, lo = _split3(g)
    trib = trib_ref[...]
    bc = _dot(trib, hi) + _dot(trib, mid) + _dot(trib, lo)
    k = gk_ref[...].astype(F32)
    q_dec = gq_ref[...].astype(F32) * (dk ** -0.5) * jnp.exp(bc)
    k_dec = (k * jnp.exp(-bc)).astype(BF16)

    klane = lax.broadcasted_iota(jnp.int32, (1, kw), 1) // dk
    row_h = lax.broadcasted_iota(jnp.int32, (n_heads * C, C), 0)
    col = lax.broadcasted_iota(jnp.int32, (n_heads * C, C), 1)
    tril = col <= (row_h & (C - 1))
    bd = (lax.broadcasted_iota(jnp.int32, (vw, kw), 0) // dv
          == lax.broadcasted_iota(jnp.int32, (vw, kw), 1) // dk)

    for c in range(T // C):
        r0 = c * C
        bcc = bc[r0:r0 + C]
        b_last = bcc[C - 1:C]
        qd = q_dec[r0:r0 + C]
        lhs = jnp.concatenate([jnp.where(klane == h, qd, 0.0) for h in range(n_heads)],
                              axis=0).astype(BF16)
        a = jnp.where(tril, _dot_nt(lhs, k_dec[r0:r0 + C]), 0.0).astype(BF16)
        v = gv_ref[r0:r0 + C, :]
        o_intra = jnp.concatenate(
            [_dot(a[h * C:(h + 1) * C], v[:, h * dv:(h + 1) * dv]) for h in range(n_heads)],
            axis=1)
        st = st_ref[...]
        o_inter = _dot_nt(qd.astype(BF16), st.astype(BF16))
        ks = (k[r0:r0 + C] * jnp.exp(b_last - bcc)).astype(BF16)
        d_st = _dot_tn(v, ks)
        st_ref[...] = st * jnp.exp(b_last) + jnp.where(bd, d_st, 0.0)
        o_ref[r0:r0 + C, :] = o_intra + o_inter

    o = o_ref[...]
    gr = gr_ref[...].astype(F32)
    gain = gain_ref[...]
    outs = []
    for h in range(n_heads):
        oh = o[:, h * dv:(h + 1) * dv]
        on = oh * lax.rsqrt(jnp.mean(oh * oh, axis=-1, keepdims=True) + EPS)
        outs.append(on * gain[:, h * dv:(h + 1) * dv])
    out_ref[...] = (jnp.concatenate(outs, axis=1) * (gr * jax.nn.sigmoid(gr))).astype(BF16)


def _gla(gq, gk, gv, gr, gl, out_norm_g):
    B, S, kw = gq.shape
    vw = gv.shape[2]
    dk, dv = kw // GLA_HEADS, vw // GLA_HEADS
    T = min(ROW_TILE, S)
    assert S % T == 0 and T % GLA_CHUNK == 0
    r = np.arange(T)
    trib = jnp.asarray((r[None, :] <= r[:, None]) & (r[None, :] // GLA_CHUNK == r[:, None] // GLA_CHUNK),
                       BF16)
    row = lambda c: pl.BlockSpec((None, T, c), lambda b, i: (b, i, 0))
    return pl.pallas_call(
        functools.partial(_gla_kernel, dk=dk, dv=dv),
        out_shape=jax.ShapeDtypeStruct((B, S, vw), BF16),
        grid=(B, S // T),
        in_specs=[row(kw), row(kw), row(vw), row(vw), row(kw), _const_spec((T, T)),
                  _const_spec((1, vw))],
        out_specs=row(vw),
        scratch_shapes=[pltpu.VMEM((vw, kw), F32), pltpu.VMEM((T, vw), F32)],
        compiler_params=_params(("arbitrary", "arbitrary")),
        name="gla",
    )(gq, gk, gv, gr, gl, trib, out_norm_g.reshape(1, vw))


def _fox_kernel(q_ref, k_ref, v_ref, o_ref, m_ref, acc_ref, sa_ref, sb_ref, *, head_dim, n_sub):
    T = q_ref.shape[0]
    R = T // n_sub
    i = pl.program_id(2)
    m_ref[...] = jnp.full_like(m_ref, -jnp.inf)
    acc_ref[...] = jnp.zeros_like(acc_ref)

    def scores(j, s_ref):
        start = pl.multiple_of(j * T, T)
        s_ref[...] = _dot_nt(q_ref[...], k_ref[pl.ds(start, T), :])

    def softmax_pv(j, s_ref, diag):
        start = pl.multiple_of(j * T, T)
        for r in range(n_sub):
            rows = slice(r * R, (r + 1) * R)
            ncols = (r + 1) * R if diag else T
            s = s_ref[rows, :ncols]
            if diag:
                row = lax.broadcasted_iota(jnp.int32, (R, ncols), 0) + r * R
                col = lax.broadcasted_iota(jnp.int32, (R, ncols), 1)
                s = jnp.where(col <= row, s, -jnp.inf)
            m_prev = m_ref[rows, :]
            m_new = jnp.maximum(m_prev, jnp.max(s, axis=-1, keepdims=True))
            p = jnp.exp2(s - m_new).astype(BF16)
            acc_ref[rows, :] = (jnp.exp2(m_prev - m_new) * acc_ref[rows, :]
                                + _dot(p, v_ref[pl.ds(start, ncols), :]))
            m_ref[rows, :] = m_new

    scores(0, sa_ref)

    def pair(p, carry):
        scores(2 * p + 1, sb_ref)
        softmax_pv(2 * p, sa_ref, False)
        scores(2 * p + 2, sa_ref)
        softmax_pv(2 * p + 1, sb_ref, False)
        return carry

    lax.fori_loop(0, i // 2, pair, 0)

    @pl.when(i % 2 == 0)
    def _():
        softmax_pv(i, sa_ref, True)

    @pl.when(i % 2 == 1)
    def _():
        scores(i, sb_ref)
        softmax_pv(i - 1, sa_ref, False)
        softmax_pv(i, sb_ref, True)

    acc = acc_ref[...]
    lane = lax.broadcasted_iota(jnp.int32, acc.shape, 1)
    inv_l = 1.0 / acc[:, head_dim:head_dim + 1]
    o_ref[...] = jnp.where(lane < head_dim, acc * inv_l, 0.0).astype(BF16)


def _fox(qa, ka, va, head_dim):
    B, S, W = qa.shape
    H = W // LANES
    T = min(FOX_TILE, S)
    assert S % T == 0
    return pl.pallas_call(
        functools.partial(_fox_kernel, head_dim=head_dim, n_sub=FOX_SUB),
        out_shape=jax.ShapeDtypeStruct((B, S, W), BF16),
        grid=(B, H, S // T),
        in_specs=[pl.BlockSpec((None, T, LANES), lambda b, h, i: (b, i, h)),
                  pl.BlockSpec((None, S, LANES), lambda b, h, i: (b, 0, h)),
                  pl.BlockSpec((None, S, LANES), lambda b, h, i: (b, 0, h))],
        out_specs=pl.BlockSpec((None, T, LANES), lambda b, h, i: (b, i, h)),
        scratch_shapes=[pltpu.VMEM((T, 1), F32), pltpu.VMEM((T, LANES), F32),
                        pltpu.VMEM((T, T), F32), pltpu.VMEM((T, T), F32)],
        compiler_params=_params(("arbitrary", "arbitrary", "arbitrary")),
        name="fox",
    )(qa, ka, va)


def _mem_kv_kernel(mem_ref, g_ref, w_ref, kg_ref, k_ref, v_ref, *, xd):
    x = mem_ref[...]
    D = x.shape[1]
    xn = (x * lax.rsqrt(jnp.mean(x * x, axis=-1, keepdims=True) + EPS) * g_ref[...]).astype(BF16)
    kg = kg_ref[...]
    for h in range(D // xd):
        kh = _dot(xn, w_ref[:, h * xd:(h + 1) * xd])
        kn = kh * lax.rsqrt(jnp.mean(kh * kh, axis=-1, keepdims=True) + EPS) * kg
        k_ref[:, h * xd:(h + 1) * xd] = kn.astype(BF16)
    v_ref[...] = _dot(xn, w_ref[:, D:]).astype(BF16)


def _mem_kv(mem, norm_mem_g, wkv, k_norm_g):
    B, M, D = mem.shape
    xd = k_norm_g.shape[0]
    kg = (k_norm_g * (LOG2E / math.sqrt(xd))).reshape(1, xd)
    blk = pl.BlockSpec((None, M, D), lambda b: (b, 0, 0))
    return pl.pallas_call(
        functools.partial(_mem_kv_kernel, xd=xd),
        out_shape=[jax.ShapeDtypeStruct((B, M, D), BF16)] * 2,
        grid=(B,),
        in_specs=[blk, _const_spec((1, D)), _const_spec((D, 2 * D)), _const_spec((1, xd))],
        out_specs=[blk, blk],
        compiler_params=_params(("arbitrary",)),
        name="mem_kv",
    )(mem, norm_mem_g.reshape(1, D), wkv.astype(BF16), kg)


def _mix_kernel(x_ref, fox_ref, gla_ref, wof_ref, wog_ref, g_ref, wq_ref, qg_ref, km_ref, vm_ref,
                wo_ref, h_ref, *, xd):
    h = x_ref[...] + _dot(fox_ref[...], wof_ref[...]) + _dot(gla_ref[...], wog_ref[...])
    hn = (h * lax.rsqrt(jnp.mean(h * h, axis=-1, keepdims=True) + EPS) * g_ref[...]).astype(BF16)
    D = h.shape[1]
    qg = qg_ref[...]
    outs = []
    for hd in range(D // xd):
        sl = slice(hd * xd, (hd + 1) * xd)
        q = _dot(hn, wq_ref[:, sl])
        qn = (q * lax.rsqrt(jnp.mean(q * q, axis=-1, keepdims=True) + EPS) * qg).astype(BF16)
        s = _dot_nt(qn, km_ref[:, sl])
        p = jnp.exp2(s - jnp.max(s, axis=-1, keepdims=True))
        inv_l = 1.0 / jnp.sum(p, axis=-1, keepdims=True)
        outs.append((_dot(p.astype(BF16), vm_ref[:, sl]) * inv_l).astype(BF16))
    h_ref[...] = h + _dot(jnp.concatenate(outs, axis=1), wo_ref[...])


def _mix(x, fox, gla, w_out, head_dim, norm_g, wq, q_norm_g, km, vm, wo):
    B, S, D = x.shape
    fw = fox.shape[2]
    gw = gla.shape[2]
    M = km.shape[1]
    xd = q_norm_g.shape[0]
    n_fox = fw // LANES
    tm = min(ROW_TILE, S)
    wof = jnp.zeros((n_fox, LANES, D), F32).at[:, :head_dim].set(
        w_out[:n_fox * head_dim].reshape(n_fox, head_dim, D)).reshape(fw, D).astype(BF16)
    wog = w_out[n_fox * head_dim:].astype(BF16)
    row = lambda c: pl.BlockSpec((None, tm, c), lambda b, i: (b, i, 0))
    mem = pl.BlockSpec((None, M, D), lambda b, i: (b, 0, 0))
    return pl.pallas_call(
        functools.partial(_mix_kernel, xd=xd),
        out_shape=jax.ShapeDtypeStruct((B, S, D), F32),
        grid=(B, S // tm),
        in_specs=[row(D), row(fw), row(gw), _const_spec((fw, D)), _const_spec((gw, D)),
                  _const_spec((1, D)), _const_spec((D, D)), _const_spec((1, xd)), mem, mem,
                  _const_spec((D, D))],
        out_specs=row(D),
        compiler_params=_params(("arbitrary", "arbitrary")),
        name="mix",
    )(x, fox, gla, wof, wog, norm_g.reshape(1, D), wq.astype(BF16), q_norm_g.reshape(1, xd),
      km, vm, wo.astype(BF16))


def _mlp_kernel(h_ref, g_ref, w1_ref, w2_ref, y_ref, *, slab):
    h = h_ref[...]
    hn = (h * lax.rsqrt(jnp.mean(h * h, axis=-1, keepdims=True) + EPS) * g_ref[...]).astype(BF16)
    y = h
    for j in range(w1_ref.shape[1] // slab):
        u = jnp.maximum(_dot(hn, w1_ref[:, j * slab:(j + 1) * slab]), 0.0)
        y = y + _dot((u * u).astype(BF16), w2_ref[j * slab:(j + 1) * slab, :])
    y_ref[...] = y


def _mlp(h, norm_g, w1, w2):
    B, S, D = h.shape
    F = w1.shape[1]
    tm = min(ROW_TILE, S)
    row = pl.BlockSpec((None, tm, D), lambda b, i: (b, i, 0))
    return pl.pallas_call(
        functools.partial(_mlp_kernel, slab=min(F, 1024)),
        out_shape=jax.ShapeDtypeStruct((B, S, D), F32),
        grid=(B, S // tm),
        in_specs=[row, _const_spec((1, D)), _const_spec((D, F)), _const_spec((F, D))],
        out_specs=row,
        compiler_params=_params(("arbitrary", "arbitrary")),
        name="mlp",
    )(h, norm_g.reshape(1, D), w1.astype(BF16), w2.astype(BF16))


def kernel(x, mem, norm_mix_g, w_in, fox_b_f, fox_q_norm_g, fox_k_norm_g, gla_w_gate2, gla_b_gate,
           gla_out_norm_g, w_out, norm_xattn_g, norm_mem_g, xattn_wq, xattn_wkv, xattn_q_norm_g,
           xattn_k_norm_g, xattn_wo, norm_mlp_g, mlp_w1, mlp_w2):
    head_dim = fox_q_norm_g.shape[0]
    qa, ka, va, gq, gk, gv, gr, gl = _in_proj(
        x, norm_mix_g, w_in, fox_b_f, fox_q_norm_g, fox_k_norm_g, gla_w_gate2, gla_b_gate,
        gla_out_norm_g.shape[0])
    gla = _gla(gq, gk, gv, gr, gl, gla_out_norm_g)
    fox = _fox(qa, ka, va, head_dim)
    km, vm = _mem_kv(mem, norm_mem_g, xattn_wkv, xattn_k_norm_g)
    h = _mix(x, fox, gla, w_out, head_dim, norm_xattn_g, xattn_wq, xattn_q_norm_g, km, vm, xattn_wo)
    return _mlp(h, norm_mlp_g, mlp_w1, mlp_w2)
```

```python
import functools
import math

import numpy as np
import jax
import jax.numpy as jnp
from jax import lax
from jax.experimental import pallas as pl
from jax.experimental.pallas import tpu as pltpu

F32 = jnp.float32
BF16 = jnp.bfloat16

EPS = 1e-6
LOG2E = 1.4426950408889634

GLA_CHUNK = 64
GLA_HEADS = 4
GLA_TAU = 16.0

LANES = 128
MXU_K = 256
VMEM_LIMIT_BYTES = 56 * 1024 * 1024

ROW_TILE = 512
FOX_TILE = 512
FOX_SUB = 4
AUG0 = 64


def _const_spec(shape):
    return pl.BlockSpec(shape, lambda *_: (0,) * len(shape))


def _params(semantics):
    return pltpu.CompilerParams(dimension_semantics=semantics,
                                vmem_limit_bytes=VMEM_LIMIT_BYTES)


def _split3(c):
    hi = c.astype(BF16)
    r = c - hi.astype(F32)
    mid = r.astype(BF16)
    lo = (r - mid.astype(F32)).astype(BF16)
    return hi, mid, lo


def _log_sigmoid(z):
    return -(jnp.maximum(-z, 0.0) + jnp.log1p(jnp.exp(-jnp.abs(z))))


def _dot(a, b):
    return jnp.dot(a, b, preferred_element_type=F32)


def _dot_nt(a, b):
    return lax.dot_general(a, b, (((1,), (1,)), ((), ())), preferred_element_type=F32)


def _dot_tn(a, b):
    return lax.dot_general(a, b, (((0,), (0,)), ((), ())), preferred_element_type=F32)


def _in_proj_kernel(x_ref, g_ref, w_ref, ones_ref, tri_ref, pq_ref, pk_ref, qg_ref, kg_ref,
                    bf_ref, wg2_ref, bg_ref, e64_ref,
                    qa_ref, ka_ref, va_ref, gq_ref, gk_ref, gv_ref, gr_ref, gl_ref,
                    carry_ref, *, fox_w, gkw, gvw, n_fox, head_dim):
    @pl.when(pl.program_id(1) == 0)
    def _():
        carry_ref[...] = jnp.zeros_like(carry_ref)

    x = x_ref[...]
    xn = (x * lax.rsqrt(jnp.mean(x * x, axis=-1, keepdims=True) + EPS) * g_ref[...]).astype(BF16)

    o_k = fox_w
    o_v = 2 * fox_w
    o_gq = 3 * fox_w
    o_gk = o_gq + gkw
    o_gv = o_gk + gkw
    o_gr = o_gv + gvw
    o_sm = o_gr + gvw

    def proj(lo, width):
        return _dot(xn, w_ref[:, lo:lo + width])

    def head_norm(f, gain_row):
        ssq = _dot((f * f).astype(BF16), ones_ref[...])
        return f * lax.rsqrt(ssq * (1.0 / head_dim) + EPS) * gain_row

    def spread(f):
        cols = []
        for h in range(n_fox):
            c = f[:, (h // 2) * LANES:(h // 2 + 1) * LANES]
            if h % 2:
                c = pltpu.roll(c, LANES // 2, axis=1)
            cols.append(c)
        return jnp.concatenate(cols, axis=1)

    sm = proj(o_sm, LANES)
    lane = lax.broadcasted_iota(jnp.int32, sm.shape, 1)
    lf = jnp.where(lane < n_fox, _log_sigmoid(sm + bf_ref[...]), 0.0)
    hi, mid, lo = _split3(lf)
    tri = tri_ref[...]
    c = _dot(tri, hi) + _dot(tri, mid) + _dot(tri, lo) + carry_ref[...]
    tm = c.shape[0]
    carry_ref[...] = c[tm - 1:tm, :]
    hi, mid, lo = _split3(c * LOG2E)
    cpk = (hi.astype(F32) + pltpu.roll(mid.astype(F32), n_fox, axis=1)
           + pltpu.roll(lo.astype(F32), 2 * n_fox, axis=1))
    cpk = jnp.where(lane == 3 * n_fox, 1.0, cpk).astype(BF16)

    wide_lane = lax.broadcasted_iota(jnp.int32, (1, n_fox * LANES), 1)
    is_head_dim = (wide_lane & (LANES - 1)) < head_dim

    fq = head_norm(proj(0, fox_w), qg_ref[...])
    qa_ref[...] = jnp.where(is_head_dim, spread(fq), _dot(cpk, pq_ref[...])).astype(BF16)
    fk = head_norm(proj(o_k, fox_w), kg_ref[...])
    ka_ref[...] = jnp.where(is_head_dim, spread(fk), _dot(cpk, pk_ref[...])).astype(BF16)
    fv = proj(o_v, fox_w)
    va_ref[...] = jnp.where(is_head_dim, spread(fv), e64_ref[...]).astype(BF16)

    gq_ref[...] = proj(o_gq, gkw).astype(BF16)
    gk_ref[...] = proj(o_gk, gkw).astype(BF16)
    gv_ref[...] = proj(o_gv, gvw).astype(BF16)
    gr_ref[...] = proj(o_gr, gvw).astype(BF16)
    gate = _dot(sm.astype(BF16), wg2_ref[...]) + bg_ref[...]
    gl_ref[...] = _log_sigmoid(gate) * (1.0 / GLA_TAU)


def _in_proj(x, norm_g, w_in, fox_b_f, fox_q_norm_g, fox_k_norm_g, gla_w_gate2, gla_b_gate, gvw):
    B, S, D = x.shape
    n_fox = fox_b_f.shape[0]
    head_dim = fox_q_norm_g.shape[0]
    fox_w = n_fox * head_dim
    rank, gkw = gla_w_gate2.shape
    tm = min(ROW_TILE, S)
    assert S % tm == 0 and head_dim == LANES // 2 and n_fox % 2 == 0
    assert 3 * n_fox + 1 <= LANES and n_fox + rank <= LANES

    sizes = (fox_w, fox_w, fox_w, n_fox, gkw, gkw, gvw, rank, gvw)
    offs = np.concatenate([[0], np.cumsum(sizes)])
    wq, wk, wv, wf, wgq, wgk, wgv, wlr, wgr = [w_in[:, offs[i]:offs[i + 1]] for i in range(9)]
    w_small = jnp.zeros((D, LANES), F32).at[:, :n_fox].set(wf).at[:, n_fox:n_fox + rank].set(wlr)
    w_all = jnp.concatenate([wq, wk, wv, wgq, wgk, wgv, wgr, w_small], axis=1).astype(BF16)
    width = w_all.shape[1]

    scale = (1.0 / math.sqrt(head_dim)) * LOG2E
    qg = (jnp.tile(fox_q_norm_g, n_fox) * scale).reshape(1, fox_w)
    kg = jnp.tile(fox_k_norm_g, n_fox).reshape(1, fox_w)
    bf = jnp.zeros((1, LANES), F32).at[0, :n_fox].set(fox_b_f)
    wg2 = jnp.zeros((LANES, gkw), F32).at[n_fox:n_fox + rank].set(gla_w_gate2).astype(BF16)
    bg = gla_b_gate.reshape(1, gkw)

    grp = np.arange(fox_w) // head_dim
    ones_blk = jnp.asarray(grp[:, None] == grp[None, :], BF16)
    r = np.arange(tm)
    tri = jnp.asarray(r[None, :] <= r[:, None], BF16)
    pq = np.zeros((LANES, n_fox * LANES), np.float32)
    pk = np.zeros((LANES, n_fox * LANES), np.float32)
    e64 = np.zeros((1, n_fox * LANES), np.float32)
    for h in range(n_fox):
        base = h * LANES + AUG0
        for part in range(3):
            pq[part * n_fox + h, base + part] = 1.0
            pk[part * n_fox + h, base + 3 + part] = -1.0
            pq[3 * n_fox, base + 3 + part] = 1.0
            pk[3 * n_fox, base + part] = 1.0
        e64[0, base] = 1.0
    pq = jnp.asarray(pq, BF16)
    pk = jnp.asarray(pk, BF16)
    e64 = jnp.asarray(e64)

    row = lambda c: pl.BlockSpec((None, tm, c), lambda b, i: (b, i, 0))
    kern = functools.partial(_in_proj_kernel, fox_w=fox_w, gkw=gkw, gvw=gvw, n_fox=n_fox,
                             head_dim=head_dim)
    aug_w = n_fox * LANES
    out_shapes = [jax.ShapeDtypeStruct((B, S, aug_w), BF16)] * 3 + [
        jax.ShapeDtypeStruct((B, S, gkw), BF16), jax.ShapeDtypeStruct((B, S, gkw), BF16),
        jax.ShapeDtypeStruct((B, S, gvw), BF16), jax.ShapeDtypeStruct((B, S, gvw), BF16),
        jax.ShapeDtypeStruct((B, S, gkw), F32)]
    return pl.pallas_call(
        kern,
        out_shape=out_shapes,
        grid=(B, S // tm),
        in_specs=[row(D), _const_spec((1, D)), _const_spec((D, width)),
                  _const_spec((fox_w, fox_w)), _const_spec((tm, tm)),
                  _const_spec((LANES, aug_w)), _const_spec((LANES, aug_w)),
                  _const_spec((1, fox_w)), _const_spec((1, fox_w)), _const_spec((1, LANES)),
                  _const_spec((LANES, gkw)), _const_spec((1, gkw)), _const_spec((1, aug_w))],
        out_specs=[row(aug_w), row(aug_w), row(aug_w), row(gkw), row(gkw), row(gvw), row(gvw),
                   row(gkw)],
        scratch_shapes=[pltpu.VMEM((1, LANES), F32)],
        compiler_params=_params(("arbitrary", "arbitrary")),
        name="in_proj",
    )(x, norm_g.reshape(1, D), w_all, ones_blk, tri, pq, pk, qg, kg, bf, wg2, bg, e64)


def _gla_kernel(gq_ref, gk_ref, gv_ref, gr_ref, gl_ref, trib_ref, gain_ref, out_ref,
                st_ref, o_ref, *, dk, dv):
    @pl.when(pl.program_id(1) == 0)
    def _():
        st_ref[...] = jnp.zeros_like(st_ref)

    T, kw = gl_ref.shape
    vw = gv_ref.shape[1]
    n_heads = kw // dk
    C = GLA_CHUNK

    g = gl_ref[...]
    hi, mid, lo = _split3(g)
    trib = trib_ref[...]
    bc = _dot(trib, hi) + _dot(trib, mid) + _dot(trib, lo)
    k = gk_ref[...].astype(F32)
    q_dec = gq_ref[...].astype(F32) * (dk ** -0.5) * jnp.exp(bc)
    k_dec = (k * jnp.exp(-bc)).astype(BF16)

    klane = lax.broadcasted_iota(jnp.int32, (1, kw), 1) // dk
    row_h = lax.broadcasted_iota(jnp.int32, (n_heads * C, C), 0)
    col = lax.broadcasted_iota(jnp.int32, (n_heads * C, C), 1)
    tril = col <= (row_h & (C - 1))
    bd = (lax.broadcasted_iota(jnp.int32, (vw, kw), 0) // dv
          == lax.broadcasted_iota(jnp.int32, (vw, kw), 1) // dk)

    for c in range(T // C):
        r0 = c * C
        bcc = bc[r0:r0 + C]
        b_last = bcc[C - 1:C]
        qd = q_dec[r0:r0 + C]
        lhs = jnp.concatenate([jnp.where(klane == h, qd, 0.0) for h in range(n_heads)],
                              axis=0).astype(BF16)
        a = jnp.where(tril, _dot_nt(lhs, k_dec[r0:r0 + C]), 0.0).astype(BF16)
        v = gv_ref[r0:r0 + C, :]
        o_intra = jnp.concatenate(
            [_dot(a[h * C:(h + 1) * C], v[:, h * dv:(h + 1) * dv]) for h in range(n_heads)],
            axis=1)
        st = st_ref[...]
        o_inter = _dot_nt(qd.astype(BF16), st.astype(BF16))
        ks = (k[r0:r0 + C] * jnp.exp(b_last - bcc)).astype(BF16)
        d_st = _dot_tn(v, ks)
        st_ref[...] = st * jnp.exp(b_last) + jnp.where(bd, d_st, 0.0)
        o_ref[r0:r0 + C, :] = o_intra + o_inter

    o = o_ref[...]
    gr = gr_ref[...].astype(F32)
    gain = gain_ref[...]
    outs = []
    for h in range(n_heads):
        oh = o[:, h * dv:(h + 1) * dv]
        on = oh * lax.rsqrt(jnp.mean(oh * oh, axis=-1, keepdims=True) + EPS)
        outs.append(on * gain[:, h * dv:(h + 1) * dv])
    out_ref[...] = (jnp.concatenate(outs, axis=1) * (gr * jax.nn.sigmoid(gr))).astype(BF16)


def _gla(gq, gk, gv, gr, gl, out_norm_g):
    B, S, kw = gq.shape
    vw = gv.shape[2]
    dk, dv = kw // GLA_HEADS, vw // GLA_HEADS
    T = min(ROW_TILE, S)
    assert S % T == 0 and T % GLA_CHUNK == 0
    r = np.arange(T)
    trib = jnp.asarray((r[None, :] <= r[:, None]) & (r[None, :] // GLA_CHUNK == r[:, None] // GLA_CHUNK),
                       BF16)
    row = lambda c: pl.BlockSpec((None, T, c), lambda b, i: (b, i, 0))
    return pl.pallas_call(
        functools.partial(_gla_kernel, dk=dk, dv=dv),
        out_shape=jax.ShapeDtypeStruct((B, S, vw), BF16),
        grid=(B, S // T),
        in_specs=[row(kw), row(kw), row(vw), row(vw), row(kw), _const_spec((T, T)),
                  _const_spec((1, vw))],
        out_specs=row(vw),
        scratch_shapes=[pltpu.VMEM((vw, kw), F32), pltpu.VMEM((T, vw), F32)],
        compiler_params=_params(("arbitrary", "arbitrary")),
        name="gla",
    )(gq, gk, gv, gr, gl, trib, out_norm_g.reshape(1, vw))


def _fox_kernel(q_ref, k_ref, v_ref, o_ref, m_ref, acc_ref, s_a, s_b, p_a, p_b, al_a, al_b,
                *, head_dim, n_sub):
    T = q_ref.shape[0]
    R = T // n_sub
    i = pl.program_id(2)
    m_ref[...] = jnp.full_like(m_ref, -jnp.inf)
    acc_ref[...] = jnp.zeros_like(acc_ref)
    s_buf, p_buf, al_buf = (s_a, s_b), (p_a, p_b), (al_a, al_b)

    def scores(b, par):
        start = pl.multiple_of(b * T, T)
        s_buf[par][...] = _dot_nt(q_ref[...], k_ref[pl.ds(start, T), :])

    def softmax(par, diag):
        for r in range(n_sub):
            rows = slice(r * R, (r + 1) * R)
            s = s_buf[par][rows, :]
            if diag:
                row = lax.broadcasted_iota(jnp.int32, (R, T), 0) + r * R
                col = lax.broadcasted_iota(jnp.int32, (R, T), 1)
                s = jnp.where(col <= row, s, -jnp.inf)
            m_prev = m_ref[rows, :]
            m_cur = jnp.broadcast_to(jnp.max(s, axis=-1, keepdims=True), (R, LANES))
            m_new = jnp.maximum(m_prev, m_cur)
            for g in range(T // LANES):
                cols = slice(g * LANES, (g + 1) * LANES)
                p_buf[par][rows, cols] = jnp.exp2(s[:, cols] - m_new).astype(BF16)
            al_buf[par][rows, :] = jnp.exp2(m_prev - m_new)
            m_ref[rows, :] = m_new

    def pv(b, par):
        start = pl.multiple_of(b * T, T)
        pvs = [_dot(p_buf[par][:, c:c + MXU_K], v_ref[pl.ds(start + c, MXU_K), :])
               for c in range(0, T, MXU_K)]
        acc_ref[...] = al_buf[par][...] * acc_ref[...] + sum(pvs[1:], pvs[0])

    scores(0, 0)

    @pl.when(i >= 1)
    def _():
        scores(1, 1)
        softmax(0, False)

    def pair(u, carry):
        t = 2 * u + 2
        pv(t - 2, 0)
        scores(t, 0)
        softmax(1, False)
        pv(t - 1, 1)
        scores(t + 1, 1)
        softmax(0, False)
        return carry

    lax.fori_loop(0, (i - 1) // 2, pair, 0)

    @pl.when(i % 2 == 1)
    def _():
        pv(i - 1, 0)
        softmax(1, True)
        pv(i, 1)

    @pl.when((i % 2 == 0) & (i >= 2))
    def _():
        pv(i - 2, 0)
        scores(i, 0)
        softmax(1, False)
        pv(i - 1, 1)

    @pl.when(i % 2 == 0)
    def _():
        softmax(0, True)
        pv(i, 0)

    acc = acc_ref[...]
    lane = lax.broadcasted_iota(jnp.int32, acc.shape, 1)
    inv_l = 1.0 / acc[:, head_dim:head_dim + 1]
    o_ref[...] = jnp.where(lane < head_dim, acc * inv_l, 0.0).astype(BF16)


def _fox(qa, ka, va, head_dim):
    B, S, W = qa.shape
    H = W // LANES
    T = min(FOX_TILE, S)
    assert S % T == 0
    return pl.pallas_call(
        functools.partial(_fox_kernel, head_dim=head_dim, n_sub=FOX_SUB),
        out_shape=jax.ShapeDtypeStruct((B, S, W), BF16),
        grid=(B, H, S // T),
        in_specs=[pl.BlockSpec((None, T, LANES), lambda b, h, i: (b, i, h)),
                  pl.BlockSpec((None, S, LANES), lambda b, h, i: (b, 0, h)),
                  pl.BlockSpec((None, S, LANES), lambda b, h, i: (b, 0, h))],
        out_specs=pl.BlockSpec((None, T, LANES), lambda b, h, i: (b, i, h)),
        scratch_shapes=[pltpu.VMEM((T, LANES), F32), pltpu.VMEM((T, LANES), F32),
                        pltpu.VMEM((T, T), F32), pltpu.VMEM((T, T), F32),
                        pltpu.VMEM((T, T), BF16), pltpu.VMEM((T, T), BF16),
                        pltpu.VMEM((T, LANES), F32), pltpu.VMEM((T, LANES), F32)],
        compiler_params=_params(("arbitrary", "arbitrary", "arbitrary")),
        name="fox",
    )(qa, ka, va)


def _mem_kv_kernel(mem_ref, g_ref, w_ref, kg_ref, k_ref, v_ref, *, xd):
    x = mem_ref[...]
    D = x.shape[1]
    xn = (x * lax.rsqrt(jnp.mean(x * x, axis=-1, keepdims=True) + EPS) * g_ref[...]).astype(BF16)
    kg = kg_ref[...]
    for h in range(D // xd):
        kh = _dot(xn, w_ref[:, h * xd:(h + 1) * xd])
        kn = kh * lax.rsqrt(jnp.mean(kh * kh, axis=-1, keepdims=True) + EPS) * kg
        k_ref[:, h * xd:(h + 1) * xd] = kn.astype(BF16)
    v_ref[...] = _dot(xn, w_ref[:, D:]).astype(BF16)


def _mem_kv(mem, norm_mem_g, wkv, k_norm_g):
    B, M, D = mem.shape
    xd = k_norm_g.shape[0]
    kg = (k_norm_g * (LOG2E / math.sqrt(xd))).reshape(1, xd)
    blk = pl.BlockSpec((None, M, D), lambda b: (b, 0, 0))
    return pl.pallas_call(
        functools.partial(_mem_kv_kernel, xd=xd),
        out_shape=[jax.ShapeDtypeStruct((B, M, D), BF16)] * 2,
        grid=(B,),
        in_specs=[blk, _const_spec((1, D)), _const_spec((D, 2 * D)), _const_spec((1, xd))],
        out_specs=[blk, blk],
        compiler_params=_params(("arbitrary",)),
        name="mem_kv",
    )(mem, norm_mem_g.reshape(1, D), wkv.astype(BF16), kg)


def _mix_kernel(x_ref, fox_ref, gla_ref, wof_ref, wog_ref, g_ref, wq_ref, qg_ref, km_ref, vm_ref,
                wo_ref, h_ref, *, xd):
    h = x_ref[...] + _dot(fox_ref[...], wof_ref[...]) + _dot(gla_ref[...], wog_ref[...])
    hn = (h * lax.rsqrt(jnp.mean(h * h, axis=-1, keepdims=True) + EPS) * g_ref[...]).astype(BF16)
    D = h.shape[1]
    qg = qg_ref[...]
    outs = []
    for hd in range(D // xd):
        sl = slice(hd * xd, (hd + 1) * xd)
        q = _dot(hn, wq_ref[:, sl])
        qn = (q * lax.rsqrt(jnp.mean(q * q, axis=-1, keepdims=True) + EPS) * qg).astype(BF16)
        s = _dot_nt(qn, km_ref[:, sl])
        p = jnp.exp2(s - jnp.max(s, axis=-1, keepdims=True))
        inv_l = 1.0 / jnp.sum(p, axis=-1, keepdims=True)
        outs.append((_dot(p.astype(BF16), vm_ref[:, sl]) * inv_l).astype(BF16))
    h_ref[...] = h + _dot(jnp.concatenate(outs, axis=1), wo_ref[...])


def _mix(x, fox, gla, w_out, head_dim, norm_g, wq, q_norm_g, km, vm, wo):
    B, S, D = x.shape
    fw = fox.shape[2]
    gw = gla.shape[2]
    M = km.shape[1]
    xd = q_norm_g.shape[0]
    n_fox = fw // LANES
    tm = min(ROW_TILE, S)
    wof = jnp.zeros((n_fox, LANES, D), F32).at[:, :head_dim].set(
        w_out[:n_fox * head_dim].reshape(n_fox, head_dim, D)).reshape(fw, D).astype(BF16)
    wog = w_out[n_fox * head_dim:].astype(BF16)
    row = lambda c: pl.BlockSpec((None, tm, c), lambda b, i: (b, i, 0))
    mem = pl.BlockSpec((None, M, D), lambda b, i: (b, 0, 0))
    return pl.pallas_call(
        functools.partial(_mix_kernel, xd=xd),
        out_shape=jax.ShapeDtypeStruct((B, S, D), F32),
        grid=(B, S // tm),
        in_specs=[row(D), row(fw), row(gw), _const_spec((fw, D)), _const_spec((gw, D)),
                  _const_spec((1, D)), _const_spec((D, D)), _const_spec((1, xd)), mem, mem,
                  _const_spec((D, D))],
        out_specs=row(D),
        compiler_params=_params(("arbitrary", "arbitrary")),
        name="mix",
    )(x, fox, gla, wof, wog, norm_g.reshape(1, D), wq.astype(BF16), q_norm_g.reshape(1, xd),
      km, vm, wo.astype(BF16))


def _mlp_kernel(h_ref, g_ref, w1_ref, w2_ref, y_ref, *, slab):
    h = h_ref[...]
    hn = (h * lax.rsqrt(jnp.mean(h * h, axis=-1, keepdims=True) + EPS) * g_ref[...]).astype(BF16)
    y = h
    for j in range(w1_ref.shape[1] // slab):
        u = jnp.maximum(_dot(hn, w1_ref[:, j * slab:(j + 1) * slab]), 0.0)
        y = y + _dot((u * u).astype(BF16), w2_ref[j * slab:(j + 1) * slab, :])
    y_ref[...] = y


def _mlp(h, norm_g, w1, w2):
    B, S, D = h.shape
    F = w1.shape[1]
    tm = min(ROW_TILE, S)
    row = pl.BlockSpec((None, tm, D), lambda b, i: (b, i, 0))
    return pl.pallas_call(
        functools.partial(_mlp_kernel, slab=min(F, 1024)),
        out_shape=jax.ShapeDtypeStruct((B, S, D), F32),
        grid=(B, S // tm),
        in_specs=[row, _const_spec((1, D)), _const_spec((D, F)), _const_spec((F, D))],
        out_specs=row,
        compiler_params=_params(("arbitrary", "arbitrary")),
        name="mlp",
    )(h, norm_g.reshape(1, D), w1.astype(BF16), w2.astype(BF16))


def kernel(x, mem, norm_mix_g, w_in, fox_b_f, fox_q_norm_g, fox_k_norm_g, gla_w_gate2, gla_b_gate,
           gla_out_norm_g, w_out, norm_xattn_g, norm_mem_g, xattn_wq, xattn_wkv, xattn_q_norm_g,
           xattn_k_norm_g, xattn_wo, norm_mlp_g, mlp_w1, mlp_w2):
    head_dim = fox_q_norm_g.shape[0]
    qa, ka, va, gq, gk, gv, gr, gl = _in_proj(
        x, norm_mix_g, w_in, fox_b_f, fox_q_norm_g, fox_k_norm_g, gla_w_gate2, gla_b_gate,
        gla_out_norm_g.shape[0])
    gla = _gla(gq, gk, gv, gr, gl, gla_out_norm_g)
    fox = _fox(qa, ka, va, head_dim)
    km, vm = _mem_kv(mem, norm_mem_g, xattn_wkv, xattn_k_norm_g)
    h = _mix(x, fox, gla, w_out, head_dim, norm_xattn_g, xattn_wq, xattn_q_norm_g, km, vm, xattn_wo)
    return _mlp(h, norm_mlp_g, mlp_w1, mlp_w2)
```

```python
import functools
import math

import numpy as np
import jax
import jax.numpy as jnp
from jax import lax
from jax.experimental import pallas as pl
from jax.experimental.pallas import tpu as pltpu

F32 = jnp.float32
BF16 = jnp.bfloat16

EPS = 1e-6
LOG2E = 1.4426950408889634

GLA_CHUNK = 64
GLA_HEADS = 4
GLA_TAU = 16.0

LANES = 128
MXU_K = 256
VMEM_LIMIT_BYTES = 56 * 1024 * 1024

ROW_TILE = 512
FOX_TILE = 512
FOX_UNROLL = 8
FOX_DIAG_UNROLL = 4
AUG0 = 64


def _const_spec(shape):
    return pl.BlockSpec(shape, lambda *_: (0,) * len(shape))


def _params(semantics, flags=None):
    return pltpu.CompilerParams(dimension_semantics=semantics,
                                vmem_limit_bytes=VMEM_LIMIT_BYTES, flags=flags)


def _split3(c):
    hi = c.astype(BF16)
    r = c - hi.astype(F32)
    mid = r.astype(BF16)
    lo = (r - mid.astype(F32)).astype(BF16)
    return hi, mid, lo


def _log_sigmoid(z):
    return -(jnp.maximum(-z, 0.0) + jnp.log1p(jnp.exp(-jnp.abs(z))))


def _dot(a, b):
    return jnp.dot(a, b, preferred_element_type=F32)


def _dot_nt(a, b):
    return lax.dot_general(a, b, (((1,), (1,)), ((), ())), preferred_element_type=F32)


def _dot_tn(a, b):
    return lax.dot_general(a, b, (((0,), (0,)), ((), ())), preferred_element_type=F32)


def _in_proj_kernel(x_ref, g_ref, w_ref, ones_ref, tri_ref, pq_ref, pk_ref, qg_ref, kg_ref,
                    bf_ref, wg2_ref, bg_ref, e64_ref,
                    qa_ref, ka_ref, va_ref, gq_ref, gk_ref, gv_ref, gr_ref, gl_ref,
                    carry_ref, *, fox_w, gkw, gvw, n_fox, head_dim):
    @pl.when(pl.program_id(1) == 0)
    def _():
        carry_ref[...] = jnp.zeros_like(carry_ref)

    x = x_ref[...]
    xn = (x * lax.rsqrt(jnp.mean(x * x, axis=-1, keepdims=True) + EPS) * g_ref[...]).astype(BF16)

    o_k = fox_w
    o_v = 2 * fox_w
    o_gq = 3 * fox_w
    o_gk = o_gq + gkw
    o_gv = o_gk + gkw
    o_gr = o_gv + gvw
    o_sm = o_gr + gvw

    def proj(lo, width):
        return _dot(xn, w_ref[:, lo:lo + width])

    def head_norm(f, gain_row):
        ssq = _dot((f * f).astype(BF16), ones_ref[...])
        return f * lax.rsqrt(ssq * (1.0 / head_dim) + EPS) * gain_row

    def spread(f):
        cols = []
        for h in range(n_fox):
            c = f[:, (h // 2) * LANES:(h // 2 + 1) * LANES]
            if h % 2:
                c = pltpu.roll(c, LANES // 2, axis=1)
            cols.append(c)
        return jnp.concatenate(cols, axis=1)

    sm = proj(o_sm, LANES)
    lane = lax.broadcasted_iota(jnp.int32, sm.shape, 1)
    lf = jnp.where(lane < n_fox, _log_sigmoid(sm + bf_ref[...]), 0.0)
    hi, mid, lo = _split3(lf)
    tri = tri_ref[...]
    c = _dot(tri, hi) + _dot(tri, mid) + _dot(tri, lo) + carry_ref[...]
    tm = c.shape[0]
    carry_ref[...] = c[tm - 1:tm, :]
    hi, mid, lo = _split3(c * LOG2E)
    cpk = (hi.astype(F32) + pltpu.roll(mid.astype(F32), n_fox, axis=1)
           + pltpu.roll(lo.astype(F32), 2 * n_fox, axis=1))
    cpk = jnp.where(lane == 3 * n_fox, 1.0, cpk).astype(BF16)

    wide_lane = lax.broadcasted_iota(jnp.int32, (1, n_fox * LANES), 1)
    is_head_dim = (wide_lane & (LANES - 1)) < head_dim

    fq = head_norm(proj(0, fox_w), qg_ref[...])
    qa_ref[...] = jnp.where(is_head_dim, spread(fq), _dot(cpk, pq_ref[...])).astype(BF16)
    fk = head_norm(proj(o_k, fox_w), kg_ref[...])
    ka_ref[...] = jnp.where(is_head_dim, spread(fk), _dot(cpk, pk_ref[...])).astype(BF16)
    fv = proj(o_v, fox_w)
    va_ref[...] = jnp.where(is_head_dim, spread(fv), e64_ref[...]).astype(BF16)

    gq_ref[...] = proj(o_gq, gkw).astype(BF16)
    gk_ref[...] = proj(o_gk, gkw).astype(BF16)
    gv_ref[...] = proj(o_gv, gvw).astype(BF16)
    gr_ref[...] = proj(o_gr, gvw).astype(BF16)
    gate = _dot(sm.astype(BF16), wg2_ref[...]) + bg_ref[...]
    gl_ref[...] = _log_sigmoid(gate) * (1.0 / GLA_TAU)


def _in_proj(x, norm_g, w_in, fox_b_f, fox_q_norm_g, fox_k_norm_g, gla_w_gate2, gla_b_gate, gvw):
    B, S, D = x.shape
    n_fox = fox_b_f.shape[0]
    head_dim = fox_q_norm_g.shape[0]
    fox_w = n_fox * head_dim
    rank, gkw = gla_w_gate2.shape
    tm = min(ROW_TILE, S)
    assert S % tm == 0 and head_dim == LANES // 2 and n_fox % 2 == 0
    assert 3 * n_fox + 1 <= LANES and n_fox + rank <= LANES

    sizes = (fox_w, fox_w, fox_w, n_fox, gkw, gkw, gvw, rank, gvw)
    offs = np.concatenate([[0], np.cumsum(sizes)])
    wq, wk, wv, wf, wgq, wgk, wgv, wlr, wgr = [w_in[:, offs[i]:offs[i + 1]] for i in range(9)]
    w_small = jnp.zeros((D, LANES), F32).at[:, :n_fox].set(wf).at[:, n_fox:n_fox + rank].set(wlr)
    w_all = jnp.concatenate([wq, wk, wv, wgq, wgk, wgv, wgr, w_small], axis=1).astype(BF16)
    width = w_all.shape[1]

    scale = (1.0 / math.sqrt(head_dim)) * LOG2E
    qg = (jnp.tile(fox_q_norm_g, n_fox) * scale).reshape(1, fox_w)
    kg = jnp.tile(fox_k_norm_g, n_fox).reshape(1, fox_w)
    bf = jnp.zeros((1, LANES), F32).at[0, :n_fox].set(fox_b_f)
    wg2 = jnp.zeros((LANES, gkw), F32).at[n_fox:n_fox + rank].set(gla_w_gate2).astype(BF16)
    bg = gla_b_gate.reshape(1, gkw)

    grp = np.arange(fox_w) // head_dim
    ones_blk = jnp.asarray(grp[:, None] == grp[None, :], BF16)
    r = np.arange(tm)
    tri = jnp.asarray(r[None, :] <= r[:, None], BF16)
    pq = np.zeros((LANES, n_fox * LANES), np.float32)
    pk = np.zeros((LANES, n_fox * LANES), np.float32)
    e64 = np.zeros((1, n_fox * LANES), np.float32)
    for h in range(n_fox):
        base = h * LANES + AUG0
        for part in range(3):
            pq[part * n_fox + h, base + part] = 1.0
            pk[part * n_fox + h, base + 3 + part] = -1.0
            pq[3 * n_fox, base + 3 + part] = 1.0
            pk[3 * n_fox, base + part] = 1.0
        e64[0, base] = 1.0
    pq = jnp.asarray(pq, BF16)
    pk = jnp.asarray(pk, BF16)
    e64 = jnp.asarray(e64)

    row = lambda c: pl.BlockSpec((None, tm, c), lambda b, i: (b, i, 0))
    kern = functools.partial(_in_proj_kernel, fox_w=fox_w, gkw=gkw, gvw=gvw, n_fox=n_fox,
                             head_dim=head_dim)
    aug_w = n_fox * LANES
    out_shapes = [jax.ShapeDtypeStruct((B, S, aug_w), BF16)] * 3 + [
        jax.ShapeDtypeStruct((B, S, gkw), BF16), jax.ShapeDtypeStruct((B, S, gkw), BF16),
        jax.ShapeDtypeStruct((B, S, gvw), BF16), jax.ShapeDtypeStruct((B, S, gvw), BF16),
        jax.ShapeDtypeStruct((B, S, gkw), F32)]
    return pl.pallas_call(
        kern,
        out_shape=out_shapes,
        grid=(B, S // tm),
        in_specs=[row(D), _const_spec((1, D)), _const_spec((D, width)),
                  _const_spec((fox_w, fox_w)), _const_spec((tm, tm)),
                  _const_spec((LANES, aug_w)), _const_spec((LANES, aug_w)),
                  _const_spec((1, fox_w)), _const_spec((1, fox_w)), _const_spec((1, LANES)),
                  _const_spec((LANES, gkw)), _const_spec((1, gkw)), _const_spec((1, aug_w))],
        out_specs=[row(aug_w), row(aug_w), row(aug_w), row(gkw), row(gkw), row(gvw), row(gvw),
                   row(gkw)],
        scratch_shapes=[pltpu.VMEM((1, LANES), F32)],
        compiler_params=_params(("arbitrary", "arbitrary")),
        name="in_proj",
    )(x, norm_g.reshape(1, D), w_all, ones_blk, tri, pq, pk, qg, kg, bf, wg2, bg, e64)


def _gla_kernel(gq_ref, gk_ref, gv_ref, gr_ref, gl_ref, trib_ref, gain_ref, out_ref,
                st_ref, o_ref, *, dk, dv):
    @pl.when(pl.program_id(1) == 0)
    def _():
        st_ref[...] = jnp.zeros_like(st_ref)

    T, kw = gl_ref.shape
    vw = gv_ref.shape[1]
    n_heads = kw // dk
    C = GLA_CHUNK

    g = gl_ref[...]
    hi, mid, lo = _split3(g)
    trib = trib_ref[...]
    bc = _dot(trib, hi) + _dot(trib, mid) + _dot(trib, lo)
    k = gk_ref[...].astype(F32)
    q_dec = gq_ref[...].astype(F32) * (dk ** -0.5) * jnp.exp(bc)
    k_dec = (k * jnp.exp(-bc)).astype(BF16)

    klane = lax.broadcasted_iota(jnp.int32, (1, kw), 1) // dk
    row_h = lax.broadcasted_iota(jnp.int32, (n_heads * C, C), 0)
    col = lax.broadcasted_iota(jnp.int32, (n_heads * C, C), 1)
    tril = col <= (row_h & (C - 1))
    bd = (lax.broadcasted_iota(jnp.int32, (vw, kw), 0) // dv
          == lax.broadcasted_iota(jnp.int32, (vw, kw), 1) // dk)

    for c in range(T // C):
        r0 = c * C
        bcc = bc[r0:r0 + C]
        b_last = bcc[C - 1:C]
        qd = q_dec[r0:r0 + C]
        lhs = jnp.concatenate([jnp.where(klane == h, qd, 0.0) for h in range(n_heads)],
                              axis=0).astype(BF16)
        a = jnp.where(tril, _dot_nt(lhs, k_dec[r0:r0 + C]), 0.0).astype(BF16)
        v = gv_ref[r0:r0 + C, :]
        o_intra = jnp.concatenate(
            [_dot(a[h * C:(h + 1) * C], v[:, h * dv:(h + 1) * dv]) for h in range(n_heads)],
            axis=1)
        st = st_ref[...]
        o_inter = _dot_nt(qd.astype(BF16), st.astype(BF16))
        ks = (k[r0:r0 + C] * jnp.exp(b_last - bcc)).astype(BF16)
        d_st = _dot_tn(v, ks)
        st_ref[...] = st * jnp.exp(b_last) + jnp.where(bd, d_st, 0.0)
        o_ref[r0:r0 + C, :] = o_intra + o_inter

    o = o_ref[...]
    gr = gr_ref[...].astype(F32)
    gain = gain_ref[...]
    outs = []
    for h in range(n_heads):
        oh = o[:, h * dv:(h + 1) * dv]
        on = oh * lax.rsqrt(jnp.mean(oh * oh, axis=-1, keepdims=True) + EPS)
        outs.append(on * gain[:, h * dv:(h + 1) * dv])
    out_ref[...] = (jnp.concatenate(outs, axis=1) * (gr * jax.nn.sigmoid(gr))).astype(BF16)


def _gla(gq, gk, gv, gr, gl, out_norm_g):
    B, S, kw = gq.shape
    vw = gv.shape[2]
    dk, dv = kw // GLA_HEADS, vw // GLA_HEADS
    T = min(ROW_TILE, S)
    assert S % T == 0 and T % GLA_CHUNK == 0
    r = np.arange(T)
    trib = jnp.asarray((r[None, :] <= r[:, None]) & (r[None, :] // GLA_CHUNK == r[:, None] // GLA_CHUNK),
                       BF16)
    row = lambda c: pl.BlockSpec((None, T, c), lambda b, i: (b, i, 0))
    return pl.pallas_call(
        functools.partial(_gla_kernel, dk=dk, dv=dv),
        out_shape=jax.ShapeDtypeStruct((B, S, vw), BF16),
        grid=(B, S // T),
        in_specs=[row(kw), row(kw), row(vw), row(vw), row(kw), _const_spec((T, T)),
                  _const_spec((1, vw))],
        out_specs=row(vw),
        scratch_shapes=[pltpu.VMEM((vw, kw), F32), pltpu.VMEM((T, vw), F32)],
        compiler_params=_params(("arbitrary", "arbitrary")),
        name="gla",
    )(gq, gk, gv, gr, gl, trib, out_norm_g.reshape(1, vw))


def _fox_kernel(q_ref, k_ref, v_ref, o_ref, m_all, acc_all, s0, s1, p0, p1, al0, al1,
                *, head_dim, nq):
    T = s0.shape[0]
    s_buf, p_buf, al_buf = (s0, s1), (p0, p1), (al0, al1)
    depth = 3

    def rows(blk):
        start = blk * T
        return pl.ds(start if isinstance(blk, int) else pl.multiple_of(start, T), T)

    m_all[rows(nq), :] = jnp.zeros((T, LANES), F32)
    acc_all[rows(nq), :] = jnp.zeros((T, LANES), F32)
    for ref in (s1, p0, al0):
        ref[...] = jnp.zeros_like(ref)

    def scores(qi, kj, par, diag):
        qr = jnp.minimum(qi, nq - 1)
        s = _dot_nt(q_ref[rows(qr), :], k_ref[rows(kj), :])
        if diag:
            row = lax.broadcasted_iota(jnp.int32, (T, T), 0)
            col = lax.broadcasted_iota(jnp.int32, (T, T), 1)
            s = jnp.where(col <= row, s, -jnp.inf)
        s_buf[par][...] = s

    def softmax(qi, par, diag):
        m_cur = jnp.broadcast_to(jnp.max(s_buf[par][...], axis=-1, keepdims=True), (T, LANES))
        if diag:
            m_new = m_cur
        else:
            m_prev = m_all[rows(qi), :]
            m_new = jnp.maximum(m_prev, m_cur)
            al_buf[par][...] = jnp.exp2(m_prev - m_new)
        m_all[rows(qi), :] = m_new
        for g in range(T // LANES):
            cols = slice(g * LANES, (g + 1) * LANES)
            p_buf[par][:, cols] = jnp.exp2(s_buf[par][:, cols] - m_new).astype(BF16)

    def pv(qi, kj, par, diag):
        start = pl.multiple_of(kj * T, T)
        pvs = [_dot(p_buf[par][:, c:c + MXU_K], v_ref[pl.ds(start + c, MXU_K), :])
               for c in range(0, T, MXU_K)]
        out = sum(pvs[1:], pvs[0])
        if not diag:
            out = al_buf[par][...] * acc_all[rows(qi), :] + out
        acc_all[rows(qi), :] = out

    def step(c, blocks, diag):
        (q0, k0), (q1, _), (q2, k2) = blocks
        pv(q2, k2, c % 2, diag)
        scores(q0, k0, c % 2, diag)
        softmax(q1, (c - 1) % 2, diag)

    def run(n_blocks, first, advance, diag, steps_per_trip):
        spare = (jnp.int32(nq), jnp.int32(0))

        def trip(u, carry):
            nxt, hist = carry[:2], [carry[2:4], carry[4:6]]
            for c in range(steps_per_trip):
                step(c, [nxt] + hist, diag)
                hist = [nxt] + hist[:1]
                nxt = advance(*nxt)
            return tuple(nxt) + tuple(x for h in hist for x in h)

        n_steps = n_blocks + depth - 1
        lax.fori_loop(0, -(-n_steps // steps_per_trip), trip, tuple(first) + spare * 2)

    def next_diag(qi, kj):
        n = jnp.minimum(qi + 1, nq)
        return n, jnp.minimum(n, nq - 1)

    def next_lower(qi, kj):
        wrap = kj + 1 >= qi
        qn = jnp.where(wrap, jnp.minimum(qi + 1, nq), qi)
        return qn, jnp.where(wrap & (qi < nq), 0, jnp.minimum(kj + 1, nq - 1))

    run(nq, (jnp.int32(0), jnp.int32(0)), next_diag, True, FOX_DIAG_UNROLL)
    if nq > 1:
        run(nq * (nq - 1) // 2, (jnp.int32(1), jnp.int32(0)), next_lower, False, FOX_UNROLL)

    def finish(b, carry):
        acc = acc_all[rows(b), :]
        lane = lax.broadcasted_iota(jnp.int32, acc.shape, 1)
        inv_l = 1.0 / acc[:, head_dim:head_dim + 1]
        o_ref[rows(b), :] = jnp.where(lane < head_dim, acc * inv_l, 0.0).astype(BF16)
        return carry

    lax.fori_loop(0, nq, finish, 0)


def _fox(qa, ka, va, head_dim):
    B, S, W = qa.shape
    H = W // LANES
    T = min(FOX_TILE, S)
    assert S % T == 0
    nq = S // T
    head = pl.BlockSpec((None, S, LANES), lambda b, h: (b, 0, h))
    state = pltpu.VMEM(((nq + 1) * T, LANES), F32)
    return pl.pallas_call(
        functools.partial(_fox_kernel, head_dim=head_dim, nq=nq),
        out_shape=jax.ShapeDtypeStruct((B, S, W), BF16),
        grid=(B, H),
        in_specs=[head, head, head],
        out_specs=head,
        scratch_shapes=([state, state] + [pltpu.VMEM((T, T), F32)] * 2
                        + [pltpu.VMEM((T, T), BF16)] * 2 + [pltpu.VMEM((T, LANES), F32)] * 2),
        compiler_params=_params(("arbitrary", "arbitrary")),
        name="fox",
    )(qa, ka, va)


def _mem_kv_kernel(mem_ref, g_ref, w_ref, kg_ref, k_ref, v_ref, *, xd):
    x = mem_ref[...]
    D = x.shape[1]
    xn = (x * lax.rsqrt(jnp.mean(x * x, axis=-1, keepdims=True) + EPS) * g_ref[...]).astype(BF16)
    kg = kg_ref[...]
    for h in range(D // xd):
        kh = _dot(xn, w_ref[:, h * xd:(h + 1) * xd])
        kn = kh * lax.rsqrt(jnp.mean(kh * kh, axis=-1, keepdims=True) + EPS) * kg
        k_ref[:, h * xd:(h + 1) * xd] = kn.astype(BF16)
    v_ref[...] = _dot(xn, w_ref[:, D:]).astype(BF16)


def _mem_kv(mem, norm_mem_g, wkv, k_norm_g):
    B, M, D = mem.shape
    xd = k_norm_g.shape[0]
    kg = (k_norm_g * (LOG2E / math.sqrt(xd))).reshape(1, xd)
    blk = pl.BlockSpec((None, M, D), lambda b: (b, 0, 0))
    return pl.pallas_call(
        functools.partial(_mem_kv_kernel, xd=xd),
        out_shape=[jax.ShapeDtypeStruct((B, M, D), BF16)] * 2,
        grid=(B,),
        in_specs=[blk, _const_spec((1, D)), _const_spec((D, 2 * D)), _const_spec((1, xd))],
        out_specs=[blk, blk],
        compiler_params=_params(("arbitrary",)),
        name="mem_kv",
    )(mem, norm_mem_g.reshape(1, D), wkv.astype(BF16), kg)


def _mix_kernel(x_ref, fox_ref, gla_ref, wof_ref, wog_ref, g_ref, wq_ref, qg_ref, km_ref, vm_ref,
                wo_ref, h_ref, *, xd):
    h = x_ref[...] + _dot(fox_ref[...], wof_ref[...]) + _dot(gla_ref[...], wog_ref[...])
    hn = (h * lax.rsqrt(jnp.mean(h * h, axis=-1, keepdims=True) + EPS) * g_ref[...]).astype(BF16)
    D = h.shape[1]
    qg = qg_ref[...]
    outs = []
    for hd in range(D // xd):
        sl = slice(hd * xd, (hd + 1) * xd)
        q = _dot(hn, wq_ref[:, sl])
        qn = (q * lax.rsqrt(jnp.mean(q * q, axis=-1, keepdims=True) + EPS) * qg).astype(BF16)
        s = _dot_nt(qn, km_ref[:, sl])
        p = jnp.exp2(s - jnp.max(s, axis=-1, keepdims=True))
        inv_l = 1.0 / jnp.sum(p, axis=-1, keepdims=True)
        outs.append((_dot(p.astype(BF16), vm_ref[:, sl]) * inv_l).astype(BF16))
    h_ref[...] = h + _dot(jnp.concatenate(outs, axis=1), wo_ref[...])


def _mix(x, fox, gla, w_out, head_dim, norm_g, wq, q_norm_g, km, vm, wo):
    B, S, D = x.shape
    fw = fox.shape[2]
    gw = gla.shape[2]
    M = km.shape[1]
    xd = q_norm_g.shape[0]
    n_fox = fw // LANES
    tm = min(ROW_TILE, S)
    wof = jnp.zeros((n_fox, LANES, D), F32).at[:, :head_dim].set(
        w_out[:n_fox * head_dim].reshape(n_fox, head_dim, D)).reshape(fw, D).astype(BF16)
    wog = w_out[n_fox * head_dim:].astype(BF16)
    row = lambda c: pl.BlockSpec((None, tm, c), lambda b, i: (b, i, 0))
    mem = pl.BlockSpec((None, M, D), lambda b, i: (b, 0, 0))
    return pl.pallas_call(
        functools.partial(_mix_kernel, xd=xd),
        out_shape=jax.ShapeDtypeStruct((B, S, D), F32),
        grid=(B, S // tm),
        in_specs=[row(D), row(fw), row(gw), _const_spec((fw, D)), _const_spec((gw, D)),
                  _const_spec((1, D)), _const_spec((D, D)), _const_spec((1, xd)), mem, mem,
                  _const_spec((D, D))],
        out_specs=row(D),
        compiler_params=_params(("arbitrary", "arbitrary")),
        name="mix",
    )(x, fox, gla, wof, wog, norm_g.reshape(1, D), wq.astype(BF16), q_norm_g.reshape(1, xd),
      km, vm, wo.astype(BF16))


def _mlp_kernel(h_ref, g_ref, w1_ref, w2_ref, y_ref, *, slab):
    h = h_ref[...]
    hn = (h * lax.rsqrt(jnp.mean(h * h, axis=-1, keepdims=True) + EPS) * g_ref[...]).astype(BF16)
    y = h
    for j in range(w1_ref.shape[1] // slab):
        u = jnp.maximum(_dot(hn, w1_ref[:, j * slab:(j + 1) * slab]), 0.0)
        y = y + _dot((u * u).astype(BF16), w2_ref[j * slab:(j + 1) * slab, :])
    y_ref[...] = y


def _mlp(h, norm_g, w1, w2):
    B, S, D = h.shape
    F = w1.shape[1]
    tm = min(ROW_TILE, S)
    row = pl.BlockSpec((None, tm, D), lambda b, i: (b, i, 0))
    return pl.pallas_call(
        functools.partial(_mlp_kernel, slab=min(F, 1024)),
        out_shape=jax.ShapeDtypeStruct((B, S, D), F32),
        grid=(B, S // tm),
        in_specs=[row, _const_spec((1, D)), _const_spec((D, F)), _const_spec((F, D))],
        out_specs=row,
        compiler_params=_params(("arbitrary", "arbitrary")),
        name="mlp",
    )(h, norm_g.reshape(1, D), w1.astype(BF16), w2.astype(BF16))


def kernel(x, mem, norm_mix_g, w_in, fox_b_f, fox_q_norm_g, fox_k_norm_g, gla_w_gate2, gla_b_gate,
           gla_out_norm_g, w_out, norm_xattn_g, norm_mem_g, xattn_wq, xattn_wkv, xattn_q_norm_g,
           xattn_k_norm_g, xattn_wo, norm_mlp_g, mlp_w1, mlp_w2):
    head_dim = fox_q_norm_g.shape[0]
    qa, ka, va, gq, gk, gv, gr, gl = _in_proj(
        x, norm_mix_g, w_in, fox_b_f, fox_q_norm_g, fox_k_norm_g, gla_w_gate2, gla_b_gate,
        gla_out_norm_g.shape[0])
    gla = _gla(gq, gk, gv, gr, gl, gla_out_norm_g)
    fox = _fox(qa, ka, va, head_dim)
    km, vm = _mem_kv(mem, norm_mem_g, xattn_wkv, xattn_k_norm_g)
    h = _mix(x, fox, gla, w_out, head_dim, norm_xattn_g, xattn_wq, xattn_q_norm_g, km, vm, xattn_wo)
    return _mlp(h, norm_mlp_g, mlp_w1, mlp_w2)
```

```python
import functools
import math

import numpy as np
import jax
import jax.numpy as jnp
from jax import lax
from jax.experimental import pallas as pl
from jax.experimental.pallas import tpu as pltpu

F32 = jnp.float32
BF16 = jnp.bfloat16

EPS = 1e-6
LOG2E = 1.4426950408889634

GLA_CHUNK = 64
GLA_HEADS = 4
GLA_TAU = 16.0

LANES = 128
MXU_K = 256
VMEM_LIMIT_BYTES = 56 * 1024 * 1024

ROW_TILE = 512
MIX_SPLITS = 2
FOX_TILE = 512
FOX_UNROLL = 8
FOX_DIAG_UNROLL = 4
AUG0 = 64


def _const_spec(shape):
    return pl.BlockSpec(shape, lambda *_: (0,) * len(shape))


def _params(semantics, flags=None):
    return pltpu.CompilerParams(dimension_semantics=semantics,
                                vmem_limit_bytes=VMEM_LIMIT_BYTES, flags=flags)


def _split3(c):
    hi = c.astype(BF16)
    r = c - hi.astype(F32)
    mid = r.astype(BF16)
    lo = (r - mid.astype(F32)).astype(BF16)
    return hi, mid, lo


def _log_sigmoid(z):
    return -(jnp.maximum(-z, 0.0) + jnp.log1p(jnp.exp(-jnp.abs(z))))


def _dot(a, b):
    return jnp.dot(a, b, preferred_element_type=F32)


def _dot_nt(a, b):
    return lax.dot_general(a, b, (((1,), (1,)), ((), ())), preferred_element_type=F32)


def _dot_tn(a, b):
    return lax.dot_general(a, b, (((0,), (0,)), ((), ())), preferred_element_type=F32)


def _in_proj_kernel(x_ref, g_ref, w_ref, ones_ref, tri_ref, pq_ref, pk_ref, qg_ref, kg_ref,
                    bf_ref, wg2_ref, bg_ref, e64_ref,
                    qa_ref, ka_ref, va_ref, gq_ref, gk_ref, gv_ref, gr_ref, gl_ref,
                    carry_ref, *, fox_w, gkw, gvw, n_fox, head_dim):
    @pl.when(pl.program_id(1) == 0)
    def _():
        carry_ref[...] = jnp.zeros_like(carry_ref)

    x = x_ref[...]
    xn = (x * lax.rsqrt(jnp.mean(x * x, axis=-1, keepdims=True) + EPS) * g_ref[...]).astype(BF16)

    o_k = fox_w
    o_v = 2 * fox_w
    o_gq = 3 * fox_w
    o_gk = o_gq + gkw
    o_gv = o_gk + gkw
    o_gr = o_gv + gvw
    o_sm = o_gr + gvw

    def proj(lo, width):
        return _dot(xn, w_ref[:, lo:lo + width])

    def head_norm(f, gain_row):
        ssq = _dot((f * f).astype(BF16), ones_ref[...])
        return f * lax.rsqrt(ssq * (1.0 / head_dim) + EPS) * gain_row

    def spread(f):
        cols = []
        for h in range(n_fox):
            c = f[:, (h // 2) * LANES:(h // 2 + 1) * LANES]
            if h % 2:
                c = pltpu.roll(c, LANES // 2, axis=1)
            cols.append(c)
        return jnp.concatenate(cols, axis=1)

    gr_sm = proj(o_gr, gvw + LANES)
    sm = gr_sm[:, gvw:]
    lane = lax.broadcasted_iota(jnp.int32, sm.shape, 1)
    lf = jnp.where(lane < n_fox, _log_sigmoid(sm + bf_ref[...]), 0.0)
    fq = proj(0, fox_w)
    fk = proj(o_k, fox_w)
    fq = head_norm(fq, qg_ref[...])
    fv = proj(o_v, fox_w)
    fk = head_norm(fk, kg_ref[...])

    def pack3(v):
        hi, mid, lo = _split3(v)
        return (hi.astype(F32) + pltpu.roll(mid.astype(F32), n_fox, axis=1)
                + pltpu.roll(lo.astype(F32), 2 * n_fox, axis=1))

    cs = _dot(tri_ref[...], pack3(lf).astype(BF16))
    cs = cs + pltpu.roll(cs, LANES - n_fox, axis=1) + pltpu.roll(cs, LANES - 2 * n_fox, axis=1)
    c = jnp.where(lane < n_fox, cs, 0.0) + carry_ref[...]
    tm = c.shape[0]
    carry_ref[...] = c[tm - 1:tm, :]
    cpk = jnp.where(lane == 3 * n_fox, 1.0, pack3(c * LOG2E)).astype(BF16)

    g_qk = proj(o_gq, 2 * gkw)
    gq_ref[...] = g_qk[:, :gkw].astype(BF16)
    gk_ref[...] = g_qk[:, gkw:].astype(BF16)
    gv_ref[...] = proj(o_gv, gvw).astype(BF16)
    gr_ref[...] = gr_sm[:, :gvw].astype(BF16)
    gate = _dot(sm.astype(BF16), wg2_ref[...]) + bg_ref[...]
    gl_ref[...] = _log_sigmoid(gate) * (1.0 / GLA_TAU)

    wide_lane = lax.broadcasted_iota(jnp.int32, (1, n_fox * LANES), 1)
    is_head_dim = (wide_lane & (LANES - 1)) < head_dim
    va_ref[...] = jnp.where(is_head_dim, spread(fv), e64_ref[...]).astype(BF16)
    qa_ref[...] = jnp.where(is_head_dim, spread(fq), _dot(cpk, pq_ref[...])).astype(BF16)
    ka_ref[...] = jnp.where(is_head_dim, spread(fk), _dot(cpk, pk_ref[...])).astype(BF16)


def _in_proj(x, norm_g, w_in, fox_b_f, fox_q_norm_g, fox_k_norm_g, gla_w_gate2, gla_b_gate, gvw):
    B, S, D = x.shape
    n_fox = fox_b_f.shape[0]
    head_dim = fox_q_norm_g.shape[0]
    fox_w = n_fox * head_dim
    rank, gkw = gla_w_gate2.shape
    tm = min(ROW_TILE, S)
    assert S % tm == 0 and head_dim == LANES // 2 and n_fox % 2 == 0
    assert 3 * n_fox + 1 <= LANES and n_fox + rank <= LANES

    sizes = (fox_w, fox_w, fox_w, n_fox, gkw, gkw, gvw, rank, gvw)
    offs = np.concatenate([[0], np.cumsum(sizes)])
    wq, wk, wv, wf, wgq, wgk, wgv, wlr, wgr = [w_in[:, offs[i]:offs[i + 1]] for i in range(9)]
    w_small = jnp.zeros((D, LANES), F32).at[:, :n_fox].set(wf).at[:, n_fox:n_fox + rank].set(wlr)
    w_all = jnp.concatenate([wq, wk, wv, wgq, wgk, wgv, wgr, w_small], axis=1).astype(BF16)
    width = w_all.shape[1]

    scale = (1.0 / math.sqrt(head_dim)) * LOG2E
    qg = (jnp.tile(fox_q_norm_g, n_fox) * scale).reshape(1, fox_w)
    kg = jnp.tile(fox_k_norm_g, n_fox).reshape(1, fox_w)
    bf = jnp.zeros((1, LANES), F32).at[0, :n_fox].set(fox_b_f)
    wg2 = jnp.zeros((LANES, gkw), F32).at[n_fox:n_fox + rank].set(gla_w_gate2).astype(BF16)
    bg = gla_b_gate.reshape(1, gkw)

    grp = np.arange(fox_w) // head_dim
    ones_blk = jnp.asarray(grp[:, None] == grp[None, :], BF16)
    r = np.arange(tm)
    tri = jnp.asarray(r[None, :] <= r[:, None], BF16)
    pq = np.zeros((LANES, n_fox * LANES), np.float32)
    pk = np.zeros((LANES, n_fox * LANES), np.float32)
    e64 = np.zeros((1, n_fox * LANES), np.float32)
    for h in range(n_fox):
        base = h * LANES + AUG0
        for part in range(3):
            pq[part * n_fox + h, base + part] = 1.0
            pk[part * n_fox + h, base + 3 + part] = -1.0
            pq[3 * n_fox, base + 3 + part] = 1.0
            pk[3 * n_fox, base + part] = 1.0
        e64[0, base] = 1.0
    pq = jnp.asarray(pq, BF16)
    pk = jnp.asarray(pk, BF16)
    e64 = jnp.asarray(e64)

    row = lambda c: pl.BlockSpec((None, tm, c), lambda b, i: (b, i, 0))
    kern = functools.partial(_in_proj_kernel, fox_w=fox_w, gkw=gkw, gvw=gvw, n_fox=n_fox,
                             head_dim=head_dim)
    aug_w = n_fox * LANES
    out_shapes = [jax.ShapeDtypeStruct((B, S, aug_w), BF16)] * 3 + [
        jax.ShapeDtypeStruct((B, S, gkw), BF16), jax.ShapeDtypeStruct((B, S, gkw), BF16),
        jax.ShapeDtypeStruct((B, S, gvw), BF16), jax.ShapeDtypeStruct((B, S, gvw), BF16),
        jax.ShapeDtypeStruct((B, S, gkw), F32)]
    return pl.pallas_call(
        kern,
        out_shape=out_shapes,
        grid=(B, S // tm),
        in_specs=[row(D), _const_spec((1, D)), _const_spec((D, width)),
                  _const_spec((fox_w, fox_w)), _const_spec((tm, tm)),
                  _const_spec((LANES, aug_w)), _const_spec((LANES, aug_w)),
                  _const_spec((1, fox_w)), _const_spec((1, fox_w)), _const_spec((1, LANES)),
                  _const_spec((LANES, gkw)), _const_spec((1, gkw)), _const_spec((1, aug_w))],
        out_specs=[row(aug_w), row(aug_w), row(aug_w), row(gkw), row(gkw), row(gvw), row(gvw),
                   row(gkw)],
        scratch_shapes=[pltpu.VMEM((1, LANES), F32)],
        compiler_params=_params(("arbitrary", "arbitrary")),
        name="in_proj",
    )(x, norm_g.reshape(1, D), w_all, ones_blk, tri, pq, pk, qg, kg, bf, wg2, bg, e64)


def _gla_kernel(gq_ref, gk_ref, gv_ref, gr_ref, gl_ref, tri3_ref, gain_ref, out_ref,
                st_ref, o_ref, *, dk, dv):
    @pl.when(pl.program_id(1) == 0)
    def _():
        st_ref[...] = jnp.zeros_like(st_ref)

    T, kw = gl_ref.shape
    vw = gv_ref.shape[1]
    n_heads = kw // dk
    C = GLA_CHUNK

    n_chunks = T // C
    chunks = [slice(c * C, (c + 1) * C) for c in range(n_chunks)]

    hi, mid, lo = _split3(gl_ref[...])
    tri3 = tri3_ref[...]
    bcs = [_dot(tri3, jnp.concatenate([hi[r], mid[r], lo[r]], axis=0)) for r in chunks]
    b_last = [b[C - 1:C] for b in bcs]
    bc = jnp.concatenate(bcs, axis=0)
    k = gk_ref[...].astype(F32)
    q_dec = gq_ref[...].astype(F32) * (dk ** -0.5) * jnp.exp(bc)
    k_dec = (k * jnp.exp(-bc)).astype(BF16)

    klane = lax.broadcasted_iota(jnp.int32, (1, kw), 1) // dk
    row_h = lax.broadcasted_iota(jnp.int32, (n_heads * C, C), 0)
    col = lax.broadcasted_iota(jnp.int32, (n_heads * C, C), 1)
    tril = col <= (row_h & (C - 1))
    bd = (lax.broadcasted_iota(jnp.int32, (vw, kw), 0) // dv
          == lax.broadcasted_iota(jnp.int32, (vw, kw), 1) // dk)

    d_sts = [_dot_tn(gv_ref[r, :], (k[r] * jnp.exp(bl - b)).astype(BF16))
             for r, b, bl in zip(chunks, bcs, b_last)]
    st = st_ref[...]
    sts = []
    for d_st, bl in zip(d_sts, b_last):
        sts.append(st.astype(BF16))
        st = st * jnp.exp(bl) + jnp.where(bd, d_st, 0.0)
    st_ref[...] = st

    for r, st_c in zip(chunks, sts):
        qd = q_dec[r]
        lhs = jnp.concatenate([jnp.where(klane == h, qd, 0.0) for h in range(n_heads)],
                              axis=0).astype(BF16)
        a = jnp.where(tril, _dot_nt(lhs, k_dec[r]), 0.0).astype(BF16)
        v = gv_ref[r, :]
        o_intra = jnp.concatenate(
            [_dot(a[h * C:(h + 1) * C], v[:, h * dv:(h + 1) * dv]) for h in range(n_heads)],
            axis=1)
        o_ref[r, :] = o_intra + _dot_nt(qd.astype(BF16), st_c)

    o = o_ref[...]
    gr = gr_ref[...].astype(F32)
    gain = gain_ref[...]
    outs = []
    for h in range(n_heads):
        oh = o[:, h * dv:(h + 1) * dv]
        on = oh * lax.rsqrt(jnp.mean(oh * oh, axis=-1, keepdims=True) + EPS)
        outs.append(on * gain[:, h * dv:(h + 1) * dv])
    out_ref[...] = (jnp.concatenate(outs, axis=1) * (gr * jax.nn.sigmoid(gr))).astype(BF16)


def _gla(gq, gk, gv, gr, gl, out_norm_g):
    B, S, kw = gq.shape
    vw = gv.shape[2]
    dk, dv = kw // GLA_HEADS, vw // GLA_HEADS
    T = min(ROW_TILE, S)
    assert S % T == 0 and T % GLA_CHUNK == 0
    r = np.arange(GLA_CHUNK)
    tri3 = jnp.asarray(np.tile(r[None, :] <= r[:, None], (1, 3)), BF16)
    row = lambda c: pl.BlockSpec((None, T, c), lambda b, i: (b, i, 0))
    return pl.pallas_call(
        functools.partial(_gla_kernel, dk=dk, dv=dv),
        out_shape=jax.ShapeDtypeStruct((B, S, vw), BF16),
        grid=(B, S // T),
        in_specs=[row(kw), row(kw), row(vw), row(vw), row(kw),
                  _const_spec((GLA_CHUNK, 3 * GLA_CHUNK)), _const_spec((1, vw))],
        out_specs=row(vw),
        scratch_shapes=[pltpu.VMEM((vw, kw), F32), pltpu.VMEM((T, vw), F32)],
        compiler_params=_params(("arbitrary", "arbitrary")),
        name="gla",
    )(gq, gk, gv, gr, gl, tri3, out_norm_g.reshape(1, vw))


def _fox_kernel(q_ref, k_ref, v_ref, o_ref, m_all, acc_all, s0, s1, p0, p1, al0, al1,
                *, head_dim, nq):
    T = s0.shape[0]
    s_buf, p_buf, al_buf = (s0, s1), (p0, p1), (al0, al1)
    depth = 3

    def rows(blk):
        start = blk * T
        return pl.ds(start if isinstance(blk, int) else pl.multiple_of(start, T), T)

    m_all[rows(nq), :] = jnp.zeros((T, LANES), F32)
    acc_all[rows(nq), :] = jnp.zeros((T, LANES), F32)
    for ref in (s1, p0, al0):
        ref[...] = jnp.zeros_like(ref)

    def scores(qi, kj, par, diag):
        qr = jnp.minimum(qi, nq - 1)
        s = _dot_nt(q_ref[rows(qr), :], k_ref[rows(kj), :])
        if diag:
            row = lax.broadcasted_iota(jnp.int32, (T, T), 0)
            col = lax.broadcasted_iota(jnp.int32, (T, T), 1)
            s = jnp.where(col <= row, s, -jnp.inf)
        s_buf[par][...] = s

    def softmax(qi, par, diag):
        m_cur = jnp.broadcast_to(jnp.max(s_buf[par][...], axis=-1, keepdims=True), (T, LANES))
        if diag:
            m_new = m_cur
        else:
            m_prev = m_all[rows(qi), :]
            m_new = jnp.maximum(m_prev, m_cur)
            al_buf[par][...] = jnp.exp2(m_prev - m_new)
        m_all[rows(qi), :] = m_new
        for g in range(T // LANES):
            cols = slice(g * LANES, (g + 1) * LANES)
            p_buf[par][:, cols] = jnp.exp2(s_buf[par][:, cols] - m_new).astype(BF16)

    def pv(qi, kj, par, diag):
        start = pl.multiple_of(kj * T, T)
        pvs = [_dot(p_buf[par][:, c:c + MXU_K], v_ref[pl.ds(start + c, MXU_K), :])
               for c in range(0, T, MXU_K)]
        out = sum(pvs[1:], pvs[0])
        if not diag:
            out = al_buf[par][...] * acc_all[rows(qi), :] + out
        acc_all[rows(qi), :] = out

    def step(c, blocks, diag):
        (q0, k0), (q1, _), (q2, k2) = blocks
        pv(q2, k2, c % 2, diag)
        scores(q0, k0, c % 2, diag)
        softmax(q1, (c - 1) % 2, diag)

    def run(n_blocks, first, advance, diag, steps_per_trip):
        spare = (jnp.int32(nq), jnp.int32(0))

        def trip(u, carry):
            nxt, hist = carry[:2], [carry[2:4], carry[4:6]]
            for c in range(steps_per_trip):
                step(c, [nxt] + hist, diag)
                hist = [nxt] + hist[:1]
                nxt = advance(*nxt)
            return tuple(nxt) + tuple(x for h in hist for x in h)

        n_steps = n_blocks + depth - 1
        lax.fori_loop(0, -(-n_steps // steps_per_trip), trip, tuple(first) + spare * 2)

    def next_diag(qi, kj):
        n = jnp.minimum(qi + 1, nq)
        return n, jnp.minimum(n, nq - 1)

    def next_lower(qi, kj):
        wrap = kj + 1 >= qi
        qn = jnp.where(wrap, jnp.minimum(qi + 1, nq), qi)
        return qn, jnp.where(wrap & (qi < nq), 0, jnp.minimum(kj + 1, nq - 1))

    run(nq, (jnp.int32(0), jnp.int32(0)), next_diag, True, FOX_DIAG_UNROLL)
    if nq > 1:
        run(nq * (nq - 1) // 2, (jnp.int32(1), jnp.int32(0)), next_lower, False, FOX_UNROLL)

    def finish(b, carry):
        acc = acc_all[rows(b), :]
        lane = lax.broadcasted_iota(jnp.int32, acc.shape, 1)
        inv_l = 1.0 / acc[:, head_dim:head_dim + 1]
        o_ref[rows(b), :] = jnp.where(lane < head_dim, acc * inv_l, 0.0).astype(BF16)
        return carry

    lax.fori_loop(0, nq, finish, 0)


def _fox(qa, ka, va, head_dim):
    B, S, W = qa.shape
    H = W // LANES
    T = min(FOX_TILE, S)
    assert S % T == 0
    nq = S // T
    head = pl.BlockSpec((None, S, LANES), lambda b, h: (b, 0, h))
    state = pltpu.VMEM(((nq + 1) * T, LANES), F32)
    return pl.pallas_call(
        functools.partial(_fox_kernel, head_dim=head_dim, nq=nq),
        out_shape=jax.ShapeDtypeStruct((B, S, W), BF16),
        grid=(B, H),
        in_specs=[head, head, head],
        out_specs=head,
        scratch_shapes=([state, state] + [pltpu.VMEM((T, T), F32)] * 2
                        + [pltpu.VMEM((T, T), BF16)] * 2 + [pltpu.VMEM((T, LANES), F32)] * 2),
        compiler_params=_params(("arbitrary", "arbitrary")),
        name="fox",
    )(qa, ka, va)


def _mem_kv_kernel(mem_ref, g_ref, w_ref, kg_ref, k_ref, v_ref, *, xd):
    x = mem_ref[...]
    D = x.shape[1]
    xn = (x * lax.rsqrt(jnp.mean(x * x, axis=-1, keepdims=True) + EPS) * g_ref[...]).astype(BF16)
    kg = kg_ref[...]
    for h in range(D // xd):
        kh = _dot(xn, w_ref[:, h * xd:(h + 1) * xd])
        kn = kh * lax.rsqrt(jnp.mean(kh * kh, axis=-1, keepdims=True) + EPS) * kg
        k_ref[:, h * xd:(h + 1) * xd] = kn.astype(BF16)
    v_ref[...] = _dot(xn, w_ref[:, D:]).astype(BF16)


def _mem_kv(mem, norm_mem_g, wkv, k_norm_g):
    B, M, D = mem.shape
    xd = k_norm_g.shape[0]
    kg = (k_norm_g * (LOG2E / math.sqrt(xd))).reshape(1, xd)
    blk = pl.BlockSpec((None, M, D), lambda b: (b, 0, 0))
    return pl.pallas_call(
        functools.partial(_mem_kv_kernel, xd=xd),
        out_shape=[jax.ShapeDtypeStruct((B, M, D), BF16)] * 2,
        grid=(B,),
        in_specs=[blk, _const_spec((1, D)), _const_spec((D, 2 * D)), _const_spec((1, xd))],
        out_specs=[blk, blk],
        compiler_params=_params(("arbitrary",)),
        name="mem_kv",
    )(mem, norm_mem_g.reshape(1, D), wkv.astype(BF16), kg)


def _mix_kernel(x_ref, fox_ref, gla_ref, wof_ref, wog_ref, g_ref, wq_ref, qg_ref, km_ref, vm_ref,
                wo_ref, h_ref, *, xd):
    tm, D = x_ref.shape
    splits = [slice(i * tm // MIX_SPLITS, (i + 1) * tm // MIX_SPLITS) for i in range(MIX_SPLITS)]
    heads = [slice(hd * xd, (hd + 1) * xd) for hd in range(D // xd)]
    qg = qg_ref[...]

    def rms(v, gain):
        return v * lax.rsqrt(jnp.mean(v * v, axis=-1, keepdims=True) + EPS) * gain

    hs = [x_ref[r, :] + _dot(fox_ref[r, :], wof_ref[...]) + _dot(gla_ref[r, :], wog_ref[...])
          for r in splits]
    hns = [rms(h, g_ref[...]).astype(BF16) for h in hs]
    qs = [_dot(hn, wq_ref[...]) for hn in hns]
    qns = [[rms(q[:, sl], qg).astype(BF16) for sl in heads] for q in qs]
    ss = [[_dot_nt(qn, km_ref[:, sl]) for qn, sl in zip(row, heads)] for row in qns]
    ps = [[jnp.exp2(s - jnp.max(s, axis=-1, keepdims=True)) for s in row] for row in ss]
    os = [[(_dot(p.astype(BF16), vm_ref[:, sl]) * (1.0 / jnp.sum(p, axis=-1, keepdims=True))
            ).astype(BF16) for p, sl in zip(row, heads)] for row in ps]
    for r, h, o in zip(splits, hs, os):
        h_ref[r, :] = h + _dot(jnp.concatenate(o, axis=1), wo_ref[...])


def _mix(x, fox, gla, w_out, head_dim, norm_g, wq, q_norm_g, km, vm, wo):
    B, S, D = x.shape
    fw = fox.shape[2]
    gw = gla.shape[2]
    M = km.shape[1]
    xd = q_norm_g.shape[0]
    n_fox = fw // LANES
    tm = min(ROW_TILE, S)
    wof = jnp.zeros((n_fox, LANES, D), F32).at[:, :head_dim].set(
        w_out[:n_fox * head_dim].reshape(n_fox, head_dim, D)).reshape(fw, D).astype(BF16)
    wog = w_out[n_fox * head_dim:].astype(BF16)
    row = lambda c: pl.BlockSpec((None, tm, c), lambda b, i: (b, i, 0))
    mem = pl.BlockSpec((None, M, D), lambda b, i: (b, 0, 0))
    return pl.pallas_call(
        functools.partial(_mix_kernel, xd=xd),
        out_shape=jax.ShapeDtypeStruct((B, S, D), F32),
        grid=(B, S // tm),
        in_specs=[row(D), row(fw), row(gw), _const_spec((fw, D)), _const_spec((gw, D)),
                  _const_spec((1, D)), _const_spec((D, D)), _const_spec((1, xd)), mem, mem,
                  _const_spec((D, D))],
        out_specs=row(D),
        compiler_params=_params(("arbitrary", "arbitrary")),
        name="mix",
    )(x, fox, gla, wof, wog, norm_g.reshape(1, D), wq.astype(BF16), q_norm_g.reshape(1, xd),
      km, vm, wo.astype(BF16))


def _mlp_kernel(h_ref, g_ref, w1_ref, w2_ref, y_ref, *, slab):
    h = h_ref[...]
    hn = (h * lax.rsqrt(jnp.mean(h * h, axis=-1, keepdims=True) + EPS) * g_ref[...]).astype(BF16)
    y = h
    for j in range(w1_ref.shape[1] // slab):
        u = jnp.maximum(_dot(hn, w1_ref[:, j * slab:(j + 1) * slab]), 0.0)
        y = y + _dot((u * u).astype(BF16), w2_ref[j * slab:(j + 1) * slab, :])
    y_ref[...] = y


def _mlp(h, norm_g, w1, w2):
    B, S, D = h.shape
    F = w1.shape[1]
    tm = min(ROW_TILE, S)
    row = pl.BlockSpec((None, tm, D), lambda b, i: (b, i, 0))
    return pl.pallas_call(
        functools.partial(_mlp_kernel, slab=min(F, 1024)),
        out_shape=jax.ShapeDtypeStruct((B, S, D), F32),
        grid=(B, S // tm),
        in_specs=[row, _const_spec((1, D)), _const_spec((D, F)), _const_spec((F, D))],
        out_specs=row,
        compiler_params=_params(("arbitrary", "arbitrary")),
        name="mlp",
    )(h, norm_g.reshape(1, D), w1.astype(BF16), w2.astype(BF16))


def kernel(x, mem, norm_mix_g, w_in, fox_b_f, fox_q_norm_g, fox_k_norm_g, gla_w_gate2, gla_b_gate,
           gla_out_norm_g, w_out, norm_xattn_g, norm_mem_g, xattn_wq, xattn_wkv, xattn_q_norm_g,
           xattn_k_norm_g, xattn_wo, norm_mlp_g, mlp_w1, mlp_w2):
    head_dim = fox_q_norm_g.shape[0]
    qa, ka, va, gq, gk, gv, gr, gl = _in_proj(
        x, norm_mix_g, w_in, fox_b_f, fox_q_norm_g, fox_k_norm_g, gla_w_gate2, gla_b_gate,
        gla_out_norm_g.shape[0])
    gla = _gla(gq, gk, gv, gr, gl, gla_out_norm_g)
    fox = _fox(qa, ka, va, head_dim)
    km, vm = _mem_kv(mem, norm_mem_g, xattn_wkv, xattn_k_norm_g)
    h = _mix(x, fox, gla, w_out, head_dim, norm_xattn_g, xattn_wq, xattn_q_norm_g, km, vm, xattn_wo)
    return _mlp(h, norm_mlp_g, mlp_w1, mlp_w2)
```

```python
import functools
import math

import numpy as np
import jax
import jax.numpy as jnp
from jax import lax
from jax.experimental import pallas as pl
from jax.experimental.pallas import tpu as pltpu

F32 = jnp.float32
BF16 = jnp.bfloat16

EPS = 1e-6
LOG2E = 1.4426950408889634

GLA_CHUNK = 64
GLA_HEADS = 4
GLA_TAU = 16.0

LANES = 128
MXU_K = 256
VMEM_LIMIT_BYTES = 56 * 1024 * 1024

ROW_TILE = 512
MIX_SPLITS = 2
FOX_TILE = 512
FOX_UNROLL = 8
FOX_DIAG_UNROLL = 4
FOX_ZERO_LOG2 = 160.0
FOX_BOUND_SLACK = 2.0
FOX_ROUNDING_SLACK = 1.02
AUG0 = 64


def _const_spec(shape):
    return pl.BlockSpec(shape, lambda *_: (0,) * len(shape))


def _params(semantics, flags=None):
    return pltpu.CompilerParams(dimension_semantics=semantics,
                                vmem_limit_bytes=VMEM_LIMIT_BYTES, flags=flags)


def _split3(c):
    hi = c.astype(BF16)
    r = c - hi.astype(F32)
    mid = r.astype(BF16)
    lo = (r - mid.astype(F32)).astype(BF16)
    return hi, mid, lo


def _log_sigmoid(z):
    return -(jnp.maximum(-z, 0.0) + jnp.log1p(jnp.exp(-jnp.abs(z))))


def _dot(a, b):
    return jnp.dot(a, b, preferred_element_type=F32)


def _dot_nt(a, b):
    return lax.dot_general(a, b, (((1,), (1,)), ((), ())), preferred_element_type=F32)


def _dot_tn(a, b):
    return lax.dot_general(a, b, (((0,), (0,)), ((), ())), preferred_element_type=F32)


def _in_proj_kernel(x_ref, g_ref, w_ref, ones_ref, tri_ref, pq_ref, pk_ref, qg_ref, kg_ref,
                    bf_ref, wg2_ref, bg_ref, e64_ref,
                    qa_ref, ka_ref, va_ref, gq_ref, gk_ref, gv_ref, gr_ref, gl_ref, ce_ref,
                    carry_ref, *, fox_w, gkw, gvw, n_fox, head_dim):
    @pl.when(pl.program_id(1) == 0)
    def _():
        carry_ref[...] = jnp.zeros_like(carry_ref)

    x = x_ref[...]
    xn = (x * lax.rsqrt(jnp.mean(x * x, axis=-1, keepdims=True) + EPS) * g_ref[...]).astype(BF16)

    o_k = fox_w
    o_v = 2 * fox_w
    o_gq = 3 * fox_w
    o_gk = o_gq + gkw
    o_gv = o_gk + gkw
    o_gr = o_gv + gvw
    o_sm = o_gr + gvw

    def proj(lo, width):
        return _dot(xn, w_ref[:, lo:lo + width])

    def head_norm(f, gain_row):
        ssq = _dot((f * f).astype(BF16), ones_ref[...])
        return f * lax.rsqrt(ssq * (1.0 / head_dim) + EPS) * gain_row

    def spread(f):
        cols = []
        for h in range(n_fox):
            c = f[:, (h // 2) * LANES:(h // 2 + 1) * LANES]
            if h % 2:
                c = pltpu.roll(c, LANES // 2, axis=1)
            cols.append(c)
        return jnp.concatenate(cols, axis=1)

    gr_sm = proj(o_gr, gvw + LANES)
    sm = gr_sm[:, gvw:]
    lane = lax.broadcasted_iota(jnp.int32, sm.shape, 1)
    lf = jnp.where(lane < n_fox, _log_sigmoid(sm + bf_ref[...]), 0.0)
    fq = proj(0, fox_w)
    fk = proj(o_k, fox_w)
    fq = head_norm(fq, qg_ref[...])
    fv = proj(o_v, fox_w)
    fk = head_norm(fk, kg_ref[...])

    def pack3(v):
        hi, mid, lo = _split3(v)
        return (hi.astype(F32) + pltpu.roll(mid.astype(F32), n_fox, axis=1)
                + pltpu.roll(lo.astype(F32), 2 * n_fox, axis=1))

    cs = _dot(tri_ref[...], pack3(lf).astype(BF16))
    cs = cs + pltpu.roll(cs, LANES - n_fox, axis=1) + pltpu.roll(cs, LANES - 2 * n_fox, axis=1)
    c = jnp.where(lane < n_fox, cs, 0.0) + carry_ref[...]
    tm = c.shape[0]
    carry_ref[...] = c[tm - 1:tm, :]
    c2 = c * LOG2E
    cpk = jnp.where(lane == 3 * n_fox, 1.0, pack3(c2)).astype(BF16)
    ce_ref[...] = jnp.concatenate([c2[0:1], c2[tm - 1:tm], jnp.zeros((6, LANES), F32)], axis=0)

    g_qk = proj(o_gq, 2 * gkw)
    gq_ref[...] = g_qk[:, :gkw].astype(BF16)
    gk_ref[...] = g_qk[:, gkw:].astype(BF16)
    gv_ref[...] = proj(o_gv, gvw).astype(BF16)
    gr_ref[...] = gr_sm[:, :gvw].astype(BF16)
    gate = _dot(sm.astype(BF16), wg2_ref[...]) + bg_ref[...]
    gl_ref[...] = _log_sigmoid(gate) * (1.0 / GLA_TAU)

    wide_lane = lax.broadcasted_iota(jnp.int32, (1, n_fox * LANES), 1)
    is_head_dim = (wide_lane & (LANES - 1)) < head_dim
    va_ref[...] = jnp.where(is_head_dim, spread(fv), e64_ref[...]).astype(BF16)
    qa_ref[...] = jnp.where(is_head_dim, spread(fq), _dot(cpk, pq_ref[...])).astype(BF16)
    ka_ref[...] = jnp.where(is_head_dim, spread(fk), _dot(cpk, pk_ref[...])).astype(BF16)


def _in_proj(x, norm_g, w_in, fox_b_f, fox_q_norm_g, fox_k_norm_g, gla_w_gate2, gla_b_gate, gvw):
    B, S, D = x.shape
    n_fox = fox_b_f.shape[0]
    head_dim = fox_q_norm_g.shape[0]
    fox_w = n_fox * head_dim
    rank, gkw = gla_w_gate2.shape
    tm = min(ROW_TILE, S)
    assert S % tm == 0 and head_dim == LANES // 2 and n_fox % 2 == 0
    assert 3 * n_fox + 1 <= LANES and n_fox + rank <= LANES

    sizes = (fox_w, fox_w, fox_w, n_fox, gkw, gkw, gvw, rank, gvw)
    offs = np.concatenate([[0], np.cumsum(sizes)])
    wq, wk, wv, wf, wgq, wgk, wgv, wlr, wgr = [w_in[:, offs[i]:offs[i + 1]] for i in range(9)]
    w_small = jnp.zeros((D, LANES), F32).at[:, :n_fox].set(wf).at[:, n_fox:n_fox + rank].set(wlr)
    w_all = jnp.concatenate([wq, wk, wv, wgq, wgk, wgv, wgr, w_small], axis=1).astype(BF16)
    width = w_all.shape[1]

    scale = (1.0 / math.sqrt(head_dim)) * LOG2E
    qg = (jnp.tile(fox_q_norm_g, n_fox) * scale).reshape(1, fox_w)
    kg = jnp.tile(fox_k_norm_g, n_fox).reshape(1, fox_w)
    bf = jnp.zeros((1, LANES), F32).at[0, :n_fox].set(fox_b_f)
    wg2 = jnp.zeros((LANES, gkw), F32).at[n_fox:n_fox + rank].set(gla_w_gate2).astype(BF16)
    bg = gla_b_gate.reshape(1, gkw)

    grp = np.arange(fox_w) // head_dim
    ones_blk = jnp.asarray(grp[:, None] == grp[None, :], BF16)
    r = np.arange(tm)
    tri = jnp.asarray(r[None, :] <= r[:, None], BF16)
    pq = np.zeros((LANES, n_fox * LANES), np.float32)
    pk = np.zeros((LANES, n_fox * LANES), np.float32)
    e64 = np.zeros((1, n_fox * LANES), np.float32)
    for h in range(n_fox):
        base = h * LANES + AUG0
        for part in range(3):
            pq[part * n_fox + h, base + part] = 1.0
            pk[part * n_fox + h, base + 3 + part] = -1.0
            pq[3 * n_fox, base + 3 + part] = 1.0
            pk[3 * n_fox, base + part] = 1.0
        e64[0, base] = 1.0
    pq = jnp.asarray(pq, BF16)
    pk = jnp.asarray(pk, BF16)
    e64 = jnp.asarray(e64)

    row = lambda c: pl.BlockSpec((None, tm, c), lambda b, i: (b, i, 0))
    kern = functools.partial(_in_proj_kernel, fox_w=fox_w, gkw=gkw, gvw=gvw, n_fox=n_fox,
                             head_dim=head_dim)
    aug_w = n_fox * LANES
    out_shapes = [jax.ShapeDtypeStruct((B, S, aug_w), BF16)] * 3 + [
        jax.ShapeDtypeStruct((B, S, gkw), BF16), jax.ShapeDtypeStruct((B, S, gkw), BF16),
        jax.ShapeDtypeStruct((B, S, gvw), BF16), jax.ShapeDtypeStruct((B, S, gvw), BF16),
        jax.ShapeDtypeStruct((B, S, gkw), F32),
        jax.ShapeDtypeStruct((B, S // tm, 8, LANES), F32)]
    return pl.pallas_call(
        kern,
        out_shape=out_shapes,
        grid=(B, S // tm),
        in_specs=[row(D), _const_spec((1, D)), _const_spec((D, width)),
                  _const_spec((fox_w, fox_w)), _const_spec((tm, tm)),
                  _const_spec((LANES, aug_w)), _const_spec((LANES, aug_w)),
                  _const_spec((1, fox_w)), _const_spec((1, fox_w)), _const_spec((1, LANES)),
                  _const_spec((LANES, gkw)), _const_spec((1, gkw)), _const_spec((1, aug_w))],
        out_specs=[row(aug_w), row(aug_w), row(aug_w), row(gkw), row(gkw), row(gvw), row(gvw),
                   row(gkw), pl.BlockSpec((None, None, 8, LANES), lambda b, i: (b, i, 0, 0))],
        scratch_shapes=[pltpu.VMEM((1, LANES), F32)],
        compiler_params=_params(("arbitrary", "arbitrary")),
        name="in_proj",
    )(x, norm_g.reshape(1, D), w_all, ones_blk, tri, pq, pk, qg, kg, bf, wg2, bg, e64)


def _gla_kernel(gq_ref, gk_ref, gv_ref, gr_ref, gl_ref, tri3_ref, gain_ref, out_ref,
                st_ref, o_ref, *, dk, dv):
    @pl.when(pl.program_id(1) == 0)
    def _():
        st_ref[...] = jnp.zeros_like(st_ref)

    T, kw = gl_ref.shape
    vw = gv_ref.shape[1]
    n_heads = kw // dk
    C = GLA_CHUNK

    n_chunks = T // C
    chunks = [slice(c * C, (c + 1) * C) for c in range(n_chunks)]

    hi, mid, lo = _split3(gl_ref[...])
    tri3 = tri3_ref[...]
    bcs = [_dot(tri3, jnp.concatenate([hi[r], mid[r], lo[r]], axis=0)) for r in chunks]
    b_last = [b[C - 1:C] for b in bcs]
    bc = jnp.concatenate(bcs, axis=0)
    k = gk_ref[...].astype(F32)
    q_dec = gq_ref[...].astype(F32) * (dk ** -0.5) * jnp.exp(bc)
    k_dec = (k * jnp.exp(-bc)).astype(BF16)

    klane = lax.broadcasted_iota(jnp.int32, (1, kw), 1) // dk
    row_h = lax.broadcasted_iota(jnp.int32, (n_heads * C, C), 0)
    col = lax.broadcasted_iota(jnp.int32, (n_heads * C, C), 1)
    tril = col <= (row_h & (C - 1))
    bd = (lax.broadcasted_iota(jnp.int32, (vw, kw), 0) // dv
          == lax.broadcasted_iota(jnp.int32, (vw, kw), 1) // dk)

    d_sts = [_dot_tn(gv_ref[r, :], (k[r] * jnp.exp(bl - b)).astype(BF16))
             for r, b, bl in zip(chunks, bcs, b_last)]
    st = st_ref[...]
    sts = []
    for d_st, bl in zip(d_sts, b_last):
        sts.append(st.astype(BF16))
        st = st * jnp.exp(bl) + jnp.where(bd, d_st, 0.0)
    st_ref[...] = st

    for r, st_c in zip(chunks, sts):
        qd = q_dec[r]
        lhs = jnp.concatenate([jnp.where(klane == h, qd, 0.0) for h in range(n_heads)],
                              axis=0).astype(BF16)
        a = jnp.where(tril, _dot_nt(lhs, k_dec[r]), 0.0).astype(BF16)
        v = gv_ref[r, :]
        o_intra = jnp.concatenate(
            [_dot(a[h * C:(h + 1) * C], v[:, h * dv:(h + 1) * dv]) for h in range(n_heads)],
            axis=1)
        o_ref[r, :] = o_intra + _dot_nt(qd.astype(BF16), st_c)

    o = o_ref[...]
    gr = gr_ref[...].astype(F32)
    gain = gain_ref[...]
    outs = []
    for h in range(n_heads):
        oh = o[:, h * dv:(h + 1) * dv]
        on = oh * lax.rsqrt(jnp.mean(oh * oh, axis=-1, keepdims=True) + EPS)
        outs.append(on * gain[:, h * dv:(h + 1) * dv])
    out_ref[...] = (jnp.concatenate(outs, axis=1) * (gr * jax.nn.sigmoid(gr))).astype(BF16)


def _gla(gq, gk, gv, gr, gl, out_norm_g):
    B, S, kw = gq.shape
    vw = gv.shape[2]
    dk, dv = kw // GLA_HEADS, vw // GLA_HEADS
    T = min(ROW_TILE, S)
    assert S % T == 0 and T % GLA_CHUNK == 0
    r = np.arange(GLA_CHUNK)
    tri3 = jnp.asarray(np.tile(r[None, :] <= r[:, None], (1, 3)), BF16)
    row = lambda c: pl.BlockSpec((None, T, c), lambda b, i: (b, i, 0))
    return pl.pallas_call(
        functools.partial(_gla_kernel, dk=dk, dv=dv),
        out_shape=jax.ShapeDtypeStruct((B, S, vw), BF16),
        grid=(B, S // T),
        in_specs=[row(kw), row(kw), row(vw), row(vw), row(kw),
                  _const_spec((GLA_CHUNK, 3 * GLA_CHUNK)), _const_spec((1, vw))],
        out_specs=row(vw),
        scratch_shapes=[pltpu.VMEM((vw, kw), F32), pltpu.VMEM((T, vw), F32)],
        compiler_params=_params(("arbitrary", "arbitrary")),
        name="gla",
    )(gq, gk, gv, gr, gl, tri3, out_norm_g.reshape(1, vw))


def _fox_kernel(tab_q, tab_k, count, q_ref, k_ref, v_ref, o_ref, m_all, acc_all, s0, s1, p0, p1,
                al0, al1, *, head_dim, nq):
    T = s0.shape[0]
    head_idx = pl.program_id(0) * pl.num_programs(1) + pl.program_id(1)
    s_buf, p_buf, al_buf = (s0, s1), (p0, p1), (al0, al1)
    depth = 3

    def rows(blk):
        start = blk * T
        return pl.ds(start if isinstance(blk, int) else pl.multiple_of(start, T), T)

    m_all[rows(nq), :] = jnp.zeros((T, LANES), F32)
    acc_all[rows(nq), :] = jnp.zeros((T, LANES), F32)
    for ref in (s1, p0, al0):
        ref[...] = jnp.zeros_like(ref)

    def scores(qi, kj, par, diag):
        qr = jnp.minimum(qi, nq - 1)
        s = _dot_nt(q_ref[rows(qr), :], k_ref[rows(kj), :])
        if diag:
            row = lax.broadcasted_iota(jnp.int32, (T, T), 0)
            col = lax.broadcasted_iota(jnp.int32, (T, T), 1)
            s = jnp.where(col <= row, s, -jnp.inf)
        s_buf[par][...] = s

    def softmax(qi, par, diag):
        m_cur = jnp.broadcast_to(jnp.max(s_buf[par][...], axis=-1, keepdims=True), (T, LANES))
        if diag:
            m_new = m_cur
        else:
            m_prev = m_all[rows(qi), :]
            m_new = jnp.maximum(m_prev, m_cur)
            al_buf[par][...] = jnp.exp2(m_prev - m_new)
        m_all[rows(qi), :] = m_new
        for g in range(T // LANES):
            cols = slice(g * LANES, (g + 1) * LANES)
            p_buf[par][:, cols] = jnp.exp2(s_buf[par][:, cols] - m_new).astype(BF16)

    def pv(qi, kj, par, diag):
        start = pl.multiple_of(kj * T, T)
        pvs = [_dot(p_buf[par][:, c:c + MXU_K], v_ref[pl.ds(start + c, MXU_K), :])
               for c in range(0, T, MXU_K)]
        out = sum(pvs[1:], pvs[0])
        if not diag:
            out = al_buf[par][...] * acc_all[rows(qi), :] + out
        acc_all[rows(qi), :] = out

    def step(c, blocks, diag):
        (q0, k0), (q1, _), (q2, k2) = blocks
        pv(q2, k2, c % 2, diag)
        scores(q0, k0, c % 2, diag)
        softmax(q1, (c - 1) % 2, diag)

    def run(n_blocks, block_at, diag, steps_per_trip):
        spare = (jnp.int32(nq), jnp.int32(0))

        def trip(u, carry):
            hist = [carry[0:2], carry[2:4]]
            for c in range(steps_per_trip):
                nxt = block_at(u * steps_per_trip + c)
                step(c, [nxt] + hist, diag)
                hist = [nxt] + hist[:1]
            return tuple(x for h in hist for x in h)

        n_trips = (n_blocks + (depth - 2 + steps_per_trip)) // steps_per_trip
        lax.fori_loop(0, n_trips, trip, spare * 2)

    def diag_block(n):
        return jnp.minimum(n, nq), jnp.minimum(n, nq - 1)

    def lower_block(n):
        return tab_q[head_idx, n], tab_k[head_idx, n]

    run(nq, diag_block, True, FOX_DIAG_UNROLL)
    if nq > 1:
        run(count[head_idx], lower_block, False, FOX_UNROLL)

    def finish(b, carry):
        acc = acc_all[rows(b), :]
        lane = lax.broadcasted_iota(jnp.int32, acc.shape, 1)
        inv_l = 1.0 / acc[:, head_dim:head_dim + 1]
        o_ref[rows(b), :] = jnp.where(lane < head_dim, acc * inv_l, 0.0).astype(BF16)
        return carry

    lax.fori_loop(0, nq, finish, 0)


def _fox_block_table(c_edges, qk_bound, n_heads, nq):
    B = c_edges.shape[0]
    c_first = c_edges[:, :, 0, :n_heads]
    c_last = c_edges[:, :, 1, :n_heads]
    gap = c_first[:, :, None, :] - c_last[:, None, :, :]
    qi = np.arange(nq)[:, None]
    kj = np.arange(nq)[None, :]
    lower = jnp.asarray(kj < qi)[None, :, :, None]
    keep = lower & (2.0 * qk_bound + gap + FOX_BOUND_SLACK > -FOX_ZERO_LOG2)
    keep = jnp.transpose(keep, (0, 3, 1, 2)).reshape(B * n_heads, nq * nq)
    count = jnp.sum(keep, axis=1).astype(jnp.int32)
    order = jnp.argsort(~keep, axis=1, stable=True).astype(jnp.int32)
    n_tab = -(-(nq * (nq - 1) // 2 + 2) // FOX_UNROLL) * FOX_UNROLL
    order = jnp.pad(order, ((0, 0), (0, max(0, n_tab - nq * nq))))[:, :n_tab]
    live = jnp.arange(n_tab)[None, :] < count[:, None]
    tab_q = jnp.where(live, order // nq, nq).astype(jnp.int32)
    tab_k = jnp.where(live, order % nq, 0).astype(jnp.int32)
    return tab_q, tab_k, count


def _fox(qa, ka, va, c_edges, qk_bound, head_dim):
    B, S, W = qa.shape
    H = W // LANES
    T = min(FOX_TILE, S)
    assert S % T == 0 and c_edges.shape[1] * T == S
    nq = S // T
    tab_q, tab_k, count = _fox_block_table(c_edges, qk_bound, H, nq)
    head = pl.BlockSpec((None, S, LANES), lambda b, h, *_: (b, 0, h))
    state = pltpu.VMEM(((nq + 1) * T, LANES), F32)
    return pl.pallas_call(
        functools.partial(_fox_kernel, head_dim=head_dim, nq=nq),
        out_shape=jax.ShapeDtypeStruct((B, S, W), BF16),
        grid_spec=pltpu.PrefetchScalarGridSpec(
            num_scalar_prefetch=3,
            grid=(B, H),
            in_specs=[head, head, head],
            out_specs=head,
            scratch_shapes=([state, state] + [pltpu.VMEM((T, T), F32)] * 2
                            + [pltpu.VMEM((T, T), BF16)] * 2
                            + [pltpu.VMEM((T, LANES), F32)] * 2)),
        compiler_params=_params(("arbitrary", "arbitrary")),
        name="fox",
    )(tab_q, tab_k, count, qa, ka, va)


def _mem_kv_kernel(mem_ref, g_ref, w_ref, kg_ref, k_ref, v_ref, *, xd):
    x = mem_ref[...]
    D = x.shape[1]
    xn = (x * lax.rsqrt(jnp.mean(x * x, axis=-1, keepdims=True) + EPS) * g_ref[...]).astype(BF16)
    kg = kg_ref[...]
    for h in range(D // xd):
        kh = _dot(xn, w_ref[:, h * xd:(h + 1) * xd])
        kn = kh * lax.rsqrt(jnp.mean(kh * kh, axis=-1, keepdims=True) + EPS) * kg
        k_ref[:, h * xd:(h + 1) * xd] = kn.astype(BF16)
    v_ref[...] = _dot(xn, w_ref[:, D:]).astype(BF16)


def _mem_kv(mem, norm_mem_g, wkv, k_norm_g):
    B, M, D = mem.shape
    xd = k_norm_g.shape[0]
    kg = (k_norm_g * (LOG2E / math.sqrt(xd))).reshape(1, xd)
    blk = pl.BlockSpec((None, M, D), lambda b: (b, 0, 0))
    return pl.pallas_call(
        functools.partial(_mem_kv_kernel, xd=xd),
        out_shape=[jax.ShapeDtypeStruct((B, M, D), BF16)] * 2,
        grid=(B,),
        in_specs=[blk, _const_spec((1, D)), _const_spec((D, 2 * D)), _const_spec((1, xd))],
        out_specs=[blk, blk],
        compiler_params=_params(("arbitrary",)),
        name="mem_kv",
    )(mem, norm_mem_g.reshape(1, D), wkv.astype(BF16), kg)


def _mix_kernel(x_ref, fox_ref, gla_ref, wof_ref, wog_ref, g_ref, wq_ref, qg_ref, km_ref, vm_ref,
                wo_ref, h_ref, *, xd):
    tm, D = x_ref.shape
    splits = [slice(i * tm // MIX_SPLITS, (i + 1) * tm // MIX_SPLITS) for i in range(MIX_SPLITS)]
    heads = [slice(hd * xd, (hd + 1) * xd) for hd in range(D // xd)]
    qg = qg_ref[...]

    def rms(v, gain):
        return v * lax.rsqrt(jnp.mean(v * v, axis=-1, keepdims=True) + EPS) * gain

    hs = [x_ref[r, :] + _dot(fox_ref[r, :], wof_ref[...]) + _dot(gla_ref[r, :], wog_ref[...])
          for r in splits]
    hns = [rms(h, g_ref[...]).astype(BF16) for h in hs]
    qs = [_dot(hn, wq_ref[...]) for hn in hns]
    qns = [[rms(q[:, sl], qg).astype(BF16) for sl in heads] for q in qs]
    ss = [[_dot_nt(qn, km_ref[:, sl]) for qn, sl in zip(row, heads)] for row in qns]
    ps = [[jnp.exp2(s - jnp.max(s, axis=-1, keepdims=True)) for s in row] for row in ss]
    os = [[(_dot(p.astype(BF16), vm_ref[:, sl]) * (1.0 / jnp.sum(p, axis=-1, keepdims=True))
            ).astype(BF16) for p, sl in zip(row, heads)] for row in ps]
    for r, h, o in zip(splits, hs, os):
        h_ref[r, :] = h + _dot(jnp.concatenate(o, axis=1), wo_ref[...])


def _mix(x, fox, gla, w_out, head_dim, norm_g, wq, q_norm_g, km, vm, wo):
    B, S, D = x.shape
    fw = fox.shape[2]
    gw = gla.shape[2]
    M = km.shape[1]
    xd = q_norm_g.shape[0]
    n_fox = fw // LANES
    tm = min(ROW_TILE, S)
    wof = jnp.zeros((n_fox, LANES, D), F32).at[:, :head_dim].set(
        w_out[:n_fox * head_dim].reshape(n_fox, head_dim, D)).reshape(fw, D).astype(BF16)
    wog = w_out[n_fox * head_dim:].astype(BF16)
    row = lambda c: pl.BlockSpec((None, tm, c), lambda b, i: (b, i, 0))
    mem = pl.BlockSpec((None, M, D), lambda b, i: (b, 0, 0))
    return pl.pallas_call(
        functools.partial(_mix_kernel, xd=xd),
        out_shape=jax.ShapeDtypeStruct((B, S, D), F32),
        grid=(B, S // tm),
        in_specs=[row(D), row(fw), row(gw), _const_spec((fw, D)), _const_spec((gw, D)),
                  _const_spec((1, D)), _const_spec((D, D)), _const_spec((1, xd)), mem, mem,
                  _const_spec((D, D))],
        out_specs=row(D),
        compiler_params=_params(("arbitrary", "arbitrary")),
        name="mix",
    )(x, fox, gla, wof, wog, norm_g.reshape(1, D), wq.astype(BF16), q_norm_g.reshape(1, xd),
      km, vm, wo.astype(BF16))


def _mlp_kernel(h_ref, g_ref, w1_ref, w2_ref, y_ref, *, slab):
    h = h_ref[...]
    hn = (h * lax.rsqrt(jnp.mean(h * h, axis=-1, keepdims=True) + EPS) * g_ref[...]).astype(BF16)
    y = h
    for j in range(w1_ref.shape[1] // slab):
        u = jnp.maximum(_dot(hn, w1_ref[:, j * slab:(j + 1) * slab]), 0.0)
        y = y + _dot((u * u).astype(BF16), w2_ref[j * slab:(j + 1) * slab, :])
    y_ref[...] = y


def _mlp(h, norm_g, w1, w2):
    B, S, D = h.shape
    F = w1.shape[1]
    tm = min(ROW_TILE, S)
    row = pl.BlockSpec((None, tm, D), lambda b, i: (b, i, 0))
    return pl.pallas_call(
        functools.partial(_mlp_kernel, slab=min(F, 1024)),
        out_shape=jax.ShapeDtypeStruct((B, S, D), F32),
        grid=(B, S // tm),
        in_specs=[row, _const_spec((1, D)), _const_spec((D, F)), _const_spec((F, D))],
        out_specs=row,
        compiler_params=_params(("arbitrary", "arbitrary")),
        name="mlp",
    )(h, norm_g.reshape(1, D), w1.astype(BF16), w2.astype(BF16))


def kernel(x, mem, norm_mix_g, w_in, fox_b_f, fox_q_norm_g, fox_k_norm_g, gla_w_gate2, gla_b_gate,
           gla_out_norm_g, w_out, norm_xattn_g, norm_mem_g, xattn_wq, xattn_wkv, xattn_q_norm_g,
           xattn_k_norm_g, xattn_wo, norm_mlp_g, mlp_w1, mlp_w2):
    head_dim = fox_q_norm_g.shape[0]
    qa, ka, va, gq, gk, gv, gr, gl, c_edges = _in_proj(
        x, norm_mix_g, w_in, fox_b_f, fox_q_norm_g, fox_k_norm_g, gla_w_gate2, gla_b_gate,
        gla_out_norm_g.shape[0])
    gla = _gla(gq, gk, gv, gr, gl, gla_out_norm_g)
    qk_bound = (FOX_ROUNDING_SLACK * head_dim * LOG2E / math.sqrt(head_dim)
                * jnp.max(jnp.abs(fox_q_norm_g)) * jnp.max(jnp.abs(fox_k_norm_g)))
    fox = _fox(qa, ka, va, c_edges, qk_bound, head_dim)
    km, vm = _mem_kv(mem, norm_mem_g, xattn_wkv, xattn_k_norm_g)
    h = _mix(x, fox, gla, w_out, head_dim, norm_xattn_g, xattn_wq, xattn_q_norm_g, km, vm, xattn_wo)
    return _mlp(h, norm_mlp_g, mlp_w1, mlp_w2)
```

```python
import functools
import math

import numpy as np
import jax
import jax.numpy as jnp
from jax import lax
from jax.experimental import pallas as pl
from jax.experimental.pallas import tpu as pltpu

F32 = jnp.float32
BF16 = jnp.bfloat16

EPS = 1e-6
LOG2E = 1.4426950408889634

GLA_CHUNK = 64
GLA_HEADS = 4
GLA_TAU = 16.0

LANES = 128
MXU_K = 256
VMEM_LIMIT_BYTES = 56 * 1024 * 1024

ROW_TILE = 512
MIX_SPLITS = 2
FOX_TILE = 512
FOX_UNROLL = 8
FOX_DIAG_UNROLL = 4
FOX_FINISH_BLOCKS = 4
FOX_ZERO_LOG2 = 160.0
FOX_BOUND_SLACK = 2.0
FOX_ROUNDING_SLACK = 1.02
AUG0 = 64


def _const_spec(shape):
    return pl.BlockSpec(shape, lambda *_: (0,) * len(shape))


def _params(semantics, flags=None):
    return pltpu.CompilerParams(dimension_semantics=semantics,
                                vmem_limit_bytes=VMEM_LIMIT_BYTES, flags=flags)


def _split3(c):
    hi = c.astype(BF16)
    r = c - hi.astype(F32)
    mid = r.astype(BF16)
    lo = (r - mid.astype(F32)).astype(BF16)
    return hi, mid, lo


def _log_sigmoid(z):
    return -(jnp.maximum(-z, 0.0) + jnp.log1p(jnp.exp(-jnp.abs(z))))


def _dot(a, b):
    return jnp.dot(a, b, preferred_element_type=F32)


def _dot_nt(a, b):
    return lax.dot_general(a, b, (((1,), (1,)), ((), ())), preferred_element_type=F32)


def _dot_tn(a, b):
    return lax.dot_general(a, b, (((0,), (0,)), ((), ())), preferred_element_type=F32)


def _in_proj_kernel(x_ref, g_ref, w_ref, ones_ref, tri_ref, pq_ref, pk_ref, qg_ref, kg_ref,
                    bf_ref, wg2_ref, bg_ref, e64_ref,
                    qa_ref, ka_ref, va_ref, gq_ref, gk_ref, gv_ref, gr_ref, gl_ref, ce_ref,
                    carry_ref, *, fox_w, gkw, gvw, n_fox, head_dim):
    @pl.when(pl.program_id(1) == 0)
    def _():
        carry_ref[...] = jnp.zeros_like(carry_ref)

    x = x_ref[...]
    xn = (x * lax.rsqrt(jnp.mean(x * x, axis=-1, keepdims=True) + EPS) * g_ref[...]).astype(BF16)

    o_k = fox_w
    o_v = 2 * fox_w
    o_gq = 3 * fox_w
    o_gk = o_gq + gkw
    o_gv = o_gk + gkw
    o_gr = o_gv + gvw
    o_sm = o_gr + gvw

    def proj(lo, width):
        return _dot(xn, w_ref[:, lo:lo + width])

    def head_norm(f, gain_row):
        f2 = (f * f).astype(BF16)
        ssq = jnp.concatenate([_dot(f2[:, c:c + MXU_K], ones_ref[...])
                               for c in range(0, fox_w, MXU_K)], axis=1)
        return f * lax.rsqrt(ssq * (1.0 / head_dim) + EPS) * gain_row

    def spread(f):
        cols = []
        for h in range(n_fox):
            c = f[:, (h // 2) * LANES:(h // 2 + 1) * LANES]
            if h % 2:
                c = pltpu.roll(c, LANES // 2, axis=1)
            cols.append(c)
        return jnp.concatenate(cols, axis=1)

    gr_sm = proj(o_gr, gvw + LANES)
    sm = gr_sm[:, gvw:]
    lane = lax.broadcasted_iota(jnp.int32, sm.shape, 1)
    lf = jnp.where(lane < n_fox, _log_sigmoid(sm + bf_ref[...]), 0.0)
    fq = proj(0, fox_w)
    fk = proj(o_k, fox_w)
    fq = head_norm(fq, qg_ref[...])
    fv = proj(o_v, fox_w)
    fk = head_norm(fk, kg_ref[...])

    def pack3(v):
        hi, mid, lo = _split3(v)
        return (hi.astype(F32) + pltpu.roll(mid.astype(F32), n_fox, axis=1)
                + pltpu.roll(lo.astype(F32), 2 * n_fox, axis=1))

    cs = _dot(tri_ref[...], pack3(lf).astype(BF16))
    cs = cs + pltpu.roll(cs, LANES - n_fox, axis=1) + pltpu.roll(cs, LANES - 2 * n_fox, axis=1)
    c = jnp.where(lane < n_fox, cs, 0.0) + carry_ref[...]
    tm = c.shape[0]
    carry_ref[...] = c[tm - 1:tm, :]
    c2 = c * LOG2E
    cpk = jnp.where(lane == 3 * n_fox, 1.0, pack3(c2)).astype(BF16)
    ce_ref[...] = jnp.concatenate([c2[0:1], c2[tm - 1:tm], jnp.zeros((6, LANES), F32)], axis=0)

    g_qk = proj(o_gq, 2 * gkw)
    gq_ref[...] = g_qk[:, :gkw].astype(BF16)
    gk_ref[...] = g_qk[:, gkw:].astype(BF16)
    gv_ref[...] = proj(o_gv, gvw).astype(BF16)
    gr_ref[...] = gr_sm[:, :gvw].astype(BF16)
    gate = _dot(sm.astype(BF16), wg2_ref[...]) + bg_ref[...]
    gl_ref[...] = _log_sigmoid(gate) * (1.0 / GLA_TAU)

    wide_lane = lax.broadcasted_iota(jnp.int32, (1, n_fox * LANES), 1)
    is_head_dim = (wide_lane & (LANES - 1)) < head_dim
    va_ref[...] = jnp.where(is_head_dim, spread(fv), e64_ref[...]).astype(BF16)
    qa_ref[...] = jnp.where(is_head_dim, spread(fq), _dot(cpk, pq_ref[...])).astype(BF16)
    ka_ref[...] = jnp.where(is_head_dim, spread(fk), _dot(cpk, pk_ref[...])).astype(BF16)


def _in_proj(x, norm_g, w_in, fox_b_f, fox_q_norm_g, fox_k_norm_g, gla_w_gate2, gla_b_gate, gvw):
    B, S, D = x.shape
    n_fox = fox_b_f.shape[0]
    head_dim = fox_q_norm_g.shape[0]
    fox_w = n_fox * head_dim
    rank, gkw = gla_w_gate2.shape
    tm = min(ROW_TILE, S)
    assert S % tm == 0 and head_dim == LANES // 2 and n_fox % 2 == 0
    assert 3 * n_fox + 1 <= LANES and n_fox + rank <= LANES

    sizes = (fox_w, fox_w, fox_w, n_fox, gkw, gkw, gvw, rank, gvw)
    offs = np.concatenate([[0], np.cumsum(sizes)])
    w_bf = w_in.astype(BF16)
    wq, wk, wv, wf, wgq, wgk, wgv, wlr, wgr = [w_bf[:, offs[i]:offs[i + 1]] for i in range(9)]
    w_all = jnp.concatenate([wq, wk, wv, wgq, wgk, wgv, wgr, wf, wlr,
                             jnp.zeros((D, LANES - n_fox - rank), BF16)], axis=1)
    width = w_all.shape[1]

    scale = (1.0 / math.sqrt(head_dim)) * LOG2E
    qg = (jnp.tile(fox_q_norm_g, n_fox) * scale).reshape(1, fox_w)
    kg = jnp.tile(fox_k_norm_g, n_fox).reshape(1, fox_w)
    bf = jnp.zeros((1, LANES), F32).at[0, :n_fox].set(fox_b_f)
    wg2 = jnp.zeros((LANES, gkw), F32).at[n_fox:n_fox + rank].set(gla_w_gate2).astype(BF16)
    bg = gla_b_gate.reshape(1, gkw)

    assert fox_w % MXU_K == 0 and MXU_K % head_dim == 0
    grp = np.arange(MXU_K) // head_dim
    ones_blk = jnp.asarray(grp[:, None] == grp[None, :], BF16)
    r = np.arange(tm)
    tri = jnp.asarray(r[None, :] <= r[:, None], BF16)
    pq = np.zeros((LANES, n_fox * LANES), np.float32)
    pk = np.zeros((LANES, n_fox * LANES), np.float32)
    e64 = np.zeros((1, n_fox * LANES), np.float32)
    for h in range(n_fox):
        base = h * LANES + AUG0
        for part in range(3):
            pq[part * n_fox + h, base + part] = 1.0
            pk[part * n_fox + h, base + 3 + part] = -1.0
            pq[3 * n_fox, base + 3 + part] = 1.0
            pk[3 * n_fox, base + part] = 1.0
        e64[0, base] = 1.0
    pq = jnp.asarray(pq, BF16)
    pk = jnp.asarray(pk, BF16)
    e64 = jnp.asarray(e64)

    row = lambda c: pl.BlockSpec((None, tm, c), lambda b, i: (b, i, 0))
    kern = functools.partial(_in_proj_kernel, fox_w=fox_w, gkw=gkw, gvw=gvw, n_fox=n_fox,
                             head_dim=head_dim)
    aug_w = n_fox * LANES
    out_shapes = [jax.ShapeDtypeStruct((B, S, aug_w), BF16)] * 3 + [
        jax.ShapeDtypeStruct((B, S, gkw), BF16), jax.ShapeDtypeStruct((B, S, gkw), BF16),
        jax.ShapeDtypeStruct((B, S, gvw), BF16), jax.ShapeDtypeStruct((B, S, gvw), BF16),
        jax.ShapeDtypeStruct((B, S, gkw), F32),
        jax.ShapeDtypeStruct((B, S // tm, 8, LANES), F32)]
    return pl.pallas_call(
        kern,
        out_shape=out_shapes,
        grid=(B, S // tm),
        in_specs=[row(D), _const_spec((1, D)), _const_spec((D, width)),
                  _const_spec((MXU_K, MXU_K)), _const_spec((tm, tm)),
                  _const_spec((LANES, aug_w)), _const_spec((LANES, aug_w)),
                  _const_spec((1, fox_w)), _const_spec((1, fox_w)), _const_spec((1, LANES)),
                  _const_spec((LANES, gkw)), _const_spec((1, gkw)), _const_spec((1, aug_w))],
        out_specs=[row(aug_w), row(aug_w), row(aug_w), row(gkw), row(gkw), row(gvw), row(gvw),
                   row(gkw), pl.BlockSpec((None, None, 8, LANES), lambda b, i: (b, i, 0, 0))],
        scratch_shapes=[pltpu.VMEM((1, LANES), F32)],
        compiler_params=_params(("arbitrary", "arbitrary")),
        name="in_proj",
    )(x, norm_g.reshape(1, D), w_all, ones_blk, tri, pq, pk, qg, kg, bf, wg2, bg, e64)


def _gla_kernel(gq_ref, gk_ref, gv_ref, gr_ref, gl_ref, tri3_ref, gain_ref, out_ref,
                st_ref, o_ref, *, dk, dv):
    @pl.when(pl.program_id(1) == 0)
    def _():
        st_ref[...] = jnp.zeros_like(st_ref)

    T, kw = gl_ref.shape
    vw = gv_ref.shape[1]
    n_heads = kw // dk
    C = GLA_CHUNK

    n_chunks = T // C
    chunks = [slice(c * C, (c + 1) * C) for c in range(n_chunks)]

    hi, mid, lo = _split3(gl_ref[...])
    tri3 = tri3_ref[...]
    bcs = [_dot(tri3, jnp.concatenate([hi[r], mid[r], lo[r]], axis=0)) for r in chunks]
    b_last = [b[C - 1:C] for b in bcs]
    bc = jnp.concatenate(bcs, axis=0)
    k = gk_ref[...].astype(F32)
    q_dec = gq_ref[...].astype(F32) * (dk ** -0.5) * jnp.exp(bc)
    k_dec = (k * jnp.exp(-bc)).astype(BF16)

    klane = lax.broadcasted_iota(jnp.int32, (1, kw), 1) // dk
    row_h = lax.broadcasted_iota(jnp.int32, (n_heads * C, C), 0)
    col = lax.broadcasted_iota(jnp.int32, (n_heads * C, C), 1)
    tril = col <= (row_h & (C - 1))
    bd = (lax.broadcasted_iota(jnp.int32, (vw, kw), 0) // dv
          == lax.broadcasted_iota(jnp.int32, (vw, kw), 1) // dk)

    d_sts = [_dot_tn(gv_ref[r, :], (k[r] * jnp.exp(bl - b)).astype(BF16))
             for r, b, bl in zip(chunks, bcs, b_last)]
    st = st_ref[...]
    sts = []
    for d_st, bl in zip(d_sts, b_last):
        sts.append(st.astype(BF16))
        st = st * jnp.exp(bl) + jnp.where(bd, d_st, 0.0)
    st_ref[...] = st

    for r, st_c in zip(chunks, sts):
        qd = q_dec[r]
        lhs = jnp.concatenate([jnp.where(klane == h, qd, 0.0) for h in range(n_heads)],
                              axis=0).astype(BF16)
        a = jnp.where(tril, _dot_nt(lhs, k_dec[r]), 0.0).astype(BF16)
        v = gv_ref[r, :]
        o_intra = jnp.concatenate(
            [_dot(a[h * C:(h + 1) * C], v[:, h * dv:(h + 1) * dv]) for h in range(n_heads)],
            axis=1)
        o_ref[r, :] = o_intra + _dot_nt(qd.astype(BF16), st_c)

    o = o_ref[...]
    gr = gr_ref[...].astype(F32)
    gain = gain_ref[...]
    outs = []
    for h in range(n_heads):
        oh = o[:, h * dv:(h + 1) * dv]
        on = oh * lax.rsqrt(jnp.mean(oh * oh, axis=-1, keepdims=True) + EPS)
        outs.append(on * gain[:, h * dv:(h + 1) * dv])
    out_ref[...] = (jnp.concatenate(outs, axis=1) * (gr * jax.nn.sigmoid(gr))).astype(BF16)


def _gla(gq, gk, gv, gr, gl, out_norm_g):
    B, S, kw = gq.shape
    vw = gv.shape[2]
    dk, dv = kw // GLA_HEADS, vw // GLA_HEADS
    T = min(ROW_TILE, S)
    assert S % T == 0 and T % GLA_CHUNK == 0
    r = np.arange(GLA_CHUNK)
    tri3 = jnp.asarray(np.tile(r[None, :] <= r[:, None], (1, 3)), BF16)
    row = lambda c: pl.BlockSpec((None, T, c), lambda b, i: (b, i, 0))
    return pl.pallas_call(
        functools.partial(_gla_kernel, dk=dk, dv=dv),
        out_shape=jax.ShapeDtypeStruct((B, S, vw), BF16),
        grid=(B, S // T),
        in_specs=[row(kw), row(kw), row(vw), row(vw), row(kw),
                  _const_spec((GLA_CHUNK, 3 * GLA_CHUNK)), _const_spec((1, vw))],
        out_specs=row(vw),
        scratch_shapes=[pltpu.VMEM((vw, kw), F32), pltpu.VMEM((T, vw), F32)],
        compiler_params=_params(("arbitrary", "arbitrary")),
        name="gla",
    )(gq, gk, gv, gr, gl, tri3, out_norm_g.reshape(1, vw))


def _fox_kernel(tab_q, tab_k, count, q_ref, k_ref, v_ref, o_ref, *scratch, head_dim, nq):
    pair = pl.program_id(0) * pl.num_programs(1) + pl.program_id(1)
    for hh in range(2):
        lanes = pl.ds(hh * LANES, LANES)
        _fox_head(tab_q, tab_k, count, 2 * pair + hh, hh, q_ref.at[:, lanes], k_ref.at[:, lanes],
                  v_ref.at[:, lanes], o_ref, *scratch, head_dim=head_dim, nq=nq)


def _fox_head(tab_q, tab_k, count, head_idx, slot, q_ref, k_ref, v_ref, o_ref, m_all, acc_all,
              s0, s1, p0, p1, al0, al1, *, head_dim, nq):
    T = s0.shape[0]
    s_buf, p_buf, al_buf = (s0, s1), (p0, p1), (al0, al1)
    depth = 3

    def rows(blk):
        start = blk * T
        return pl.ds(start if isinstance(blk, int) else pl.multiple_of(start, T), T)

    m_all[rows(nq), :] = jnp.zeros((T, LANES), F32)
    acc_all[rows(nq), :] = jnp.zeros((T, LANES), F32)
    for ref in (s1, p0, al0):
        ref[...] = jnp.zeros_like(ref)

    half = T // 2

    def half_rows(blk, i):
        return pl.ds(pl.multiple_of(blk * T, T) + i * half, half)

    def scores(qi, kj, par, diag):
        qr = jnp.minimum(qi, nq - 1)
        if not diag:
            s_buf[par][...] = _dot_nt(q_ref[rows(qr), :], k_ref[rows(kj), :])
            return
        top = _dot_nt(q_ref[half_rows(qr, 0), :], k_ref[half_rows(kj, 0), :])
        bot = _dot_nt(q_ref[half_rows(qr, 1), :], k_ref[rows(kj), :])
        def causal(shape, offset):
            row = lax.broadcasted_iota(jnp.int32, shape, 0)
            col = lax.broadcasted_iota(jnp.int32, shape, 1)
            return col <= row + offset

        s_buf[par][:half, :half] = jnp.where(causal((half, half), 0), top, -jnp.inf)
        s_buf[par][:half, half:] = jnp.full((half, half), -jnp.inf, F32)
        s_buf[par][half:, :] = jnp.where(causal((half, T), half), bot, -jnp.inf)

    def softmax(qi, par, diag):
        m_cur = jnp.broadcast_to(jnp.max(s_buf[par][...], axis=-1, keepdims=True), (T, LANES))
        if diag:
            m_new = m_cur
        else:
            m_prev = m_all[rows(qi), :]
            m_new = jnp.maximum(m_prev, m_cur)
            al_buf[par][...] = jnp.exp2(m_prev - m_new)
        m_all[rows(qi), :] = m_new
        for g in range(T // LANES):
            cols = slice(g * LANES, (g + 1) * LANES)
            p_buf[par][:, cols] = jnp.exp2(s_buf[par][:, cols] - m_new).astype(BF16)

    def pv(qi, kj, par, diag):
        start = pl.multiple_of(kj * T, T)

        def p_dot_v(r0, r1, n_cols):
            parts = [_dot(p_buf[par][r0:r1, c:c + MXU_K], v_ref[pl.ds(start + c, MXU_K), :])
                     for c in range(0, n_cols, MXU_K)]
            return sum(parts[1:], parts[0])

        if diag:
            acc_all[half_rows(qi, 0), :] = p_dot_v(0, half, half)
            acc_all[half_rows(qi, 1), :] = p_dot_v(half, T, T)
        else:
            acc_all[rows(qi), :] = al_buf[par][...] * acc_all[rows(qi), :] + p_dot_v(0, T, T)

    def step(c, blocks, diag):
        (q0, k0), (q1, _), (q2, k2) = blocks
        pv(q2, k2, c % 2, diag)
        scores(q0, k0, c % 2, diag)
        softmax(q1, (c - 1) % 2, diag)

    def run(n_blocks, block_at, diag, steps_per_trip):
        spare = (jnp.int32(nq), jnp.int32(0))

        def trip(u, carry):
            hist = [carry[0:2], carry[2:4]]
            for c in range(steps_per_trip):
                nxt = block_at(u * steps_per_trip + c)
                step(c, [nxt] + hist, diag)
                hist = [nxt] + hist[:1]
            return tuple(x for h in hist for x in h)

        n_trips = (n_blocks + (depth - 2 + steps_per_trip)) // steps_per_trip
        lax.fori_loop(0, n_trips, trip, spare * 2)

    def diag_block(n):
        return jnp.minimum(n, nq), jnp.minimum(n, nq - 1)

    def lower_block(n):
        return tab_q[head_idx, n], tab_k[head_idx, n]

    run(nq, diag_block, True, FOX_DIAG_UNROLL)
    if nq > 1:
        run(count[head_idx], lower_block, False, FOX_UNROLL)

    per_trip = math.gcd(nq, FOX_FINISH_BLOCKS)

    def finish(u, carry):
        for i in range(per_trip):
            r = rows(u * per_trip + i)
            acc = acc_all[r, :]
            lane = lax.broadcasted_iota(jnp.int32, acc.shape, 1)
            inv_l = 1.0 / acc[:, head_dim:head_dim + 1]
            out = jnp.where(lane < head_dim, acc * inv_l, 0.0)
            if slot == 0:
                o_ref[r, :] = out.astype(BF16)
            else:
                first = o_ref[r, :].astype(F32)
                o_ref[r, :] = jnp.where(lane < head_dim, first,
                                        pltpu.roll(out, head_dim, axis=1)).astype(BF16)
        return carry

    lax.fori_loop(0, nq // per_trip, finish, 0)


def _fox_block_table(c_edges, qk_bound, n_heads, nq):
    B = c_edges.shape[0]
    c_first = c_edges[:, :, 0, :n_heads]
    c_last = c_edges[:, :, 1, :n_heads]
    gap = c_first[:, :, None, :] - c_last[:, None, :, :]
    qi = np.arange(nq)[:, None]
    kj = np.arange(nq)[None, :]
    lower = jnp.asarray(kj < qi)[None, :, :, None]
    keep = lower & (2.0 * qk_bound + gap + FOX_BOUND_SLACK > -FOX_ZERO_LOG2)
    keep = jnp.transpose(keep, (0, 3, 1, 2)).reshape(B * n_heads, nq * nq)
    count = jnp.sum(keep, axis=1).astype(jnp.int32)
    order = jnp.argsort(~keep, axis=1, stable=True).astype(jnp.int32)
    n_tab = -(-(nq * (nq - 1) // 2 + 2) // FOX_UNROLL) * FOX_UNROLL
    order = jnp.pad(order, ((0, 0), (0, max(0, n_tab - nq * nq))))[:, :n_tab]
    live = jnp.arange(n_tab)[None, :] < count[:, None]
    tab_q = jnp.where(live, order // nq, nq).astype(jnp.int32)
    tab_k = jnp.where(live, order % nq, 0).astype(jnp.int32)
    return tab_q, tab_k, count


def _fox(qa, ka, va, c_edges, qk_bound, head_dim):
    B, S, W = qa.shape
    H = W // LANES
    T = min(FOX_TILE, S)
    assert S % T == 0 and c_edges.shape[1] * T == S
    nq = S // T
    tab_q, tab_k, count = _fox_block_table(c_edges, qk_bound, H, nq)
    assert H % 2 == 0 and 2 * head_dim == LANES
    pair_in = pl.BlockSpec((None, S, 2 * LANES), lambda b, hp, *_: (b, 0, hp))
    pair_out = pl.BlockSpec((None, S, LANES), lambda b, hp, *_: (b, 0, hp))
    state = pltpu.VMEM(((nq + 1) * T, LANES), F32)
    return pl.pallas_call(
        functools.partial(_fox_kernel, head_dim=head_dim, nq=nq),
        out_shape=jax.ShapeDtypeStruct((B, S, H * head_dim), BF16),
        grid_spec=pltpu.PrefetchScalarGridSpec(
            num_scalar_prefetch=3,
            grid=(B, H // 2),
            in_specs=[pair_in, pair_in, pair_in],
            out_specs=pair_out,
            scratch_shapes=([state, state] + [pltpu.VMEM((T, T), F32)] * 2
                            + [pltpu.VMEM((T, T), BF16)] * 2
                            + [pltpu.VMEM((T, LANES), F32)] * 2)),
        compiler_params=_params(("arbitrary", "arbitrary")),
        name="fox",
    )(tab_q, tab_k, count, qa, ka, va)


def _mem_kv_kernel(mem_ref, g_ref, w_ref, kg_ref, k_ref, v_ref, *, xd):
    x = mem_ref[...]
    D = x.shape[1]
    xn = (x * lax.rsqrt(jnp.mean(x * x, axis=-1, keepdims=True) + EPS) * g_ref[...]).astype(BF16)
    kg = kg_ref[...]
    for h in range(D // xd):
        kh = _dot(xn, w_ref[:, h * xd:(h + 1) * xd])
        kn = kh * lax.rsqrt(jnp.mean(kh * kh, axis=-1, keepdims=True) + EPS) * kg
        k_ref[:, h * xd:(h + 1) * xd] = kn.astype(BF16)
    v_ref[...] = _dot(xn, w_ref[:, D:]).astype(BF16)


def _mem_kv(mem, norm_mem_g, wkv, k_norm_g):
    B, M, D = mem.shape
    xd = k_norm_g.shape[0]
    kg = (k_norm_g * (LOG2E / math.sqrt(xd))).reshape(1, xd)
    blk = pl.BlockSpec((None, M, D), lambda b: (b, 0, 0))
    return pl.pallas_call(
        functools.partial(_mem_kv_kernel, xd=xd),
        out_shape=[jax.ShapeDtypeStruct((B, M, D), BF16)] * 2,
        grid=(B,),
        in_specs=[blk, _const_spec((1, D)), _const_spec((D, 2 * D)), _const_spec((1, xd))],
        out_specs=[blk, blk],
        compiler_params=_params(("arbitrary",)),
        name="mem_kv",
    )(mem, norm_mem_g.reshape(1, D), wkv.astype(BF16), kg)


def _mix_kernel(x_ref, fox_ref, gla_ref, wof_ref, wog_ref, g_ref, wq_ref, qg_ref, km_ref, vm_ref,
                wo_ref, h_ref, *, xd):
    tm, D = x_ref.shape
    splits = [slice(i * tm // MIX_SPLITS, (i + 1) * tm // MIX_SPLITS) for i in range(MIX_SPLITS)]
    heads = [slice(hd * xd, (hd + 1) * xd) for hd in range(D // xd)]
    qg = qg_ref[...]

    def rms(v, gain):
        return v * lax.rsqrt(jnp.mean(v * v, axis=-1, keepdims=True) + EPS) * gain

    hs = [x_ref[r, :] + _dot(fox_ref[r, :], wof_ref[...]) + _dot(gla_ref[r, :], wog_ref[...])
          for r in splits]
    hns = [rms(h, g_ref[...]).astype(BF16) for h in hs]
    qs = [_dot(hn, wq_ref[...]) for hn in hns]
    qns = [[rms(q[:, sl], qg).astype(BF16) for sl in heads] for q in qs]
    ss = [[_dot_nt(qn, km_ref[:, sl]) for qn, sl in zip(row, heads)] for row in qns]
    ps = [[jnp.exp2(s - jnp.max(s, axis=-1, keepdims=True)) for s in row] for row in ss]
    os = [[(_dot(p.astype(BF16), vm_ref[:, sl]) * (1.0 / jnp.sum(p, axis=-1, keepdims=True))
            ).astype(BF16) for p, sl in zip(row, heads)] for row in ps]
    for r, h, o in zip(splits, hs, os):
        h_ref[r, :] = h + _dot(jnp.concatenate(o, axis=1), wo_ref[...])


def _mix(x, fox, gla, w_out, norm_g, wq, q_norm_g, km, vm, wo):
    B, S, D = x.shape
    fw = fox.shape[2]
    gw = gla.shape[2]
    M = km.shape[1]
    xd = q_norm_g.shape[0]
    tm = min(ROW_TILE, S)
    w_out_bf = w_out.astype(BF16)
    wof, wog = w_out_bf[:fw], w_out_bf[fw:]
    row = lambda c: pl.BlockSpec((None, tm, c), lambda b, i: (b, i, 0))
    mem = pl.BlockSpec((None, M, D), lambda b, i: (b, 0, 0))
    return pl.pallas_call(
        functools.partial(_mix_kernel, xd=xd),
        out_shape=jax.ShapeDtypeStruct((B, S, D), F32),
        grid=(B, S // tm),
        in_specs=[row(D), row(fw), row(gw), _const_spec((fw, D)), _const_spec((gw, D)),
                  _const_spec((1, D)), _const_spec((D, D)), _const_spec((1, xd)), mem, mem,
                  _const_spec((D, D))],
        out_specs=row(D),
        compiler_params=_params(("arbitrary", "arbitrary")),
        name="mix",
    )(x, fox, gla, wof, wog, norm_g.reshape(1, D), wq.astype(BF16), q_norm_g.reshape(1, xd),
      km, vm, wo.astype(BF16))


def _mlp_kernel(h_ref, g_ref, w1_ref, w2_ref, y_ref, *, slab):
    h = h_ref[...]
    hn = (h * lax.rsqrt(jnp.mean(h * h, axis=-1, keepdims=True) + EPS) * g_ref[...]).astype(BF16)
    y = h
    for j in range(w1_ref.shape[1] // slab):
        u = jnp.maximum(_dot(hn, w1_ref[:, j * slab:(j + 1) * slab]), 0.0)
        y = y + _dot((u * u).astype(BF16), w2_ref[j * slab:(j + 1) * slab, :])
    y_ref[...] = y


def _mlp(h, norm_g, w1, w2):
    B, S, D = h.shape
    F = w1.shape[1]
    tm = min(ROW_TILE, S)
    row = pl.BlockSpec((None, tm, D), lambda b, i: (b, i, 0))
    return pl.pallas_call(
        functools.partial(_mlp_kernel, slab=min(F, 1024)),
        out_shape=jax.ShapeDtypeStruct((B, S, D), F32),
        grid=(B, S // tm),
        in_specs=[row, _const_spec((1, D)), _const_spec((D, F)), _const_spec((F, D))],
        out_specs=row,
        compiler_params=_params(("arbitrary", "arbitrary")),
        name="mlp",
    )(h, norm_g.reshape(1, D), w1.astype(BF16), w2.astype(BF16))


def kernel(x, mem, norm_mix_g, w_in, fox_b_f, fox_q_norm_g, fox_k_norm_g, gla_w_gate2, gla_b_gate,
           gla_out_norm_g, w_out, norm_xattn_g, norm_mem_g, xattn_wq, xattn_wkv, xattn_q_norm_g,
           xattn_k_norm_g, xattn_wo, norm_mlp_g, mlp_w1, mlp_w2):
    head_dim = fox_q_norm_g.shape[0]
    qa, ka, va, gq, gk, gv, gr, gl, c_edges = _in_proj(
        x, norm_mix_g, w_in, fox_b_f, fox_q_norm_g, fox_k_norm_g, gla_w_gate2, gla_b_gate,
        gla_out_norm_g.shape[0])
    gla = _gla(gq, gk, gv, gr, gl, gla_out_norm_g)
    qk_bound = (FOX_ROUNDING_SLACK * head_dim * LOG2E / math.sqrt(head_dim)
                * jnp.max(jnp.abs(fox_q_norm_g)) * jnp.max(jnp.abs(fox_k_norm_g)))
    fox = _fox(qa, ka, va, c_edges, qk_bound, head_dim)
    km, vm = _mem_kv(mem, norm_mem_g, xattn_wkv, xattn_k_norm_g)
    h = _mix(x, fox, gla, w_out, norm_xattn_g, xattn_wq, xattn_q_norm_g, km, vm, xattn_wo)
    return _mlp(h, norm_mlp_g, mlp_w1, mlp_w2)
```

```python
import functools
import math

import numpy as np
import jax
import jax.numpy as jnp
from jax import lax
from jax.experimental import pallas as pl
from jax.experimental.pallas import tpu as pltpu

F32 = jnp.float32
BF16 = jnp.bfloat16

EPS = 1e-6
LOG2E = 1.4426950408889634

GLA_CHUNK = 64
GLA_HEADS = 4
GLA_TAU = 16.0

LANES = 128
MXU_K = 256
VMEM_LIMIT_BYTES = 56 * 1024 * 1024

ROW_TILE = 512
MIX_SPLITS = 2
FOX_TILE = 512
FOX_UNROLL = 8
FOX_DIAG_UNROLL = 6
FOX_FINISH_BLOCKS = 4
FOX_ZERO_LOG2 = 160.0
FOX_BOUND_SLACK = 2.0
FOX_ROUNDING_SLACK = 1.02
AUG0 = 64


def _const_spec(shape):
    return pl.BlockSpec(shape, lambda *_: (0,) * len(shape))


def _params(semantics, flags=None):
    return pltpu.CompilerParams(dimension_semantics=semantics,
                                vmem_limit_bytes=VMEM_LIMIT_BYTES, flags=flags)


def _split3(c):
    hi = c.astype(BF16)
    r = c - hi.astype(F32)
    mid = r.astype(BF16)
    lo = (r - mid.astype(F32)).astype(BF16)
    return hi, mid, lo


def _log_sigmoid(z):
    return -(jnp.maximum(-z, 0.0) + jnp.log1p(jnp.exp(-jnp.abs(z))))


def _dot(a, b):
    return jnp.dot(a, b, preferred_element_type=F32)


def _dot_nt(a, b):
    return lax.dot_general(a, b, (((1,), (1,)), ((), ())), preferred_element_type=F32)


def _dot_tn(a, b):
    return lax.dot_general(a, b, (((0,), (0,)), ((), ())), preferred_element_type=F32)


def _in_proj_kernel(x_ref, g_ref, w_ref, ones_ref, tri_ref, place_ref, qg_ref, kg_ref,
                    bf_ref, wg2_ref, bg_ref, rows_ref,
                    qa_ref, ka_ref, va_ref, gq_ref, gk_ref, gv_ref, gr_ref, gl_ref, ce_ref,
                    carry_ref, *, fox_w, gkw, gvw, n_fox, head_dim):
    @pl.when(pl.program_id(1) == 0)
    def _():
        carry_ref[...] = jnp.zeros_like(carry_ref)

    x = x_ref[...]
    xn = (x * lax.rsqrt(jnp.mean(x * x, axis=-1, keepdims=True) + EPS) * g_ref[...]).astype(BF16)

    o_k = fox_w
    o_v = 2 * fox_w
    o_gq = 3 * fox_w
    o_gk = o_gq + gkw
    o_gv = o_gk + gkw
    o_gr = o_gv + gvw
    o_sm = o_gr + gvw

    def proj(lo, width):
        return _dot(xn, w_ref[:, lo:lo + width])

    def head_norm(f, gain_row):
        f2 = (f * f).astype(BF16)
        ssq = jnp.concatenate([_dot(f2[:, c:c + MXU_K], ones_ref[...])
                               for c in range(0, fox_w, MXU_K)], axis=1)
        return f * lax.rsqrt(ssq * (1.0 / head_dim) + EPS) * gain_row

    def spread(f, odd_heads_high=False):
        cols = []
        for h in range(n_fox):
            c = f[:, (h // 2) * LANES:(h // 2 + 1) * LANES]
            if h % 2 and not odd_heads_high:
                c = pltpu.roll(c, LANES // 2, axis=1)
            cols.append(c)
        return jnp.concatenate(cols, axis=1)

    gr_sm = proj(o_gr, gvw + LANES)
    sm = gr_sm[:, gvw:]
    lane = lax.broadcasted_iota(jnp.int32, sm.shape, 1)
    lf = jnp.where(lane < n_fox, _log_sigmoid(sm + bf_ref[...]), 0.0)
    fq = proj(0, fox_w)
    fk = proj(o_k, fox_w)
    fq = head_norm(fq, qg_ref[...])
    fv = proj(o_v, fox_w)
    fk = head_norm(fk, kg_ref[...])

    def pack3(v):
        hi, mid, lo = _split3(v)
        return (hi.astype(F32) + pltpu.roll(mid.astype(F32), n_fox, axis=1)
                + pltpu.roll(lo.astype(F32), 2 * n_fox, axis=1))

    cs = _dot(tri_ref[...], pack3(lf).astype(BF16))
    cs = cs + pltpu.roll(cs, LANES - n_fox, axis=1) + pltpu.roll(cs, LANES - 2 * n_fox, axis=1)
    c = jnp.where(lane < n_fox, cs, 0.0) + carry_ref[...]
    tm = c.shape[0]
    carry_ref[...] = c[tm - 1:tm, :]
    c2 = c * LOG2E
    cpk = pack3(c2).astype(BF16)
    ce_ref[...] = jnp.concatenate([c2[0:1], c2[tm - 1:tm], jnp.zeros((6, LANES), F32)], axis=0)

    g_qk = proj(o_gq, 2 * gkw)
    gq_ref[...] = g_qk[:, :gkw].astype(BF16)
    gk_ref[...] = g_qk[:, gkw:].astype(BF16)
    gv_ref[...] = proj(o_gv, gvw).astype(BF16)
    gr_ref[...] = gr_sm[:, :gvw].astype(BF16)
    gate = _dot(sm.astype(BF16), wg2_ref[...]) + bg_ref[...]
    gl_ref[...] = _log_sigmoid(gate) * (1.0 / GLA_TAU)

    head_lane = lax.broadcasted_iota(jnp.int32, (1, n_fox * LANES), 1) & (LANES - 1)
    is_head_dim = head_lane < head_dim
    is_q_c = head_lane < AUG0 + 3
    is_k_c = head_lane >= AUG0 + 3
    placed = _dot(cpk, place_ref[...])
    odd_head = (lax.broadcasted_iota(jnp.int32, (1, n_fox * LANES), 1) // LANES) % 2 == 1
    va_ref[...] = jnp.where(is_head_dim != odd_head, spread(fv, True),
                            rows_ref[0:1, :]).astype(BF16)
    qa_ref[...] = jnp.where(is_head_dim, spread(fq),
                            jnp.where(is_q_c, placed, rows_ref[1:2, :])).astype(BF16)
    ka_ref[...] = jnp.where(is_head_dim, spread(fk),
                            jnp.where(is_k_c, placed, rows_ref[2:3, :])).astype(BF16)


def _in_proj(x, norm_g, w_in, fox_b_f, fox_q_norm_g, fox_k_norm_g, gla_w_gate2, gla_b_gate, gvw):
    B, S, D = x.shape
    n_fox = fox_b_f.shape[0]
    head_dim = fox_q_norm_g.shape[0]
    fox_w = n_fox * head_dim
    rank, gkw = gla_w_gate2.shape
    tm = min(ROW_TILE, S)
    assert S % tm == 0 and head_dim == LANES // 2 and n_fox % 2 == 0
    assert 3 * n_fox <= LANES and n_fox + rank <= LANES

    sizes = (fox_w, fox_w, fox_w, n_fox, gkw, gkw, gvw, rank, gvw)
    offs = np.concatenate([[0], np.cumsum(sizes)])
    w_bf = w_in.astype(BF16)
    wq, wk, wv, wf, wgq, wgk, wgv, wlr, wgr = [w_bf[:, offs[i]:offs[i + 1]] for i in range(9)]
    w_all = jnp.concatenate([wq, wk, wv, wgq, wgk, wgv, wgr, wf, wlr,
                             jnp.zeros((D, LANES - n_fox - rank), BF16)], axis=1)
    width = w_all.shape[1]

    scale = (1.0 / math.sqrt(head_dim)) * LOG2E
    qg = (jnp.tile(fox_q_norm_g, n_fox) * scale).reshape(1, fox_w)
    kg = jnp.tile(fox_k_norm_g, n_fox).reshape(1, fox_w)
    bf = jnp.zeros((1, LANES), F32).at[0, :n_fox].set(fox_b_f)
    wg2 = jnp.zeros((LANES, gkw), F32).at[n_fox:n_fox + rank].set(gla_w_gate2).astype(BF16)
    bg = gla_b_gate.reshape(1, gkw)

    assert fox_w % MXU_K == 0 and MXU_K % head_dim == 0
    grp = np.arange(MXU_K) // head_dim
    ones_blk = jnp.asarray(grp[:, None] == grp[None, :], BF16)
    r = np.arange(tm)
    tri = jnp.asarray(r[None, :] <= r[:, None], BF16)
    place = np.zeros((LANES, n_fox * LANES), np.float32)
    const_rows = np.zeros((8, n_fox * LANES), np.float32)
    for h in range(n_fox):
        base = h * LANES + AUG0
        for part in range(3):
            place[part * n_fox + h, base + part] = 1.0
            place[part * n_fox + h, base + 3 + part] = -1.0
            const_rows[1, base + 3 + part] = 1.0
            const_rows[2, base + part] = 1.0
        const_rows[0, base if h % 2 == 0 else h * LANES] = 1.0
    place = jnp.asarray(place, BF16)
    const_rows = jnp.asarray(const_rows)

    row = lambda c: pl.BlockSpec((None, tm, c), lambda b, i: (b, i, 0))
    kern = functools.partial(_in_proj_kernel, fox_w=fox_w, gkw=gkw, gvw=gvw, n_fox=n_fox,
                             head_dim=head_dim)
    aug_w = n_fox * LANES
    out_shapes = [jax.ShapeDtypeStruct((B, S, aug_w), BF16)] * 3 + [
        jax.ShapeDtypeStruct((B, S, gkw), BF16), jax.ShapeDtypeStruct((B, S, gkw), BF16),
        jax.ShapeDtypeStruct((B, S, gvw), BF16), jax.ShapeDtypeStruct((B, S, gvw), BF16),
        jax.ShapeDtypeStruct((B, S, gkw), F32),
        jax.ShapeDtypeStruct((B, S // tm, 8, LANES), F32)]
    return pl.pallas_call(
        kern,
        out_shape=out_shapes,
        grid=(B, S // tm),
        in_specs=[row(D), _const_spec((1, D)), _const_spec((D, width)),
                  _const_spec((MXU_K, MXU_K)), _const_spec((tm, tm)),
                  _const_spec((LANES, aug_w)),
                  _const_spec((1, fox_w)), _const_spec((1, fox_w)), _const_spec((1, LANES)),
                  _const_spec((LANES, gkw)), _const_spec((1, gkw)), _const_spec((8, aug_w))],
        out_specs=[row(aug_w), row(aug_w), row(aug_w), row(gkw), row(gkw), row(gvw), row(gvw),
                   row(gkw), pl.BlockSpec((None, None, 8, LANES), lambda b, i: (b, i, 0, 0))],
        scratch_shapes=[pltpu.VMEM((1, LANES), F32)],
        compiler_params=_params(("arbitrary", "arbitrary")),
        name="in_proj",
    )(x, norm_g.reshape(1, D), w_all, ones_blk, tri, place, qg, kg, bf, wg2, bg, const_rows)


def _gla_kernel(gq_ref, gk_ref, gv_ref, gr_ref, gl_ref, tri3_ref, gain_ref, out_ref,
                st_ref, o_ref, *, dk, dv):
    @pl.when(pl.program_id(1) == 0)
    def _():
        st_ref[...] = jnp.zeros_like(st_ref)

    T, kw = gl_ref.shape
    vw = gv_ref.shape[1]
    n_heads = kw // dk
    C = GLA_CHUNK

    n_chunks = T // C
    chunks = [slice(c * C, (c + 1) * C) for c in range(n_chunks)]

    hi, mid, lo = _split3(gl_ref[...])
    tri3 = tri3_ref[...]
    bcs = [_dot(tri3, jnp.concatenate([hi[r], mid[r], lo[r]], axis=0)) for r in chunks]
    b_last = [b[C - 1:C] for b in bcs]
    bc = jnp.concatenate(bcs, axis=0)
    k = gk_ref[...].astype(F32)
    q_dec = gq_ref[...].astype(F32) * (dk ** -0.5) * jnp.exp(bc)
    k_dec = (k * jnp.exp(-bc)).astype(BF16)

    klane = lax.broadcasted_iota(jnp.int32, (1, kw), 1) // dk
    row_h = lax.broadcasted_iota(jnp.int32, (n_heads * C, C), 0)
    col = lax.broadcasted_iota(jnp.int32, (n_heads * C, C), 1)
    tril = col <= (row_h & (C - 1))
    bd = (lax.broadcasted_iota(jnp.int32, (vw, kw), 0) // dv
          == lax.broadcasted_iota(jnp.int32, (vw, kw), 1) // dk)

    d_sts = [_dot_tn(gv_ref[r, :], (k[r] * jnp.exp(bl - b)).astype(BF16))
             for r, b, bl in zip(chunks, bcs, b_last)]
    st = st_ref[...]
    sts = []
    for d_st, bl in zip(d_sts, b_last):
        sts.append(st.astype(BF16))
        st = st * jnp.exp(bl) + jnp.where(bd, d_st, 0.0)
    st_ref[...] = st

    for r, st_c in zip(chunks, sts):
        qd = q_dec[r]
        lhs = jnp.concatenate([jnp.where(klane == h, qd, 0.0) for h in range(n_heads)],
                              axis=0).astype(BF16)
        a = jnp.where(tril, _dot_nt(lhs, k_dec[r]), 0.0).astype(BF16)
        v = gv_ref[r, :]
        o_intra = jnp.concatenate(
            [_dot(a[h * C:(h + 1) * C], v[:, h * dv:(h + 1) * dv]) for h in range(n_heads)],
            axis=1)
        o_ref[r, :] = o_intra + _dot_nt(qd.astype(BF16), st_c)

    o = o_ref[...]
    gr = gr_ref[...].astype(F32)
    gain = gain_ref[...]
    outs = []
    for h in range(n_heads):
        oh = o[:, h * dv:(h + 1) * dv]
        on = oh * lax.rsqrt(jnp.mean(oh * oh, axis=-1, keepdims=True) + EPS)
        outs.append(on * gain[:, h * dv:(h + 1) * dv])
    out_ref[...] = (jnp.concatenate(outs, axis=1) * (gr * jax.nn.sigmoid(gr))).astype(BF16)


def _gla(gq, gk, gv, gr, gl, out_norm_g):
    B, S, kw = gq.shape
    vw = gv.shape[2]
    dk, dv = kw // GLA_HEADS, vw // GLA_HEADS
    T = min(ROW_TILE, S)
    assert S % T == 0 and T % GLA_CHUNK == 0
    r = np.arange(GLA_CHUNK)
    tri3 = jnp.asarray(np.tile(r[None, :] <= r[:, None], (1, 3)), BF16)
    row = lambda c: pl.BlockSpec((None, T, c), lambda b, i: (b, i, 0))
    return pl.pallas_call(
        functools.partial(_gla_kernel, dk=dk, dv=dv),
        out_shape=jax.ShapeDtypeStruct((B, S, vw), BF16),
        grid=(B, S // T),
        in_specs=[row(kw), row(kw), row(vw), row(vw), row(kw),
                  _const_spec((GLA_CHUNK, 3 * GLA_CHUNK)), _const_spec((1, vw))],
        out_specs=row(vw),
        scratch_shapes=[pltpu.VMEM((vw, kw), F32), pltpu.VMEM((T, vw), F32)],
        compiler_params=_params(("arbitrary", "arbitrary")),
        name="gla",
    )(gq, gk, gv, gr, gl, tri3, out_norm_g.reshape(1, vw))


def _fox_kernel(tab_q, tab_k, count, q_ref, k_ref, v_ref, o_ref, *scratch, head_dim, nq):
    pair = pl.program_id(0) * pl.num_programs(1) + pl.program_id(1)
    for hh in range(2):
        lanes = pl.ds(hh * LANES, LANES)
        _fox_head(tab_q, tab_k, count, 2 * pair + hh, hh, q_ref.at[:, lanes], k_ref.at[:, lanes],
                  v_ref.at[:, lanes], o_ref, *scratch, head_dim=head_dim, nq=nq)


def _fox_head(tab_q, tab_k, count, head_idx, slot, q_ref, k_ref, v_ref, o_ref, m_all, acc_all,
              s0, s1, p0, p1, al0, al1, *, head_dim, nq):
    T = s0.shape[0]
    s_buf, p_buf, al_buf = (s0, s1), (p0, p1), (al0, al1)
    depth = 3

    def rows(blk):
        start = blk * T
        return pl.ds(start if isinstance(blk, int) else pl.multiple_of(start, T), T)

    m_all[rows(nq), :] = jnp.zeros((T, LANES), F32)
    acc_all[rows(nq), :] = jnp.zeros((T, LANES), F32)
    for ref in (s1, p0, al0):
        ref[...] = jnp.zeros_like(ref)

    half = T // 2

    def half_rows(blk, i):
        return pl.ds(pl.multiple_of(blk * T, T) + i * half, half)

    def scores(qi, kj, par, diag):
        qr = jnp.minimum(qi, nq - 1)
        if not diag:
            s_buf[par][...] = _dot_nt(q_ref[rows(qr), :], k_ref[rows(kj), :])
            return
        top = _dot_nt(q_ref[half_rows(qr, 0), :], k_ref[half_rows(kj, 0), :])
        bot = _dot_nt(q_ref[half_rows(qr, 1), :], k_ref[rows(kj), :])
        def causal(shape, offset):
            row = lax.broadcasted_iota(jnp.int32, shape, 0)
            col = lax.broadcasted_iota(jnp.int32, shape, 1)
            return col <= row + offset

        s_buf[par][:half, :half] = jnp.where(causal((half, half), 0), top, -jnp.inf)
        s_buf[par][:half, half:] = jnp.full((half, half), -jnp.inf, F32)
        s_buf[par][half:, :] = jnp.where(causal((half, T), half), bot, -jnp.inf)

    def softmax(qi, par, diag):
        m_cur = jnp.broadcast_to(jnp.max(s_buf[par][...], axis=-1, keepdims=True), (T, LANES))
        if diag:
            m_new = m_cur
        else:
            m_prev = m_all[rows(qi), :]
            m_new = jnp.maximum(m_prev, m_cur)
            al_buf[par][...] = jnp.exp2(m_prev - m_new)
        m_all[rows(qi), :] = m_new
        for g in range(T // LANES):
            cols = slice(g * LANES, (g + 1) * LANES)
            p_buf[par][:, cols] = jnp.exp2(s_buf[par][:, cols] - m_new).astype(BF16)

    def pv(qi, kj, par, diag):
        start = pl.multiple_of(kj * T, T)

        def p_dot_v(r0, r1, n_cols):
            parts = [_dot(p_buf[par][r0:r1, c:c + MXU_K], v_ref[pl.ds(start + c, MXU_K), :])
                     for c in range(0, n_cols, MXU_K)]
            return sum(parts[1:], parts[0])

        if diag:
            acc_all[half_rows(qi, 0), :] = p_dot_v(0, half, half)
            acc_all[half_rows(qi, 1), :] = p_dot_v(half, T, T)
        else:
            acc_all[rows(qi), :] = al_buf[par][...] * acc_all[rows(qi), :] + p_dot_v(0, T, T)

    def step(c, blocks, diag):
        (q0, k0), (q1, _), (q2, k2) = blocks
        pv(q2, k2, c % 2, diag)
        scores(q0, k0, c % 2, diag)
        softmax(q1, (c - 1) % 2, diag)

    def run(n_blocks, block_at, diag, steps_per_trip):
        spare = (jnp.int32(nq), jnp.int32(0))

        def trip(u, carry):
            hist = [carry[0:2], carry[2:4]]
            for c in range(steps_per_trip):
                nxt = block_at(u * steps_per_trip + c)
                step(c, [nxt] + hist, diag)
                hist = [nxt] + hist[:1]
            return tuple(x for h in hist for x in h)

        n_trips = (n_blocks + (depth - 2 + steps_per_trip)) // steps_per_trip
        lax.fori_loop(0, n_trips, trip, spare * 2)

    def diag_block(n):
        return jnp.minimum(n, nq), jnp.minimum(n, nq - 1)

    def lower_block(n):
        return tab_q[head_idx, n], tab_k[head_idx, n]

    run(nq, diag_block, True, FOX_DIAG_UNROLL)
    if nq > 1:
        run(count[head_idx], lower_block, False, FOX_UNROLL)

    per_trip = math.gcd(nq, FOX_FINISH_BLOCKS)

    def finish(u, carry):
        for i in range(per_trip):
            r = rows(u * per_trip + i)
            acc = acc_all[r, :]
            lane = lax.broadcasted_iota(jnp.int32, acc.shape, 1)
            if slot == 0:
                inv_l = 1.0 / acc[:, head_dim:head_dim + 1]
                o_ref[r, :] = jnp.where(lane < head_dim, acc * inv_l, 0.0).astype(BF16)
            else:
                inv_l = 1.0 / acc[:, 0:1]
                o_ref[r, :] = jnp.where(lane < head_dim, o_ref[r, :].astype(F32),
                                        acc * inv_l).astype(BF16)
        return carry

    lax.fori_loop(0, nq // per_trip, finish, 0)


def _fox_block_table(c_edges, qk_bound, n_heads, nq):
    B = c_edges.shape[0]
    c_first = c_edges[:, :, 0, :n_heads]
    c_last = c_edges[:, :, 1, :n_heads]
    gap = c_first[:, :, None, :] - c_last[:, None, :, :]
    qi = np.arange(nq)[:, None]
    kj = np.arange(nq)[None, :]
    lower = jnp.asarray(kj < qi)[None, :, :, None]
    keep = lower & (2.0 * qk_bound + gap + FOX_BOUND_SLACK > -FOX_ZERO_LOG2)
    keep = jnp.transpose(keep, (0, 3, 1, 2)).reshape(B * n_heads, nq * nq)
    count = jnp.sum(keep, axis=1).astype(jnp.int32)
    order = jnp.argsort(~keep, axis=1, stable=True).astype(jnp.int32)
    n_tab = -(-(nq * (nq - 1) // 2 + 2) // FOX_UNROLL) * FOX_UNROLL
    order = jnp.pad(order, ((0, 0), (0, max(0, n_tab - nq * nq))))[:, :n_tab]
    live = jnp.arange(n_tab)[None, :] < count[:, None]
    tab_q = jnp.where(live, order // nq, nq).astype(jnp.int32)
    tab_k = jnp.where(live, order % nq, 0).astype(jnp.int32)
    return tab_q, tab_k, count


def _fox(qa, ka, va, c_edges, qk_bound, head_dim):
    B, S, W = qa.shape
    H = W // LANES
    T = min(FOX_TILE, S)
    assert S % T == 0 and c_edges.shape[1] * T == S
    nq = S // T
    tab_q, tab_k, count = _fox_block_table(c_edges, qk_bound, H, nq)
    assert H % 2 == 0 and 2 * head_dim == LANES
    pair_in = pl.BlockSpec((None, S, 2 * LANES), lambda b, hp, *_: (b, 0, hp))
    pair_out = pl.BlockSpec((None, S, LANES), lambda b, hp, *_: (b, 0, hp))
    state = pltpu.VMEM(((nq + 1) * T, LANES), F32)
    return pl.pallas_call(
        functools.partial(_fox_kernel, head_dim=head_dim, nq=nq),
        out_shape=jax.ShapeDtypeStruct((B, S, H * head_dim), BF16),
        grid_spec=pltpu.PrefetchScalarGridSpec(
            num_scalar_prefetch=3,
            grid=(B, H // 2),
            in_specs=[pair_in, pair_in, pair_in],
            out_specs=pair_out,
            scratch_shapes=([state, state] + [pltpu.VMEM((T, T), F32)] * 2
                            + [pltpu.VMEM((T, T), BF16)] * 2
                            + [pltpu.VMEM((T, LANES), F32)] * 2)),
        compiler_params=_params(("arbitrary", "arbitrary")),
        name="fox",
    )(tab_q, tab_k, count, qa, ka, va)


def _mem_kv_kernel(mem_ref, g_ref, w_ref, kg_ref, k_ref, v_ref, *, xd):
    x = mem_ref[...]
    D = x.shape[1]
    xn = (x * lax.rsqrt(jnp.mean(x * x, axis=-1, keepdims=True) + EPS) * g_ref[...]).astype(BF16)
    kg = kg_ref[...]
    for h in range(D // xd):
        kh = _dot(xn, w_ref[:, h * xd:(h + 1) * xd])
        kn = kh * lax.rsqrt(jnp.mean(kh * kh, axis=-1, keepdims=True) + EPS) * kg
        k_ref[:, h * xd:(h + 1) * xd] = kn.astype(BF16)
    v_ref[...] = _dot(xn, w_ref[:, D:]).astype(BF16)


def _mem_kv(mem, norm_mem_g, wkv, k_norm_g):
    B, M, D = mem.shape
    xd = k_norm_g.shape[0]
    kg = (k_norm_g * (LOG2E / math.sqrt(xd))).reshape(1, xd)
    blk = pl.BlockSpec((None, M, D), lambda b: (b, 0, 0))
    return pl.pallas_call(
        functools.partial(_mem_kv_kernel, xd=xd),
        out_shape=[jax.ShapeDtypeStruct((B, M, D), BF16)] * 2,
        grid=(B,),
        in_specs=[blk, _const_spec((1, D)), _const_spec((D, 2 * D)), _const_spec((1, xd))],
        out_specs=[blk, blk],
        compiler_params=_params(("arbitrary",)),
        name="mem_kv",
    )(mem, norm_mem_g.reshape(1, D), wkv.astype(BF16), kg)


def _mix_kernel(x_ref, fox_ref, gla_ref, wof_ref, wog_ref, g_ref, wq_ref, qg_ref, km_ref, vm_ref,
                wo_ref, h_ref, *, xd):
    tm, D = x_ref.shape
    splits = [slice(i * tm // MIX_SPLITS, (i + 1) * tm // MIX_SPLITS) for i in range(MIX_SPLITS)]
    heads = [slice(hd * xd, (hd + 1) * xd) for hd in range(D // xd)]
    qg = qg_ref[...]

    def rms(v, gain):
        return v * lax.rsqrt(jnp.mean(v * v, axis=-1, keepdims=True) + EPS) * gain

    hs = [x_ref[r, :] + _dot(fox_ref[r, :], wof_ref[...]) + _dot(gla_ref[r, :], wog_ref[...])
          for r in splits]
    hns = [rms(h, g_ref[...]).astype(BF16) for h in hs]
    qs = [_dot(hn, wq_ref[...]) for hn in hns]
    qns = [[rms(q[:, sl], qg).astype(BF16) for sl in heads] for q in qs]
    ss = [[_dot_nt(qn, km_ref[:, sl]) for qn, sl in zip(row, heads)] for row in qns]
    ps = [[jnp.exp2(s - jnp.max(s, axis=-1, keepdims=True)) for s in row] for row in ss]
    os = [[(_dot(p.astype(BF16), vm_ref[:, sl]) * (1.0 / jnp.sum(p, axis=-1, keepdims=True))
            ).astype(BF16) for p, sl in zip(row, heads)] for row in ps]
    for r, h, o in zip(splits, hs, os):
        h_ref[r, :] = h + _dot(jnp.concatenate(o, axis=1), wo_ref[...])


def _mix(x, fox, gla, w_out, norm_g, wq, q_norm_g, km, vm, wo):
    B, S, D = x.shape
    fw = fox.shape[2]
    gw = gla.shape[2]
    M = km.shape[1]
    xd = q_norm_g.shape[0]
    tm = min(ROW_TILE, S)
    w_out_bf = w_out.astype(BF16)
    wof, wog = w_out_bf[:fw], w_out_bf[fw:]
    row = lambda c: pl.BlockSpec((None, tm, c), lambda b, i: (b, i, 0))
    mem = pl.BlockSpec((None, M, D), lambda b, i: (b, 0, 0))
    return pl.pallas_call(
        functools.partial(_mix_kernel, xd=xd),
        out_shape=jax.ShapeDtypeStruct((B, S, D), F32),
        grid=(B, S // tm),
        in_specs=[row(D), row(fw), row(gw), _const_spec((fw, D)), _const_spec((gw, D)),
                  _const_spec((1, D)), _const_spec((D, D)), _const_spec((1, xd)), mem, mem,
                  _const_spec((D, D))],
        out_specs=row(D),
        compiler_params=_params(("arbitrary", "arbitrary")),
        name="mix",
    )(x, fox, gla, wof, wog, norm_g.reshape(1, D), wq.astype(BF16), q_norm_g.reshape(1, xd),
      km, vm, wo.astype(BF16))


def _mlp_kernel(h_ref, g_ref, w1_ref, w2_ref, y_ref, *, slab):
    h = h_ref[...]
    hn = (h * lax.rsqrt(jnp.mean(h * h, axis=-1, keepdims=True) + EPS) * g_ref[...]).astype(BF16)
    n_slabs = w1_ref.shape[1] // slab
    y = h
    act = None
    for j in range(n_slabs + 1):
        nxt = None
        if j < n_slabs:
            u = jnp.maximum(_dot(hn, w1_ref[:, j * slab:(j + 1) * slab]), 0.0)
            nxt = (u * u).astype(BF16)
        if act is not None:
            y = y + _dot(act, w2_ref[(j - 1) * slab:j * slab, :])
        act = nxt
    y_ref[...] = y


def _mlp(h, norm_g, w1, w2):
    B, S, D = h.shape
    F = w1.shape[1]
    tm = min(ROW_TILE, S)
    row = pl.BlockSpec((None, tm, D), lambda b, i: (b, i, 0))
    return pl.pallas_call(
        functools.partial(_mlp_kernel, slab=min(F, 1024)),
        out_shape=jax.ShapeDtypeStruct((B, S, D), F32),
        grid=(B, S // tm),
        in_specs=[row, _const_spec((1, D)), _const_spec((D, F)), _const_spec((F, D))],
        out_specs=row,
        compiler_params=_params(("arbitrary", "arbitrary")),
        name="mlp",
    )(h, norm_g.reshape(1, D), w1.astype(BF16), w2.astype(BF16))


def kernel(x, mem, norm_mix_g, w_in, fox_b_f, fox_q_norm_g, fox_k_norm_g, gla_w_gate2, gla_b_gate,
           gla_out_norm_g, w_out, norm_xattn_g, norm_mem_g, xattn_wq, xattn_wkv, xattn_q_norm_g,
           xattn_k_norm_g, xattn_wo, norm_mlp_g, mlp_w1, mlp_w2):
    head_dim = fox_q_norm_g.shape[0]
    qa, ka, va, gq, gk, gv, gr, gl, c_edges = _in_proj(
        x, norm_mix_g, w_in, fox_b_f, fox_q_norm_g, fox_k_norm_g, gla_w_gate2, gla_b_gate,
        gla_out_norm_g.shape[0])
    gla = _gla(gq, gk, gv, gr, gl, gla_out_norm_g)
    qk_bound = (FOX_ROUNDING_SLACK * head_dim * LOG2E / math.sqrt(head_dim)
                * jnp.max(jnp.abs(fox_q_norm_g)) * jnp.max(jnp.abs(fox_k_norm_g)))
    fox = _fox(qa, ka, va, c_edges, qk_bound, head_dim)
    km, vm = _mem_kv(mem, norm_mem_g, xattn_wkv, xattn_k_norm_g)
    h = _mix(x, fox, gla, w_out, norm_xattn_g, xattn_wq, xattn_q_norm_g, km, vm, xattn_wo)
    return _mlp(h, norm_mlp_g, mlp_w1, mlp_w2)
```

```python
import functools
import math

import numpy as np
import jax
import jax.numpy as jnp
from jax import lax
from jax.experimental import pallas as pl
from jax.experimental.pallas import tpu as pltpu

F32 = jnp.float32
BF16 = jnp.bfloat16

EPS = 1e-6
LOG2E = 1.4426950408889634

GLA_CHUNK = 64
GLA_HEADS = 4
GLA_TAU = 16.0

LANES = 128
MXU_K = 256
VMEM_LIMIT_BYTES = 56 * 1024 * 1024

ROW_TILE = 512
MIX_SPLITS = 2
FOX_TILE = 512
FOX_UNROLL = 16
FOX_DIAG_UNROLL = 6
FOX_FINISH_BLOCKS = 4
FOX_ZERO_LOG2 = 160.0
FOX_BOUND_SLACK = 2.0
FOX_ROUNDING_SLACK = 1.02
AUG0 = 64


def _const_spec(shape):
    return pl.BlockSpec(shape, lambda *_: (0,) * len(shape))


def _params(semantics, flags=None):
    return pltpu.CompilerParams(dimension_semantics=semantics,
                                vmem_limit_bytes=VMEM_LIMIT_BYTES, flags=flags)


def _split3(c):
    hi = c.astype(BF16)
    r = c - hi.astype(F32)
    mid = r.astype(BF16)
    lo = (r - mid.astype(F32)).astype(BF16)
    return hi, mid, lo


def _log_sigmoid(z):
    return -(jnp.maximum(-z, 0.0) + jnp.log1p(jnp.exp(-jnp.abs(z))))


def _dot(a, b):
    return jnp.dot(a, b, preferred_element_type=F32)


def _dot_nt(a, b):
    return lax.dot_general(a, b, (((1,), (1,)), ((), ())), preferred_element_type=F32)


def _dot_tn(a, b):
    return lax.dot_general(a, b, (((0,), (0,)), ((), ())), preferred_element_type=F32)


def _in_proj_kernel(x_ref, g_ref, w_ref, ones_ref, tri_ref, place_ref, qg_ref, kg_ref,
                    bf_ref, wg2_ref, bg_ref, rows_ref,
                    qa_ref, ka_ref, va_ref, gq_ref, gk_ref, gv_ref, gr_ref, gl_ref, ce_ref,
                    carry_ref, *, fox_w, gkw, gvw, n_fox, head_dim):
    @pl.when(pl.program_id(1) == 0)
    def _():
        carry_ref[...] = jnp.zeros_like(carry_ref)

    x = x_ref[...]
    xn = (x * lax.rsqrt(jnp.mean(x * x, axis=-1, keepdims=True) + EPS) * g_ref[...]).astype(BF16)

    o_k = fox_w
    o_v = 2 * fox_w
    o_gq = 3 * fox_w
    o_gk = o_gq + gkw
    o_gv = o_gk + gkw
    o_gr = o_gv + gvw
    o_sm = o_gr + gvw

    def proj(lo, width):
        return _dot(xn, w_ref[:, lo:lo + width])

    def head_norm(f, gain_row):
        f2 = (f * f).astype(BF16)
        ssq = jnp.concatenate([_dot(f2[:, c:c + MXU_K], ones_ref[...])
                               for c in range(0, fox_w, MXU_K)], axis=1)
        return f * lax.rsqrt(ssq * (1.0 / head_dim) + EPS) * gain_row

    def spread(f):
        return jnp.concatenate([f[:, (h // 2) * LANES:(h // 2 + 1) * LANES] for h in range(n_fox)],
                               axis=1)

    gr_sm = proj(o_gr, gvw + LANES)
    sm = gr_sm[:, gvw:]
    lane = lax.broadcasted_iota(jnp.int32, sm.shape, 1)
    lf = jnp.where(lane < n_fox, _log_sigmoid(sm + bf_ref[...]), 0.0)
    fq = proj(0, fox_w)
    fk = proj(o_k, fox_w)
    fq = head_norm(fq, qg_ref[...])
    fv = proj(o_v, fox_w)
    fk = head_norm(fk, kg_ref[...])

    def pack3(v):
        hi, mid, lo = _split3(v)
        return (hi.astype(F32) + pltpu.roll(mid.astype(F32), n_fox, axis=1)
                + pltpu.roll(lo.astype(F32), 2 * n_fox, axis=1))

    cs = _dot(tri_ref[...], pack3(lf).astype(BF16))
    cs = cs + pltpu.roll(cs, LANES - n_fox, axis=1) + pltpu.roll(cs, LANES - 2 * n_fox, axis=1)
    c = jnp.where(lane < n_fox, cs, 0.0) + carry_ref[...]
    tm = c.shape[0]
    carry_ref[...] = c[tm - 1:tm, :]
    c2 = c * LOG2E
    cpk = pack3(c2).astype(BF16)
    ce_ref[...] = jnp.concatenate([c2[0:1], c2[tm - 1:tm], jnp.zeros((6, LANES), F32)], axis=0)

    g_qk = proj(o_gq, 2 * gkw)
    gq_ref[...] = g_qk[:, :gkw].astype(BF16)
    gk_ref[...] = g_qk[:, gkw:].astype(BF16)
    gv_ref[...] = proj(o_gv, gvw).astype(BF16)
    gr_ref[...] = gr_sm[:, :gvw].astype(BF16)
    gate = _dot(sm.astype(BF16), wg2_ref[...]) + bg_ref[...]
    gl_ref[...] = _log_sigmoid(gate) * (1.0 / GLA_TAU)

    wide = lax.broadcasted_iota(jnp.int32, (1, n_fox * LANES), 1)
    rel = (wide & (LANES - 1)) ^ jnp.where((wide // LANES) % 2 == 1, 0, AUG0)
    is_head_dim = rel >= AUG0
    placed = _dot(cpk, place_ref[...])
    va_ref[...] = jnp.where(is_head_dim, spread(fv), rows_ref[0:1, :]).astype(BF16)
    qa_ref[...] = jnp.where(is_head_dim, spread(fq),
                            jnp.where(rel < 3, placed, rows_ref[1:2, :])).astype(BF16)
    ka_ref[...] = jnp.where(is_head_dim, spread(fk),
                            jnp.where(rel >= 3, placed, rows_ref[2:3, :])).astype(BF16)


def _in_proj(x, norm_g, w_in, fox_b_f, fox_q_norm_g, fox_k_norm_g, gla_w_gate2, gla_b_gate, gvw):
    B, S, D = x.shape
    n_fox = fox_b_f.shape[0]
    head_dim = fox_q_norm_g.shape[0]
    fox_w = n_fox * head_dim
    rank, gkw = gla_w_gate2.shape
    tm = min(ROW_TILE, S)
    assert S % tm == 0 and head_dim == LANES // 2 and n_fox % 2 == 0
    assert 3 * n_fox <= LANES and n_fox + rank <= LANES

    sizes = (fox_w, fox_w, fox_w, n_fox, gkw, gkw, gvw, rank, gvw)
    offs = np.concatenate([[0], np.cumsum(sizes)])
    w_bf = w_in.astype(BF16)
    wq, wk, wv, wf, wgq, wgk, wgv, wlr, wgr = [w_bf[:, offs[i]:offs[i + 1]] for i in range(9)]
    w_all = jnp.concatenate([wq, wk, wv, wgq, wgk, wgv, wgr, wf, wlr,
                             jnp.zeros((D, LANES - n_fox - rank), BF16)], axis=1)
    width = w_all.shape[1]

    scale = (1.0 / math.sqrt(head_dim)) * LOG2E
    qg = (jnp.tile(fox_q_norm_g, n_fox) * scale).reshape(1, fox_w)
    kg = jnp.tile(fox_k_norm_g, n_fox).reshape(1, fox_w)
    bf = jnp.zeros((1, LANES), F32).at[0, :n_fox].set(fox_b_f)
    wg2 = jnp.zeros((LANES, gkw), F32).at[n_fox:n_fox + rank].set(gla_w_gate2).astype(BF16)
    bg = gla_b_gate.reshape(1, gkw)

    assert fox_w % MXU_K == 0 and MXU_K % head_dim == 0
    grp = np.arange(MXU_K) // head_dim
    ones_blk = jnp.asarray(grp[:, None] == grp[None, :], BF16)
    r = np.arange(tm)
    tri = jnp.asarray(r[None, :] <= r[:, None], BF16)
    place = np.zeros((LANES, n_fox * LANES), np.float32)
    const_rows = np.zeros((8, n_fox * LANES), np.float32)
    for h in range(n_fox):
        base = h * LANES + (AUG0 if h % 2 == 0 else 0)
        for part in range(3):
            place[part * n_fox + h, base + part] = 1.0
            place[part * n_fox + h, base + 3 + part] = -1.0
            const_rows[1, base + 3 + part] = 1.0
            const_rows[2, base + part] = 1.0
        const_rows[0, base] = 1.0
    place = jnp.asarray(place, BF16)
    const_rows = jnp.asarray(const_rows)

    row = lambda c: pl.BlockSpec((None, tm, c), lambda b, i: (b, i, 0))
    kern = functools.partial(_in_proj_kernel, fox_w=fox_w, gkw=gkw, gvw=gvw, n_fox=n_fox,
                             head_dim=head_dim)
    aug_w = n_fox * LANES
    out_shapes = [jax.ShapeDtypeStruct((B, S, aug_w), BF16)] * 3 + [
        jax.ShapeDtypeStruct((B, S, gkw), BF16), jax.ShapeDtypeStruct((B, S, gkw), BF16),
        jax.ShapeDtypeStruct((B, S, gvw), BF16), jax.ShapeDtypeStruct((B, S, gvw), BF16),
        jax.ShapeDtypeStruct((B, S, gkw), F32),
        jax.ShapeDtypeStruct((B, S // tm, 8, LANES), F32)]
    return pl.pallas_call(
        kern,
        out_shape=out_shapes,
        grid=(B, S // tm),
        in_specs=[row(D), _const_spec((1, D)), _const_spec((D, width)),
                  _const_spec((MXU_K, MXU_K)), _const_spec((tm, tm)),
                  _const_spec((LANES, aug_w)),
                  _const_spec((1, fox_w)), _const_spec((1, fox_w)), _const_spec((1, LANES)),
                  _const_spec((LANES, gkw)), _const_spec((1, gkw)), _const_spec((8, aug_w))],
        out_specs=[row(aug_w), row(aug_w), row(aug_w), row(gkw), row(gkw), row(gvw), row(gvw),
                   row(gkw), pl.BlockSpec((None, None, 8, LANES), lambda b, i: (b, i, 0, 0))],
        scratch_shapes=[pltpu.VMEM((1, LANES), F32)],
        compiler_params=_params(("arbitrary", "arbitrary")),
        name="in_proj",
    )(x, norm_g.reshape(1, D), w_all, ones_blk, tri, place, qg, kg, bf, wg2, bg, const_rows)


def _gla_kernel(gq_ref, gk_ref, gv_ref, gr_ref, gl_ref, tri3_ref, gain_ref, out_ref,
                st_ref, o_ref, *, dk, dv):
    @pl.when(pl.program_id(1) == 0)
    def _():
        st_ref[...] = jnp.zeros_like(st_ref)

    T, kw = gl_ref.shape
    vw = gv_ref.shape[1]
    n_heads = kw // dk
    C = GLA_CHUNK

    n_chunks = T // C
    chunks = [slice(c * C, (c + 1) * C) for c in range(n_chunks)]

    hi, mid, lo = _split3(gl_ref[...])
    tri3 = tri3_ref[...]
    bcs = [_dot(tri3, jnp.concatenate([hi[r], mid[r], lo[r]], axis=0)) for r in chunks]
    b_last = [b[C - 1:C] for b in bcs]
    bc = jnp.concatenate(bcs, axis=0)
    k = gk_ref[...].astype(F32)
    q_dec = gq_ref[...].astype(F32) * (dk ** -0.5) * jnp.exp(bc)
    k_dec = (k * jnp.exp(-bc)).astype(BF16)

    klane = lax.broadcasted_iota(jnp.int32, (1, kw), 1) // dk
    row_h = lax.broadcasted_iota(jnp.int32, (n_heads * C, C), 0)
    col = lax.broadcasted_iota(jnp.int32, (n_heads * C, C), 1)
    tril = col <= (row_h & (C - 1))
    bd = (lax.broadcasted_iota(jnp.int32, (vw, kw), 0) // dv
          == lax.broadcasted_iota(jnp.int32, (vw, kw), 1) // dk)

    d_sts = [_dot_tn(gv_ref[r, :], (k[r] * jnp.exp(bl - b)).astype(BF16))
             for r, b, bl in zip(chunks, bcs, b_last)]
    st = st_ref[...]
    sts = []
    for d_st, bl in zip(d_sts, b_last):
        sts.append(st.astype(BF16))
        st = st * jnp.exp(bl) + jnp.where(bd, d_st, 0.0)
    st_ref[...] = st

    for r, st_c in zip(chunks, sts):
        qd = q_dec[r]
        lhs = jnp.concatenate([jnp.where(klane == h, qd, 0.0) for h in range(n_heads)],
                              axis=0).astype(BF16)
        a = jnp.where(tril, _dot_nt(lhs, k_dec[r]), 0.0).astype(BF16)
        v = gv_ref[r, :]
        o_intra = jnp.concatenate(
            [_dot(a[h * C:(h + 1) * C], v[:, h * dv:(h + 1) * dv]) for h in range(n_heads)],
            axis=1)
        o_ref[r, :] = o_intra + _dot_nt(qd.astype(BF16), st_c)

    o = o_ref[...]
    gr = gr_ref[...].astype(F32)
    gain = gain_ref[...]
    outs = []
    for h in range(n_heads):
        oh = o[:, h * dv:(h + 1) * dv]
        on = oh * lax.rsqrt(jnp.mean(oh * oh, axis=-1, keepdims=True) + EPS)
        outs.append(on * gain[:, h * dv:(h + 1) * dv])
    out_ref[...] = (jnp.concatenate(outs, axis=1) * (gr * jax.nn.sigmoid(gr))).astype(BF16)


def _gla(gq, gk, gv, gr, gl, out_norm_g):
    B, S, kw = gq.shape
    vw = gv.shape[2]
    dk, dv = kw // GLA_HEADS, vw // GLA_HEADS
    T = min(ROW_TILE, S)
    assert S % T == 0 and T % GLA_CHUNK == 0
    r = np.arange(GLA_CHUNK)
    tri3 = jnp.asarray(np.tile(r[None, :] <= r[:, None], (1, 3)), BF16)
    row = lambda c: pl.BlockSpec((None, T, c), lambda b, i: (b, i, 0))
    return pl.pallas_call(
        functools.partial(_gla_kernel, dk=dk, dv=dv),
        out_shape=jax.ShapeDtypeStruct((B, S, vw), BF16),
        grid=(B, S // T),
        in_specs=[row(kw), row(kw), row(vw), row(vw), row(kw),
                  _const_spec((GLA_CHUNK, 3 * GLA_CHUNK)), _const_spec((1, vw))],
        out_specs=row(vw),
        scratch_shapes=[pltpu.VMEM((vw, kw), F32), pltpu.VMEM((T, vw), F32)],
        compiler_params=_params(("arbitrary", "arbitrary")),
        name="gla",
    )(gq, gk, gv, gr, gl, tri3, out_norm_g.reshape(1, vw))


def _fox_kernel(tab_q, tab_k, count, q_ref, k_ref, v_ref, o_ref, *scratch, head_dim, nq):
    pair = pl.program_id(0) * pl.num_programs(1) + pl.program_id(1)
    for hh in range(2):
        lanes = pl.ds(hh * LANES, LANES)
        _fox_head(tab_q, tab_k, count, 2 * pair + hh, hh, q_ref.at[:, lanes], k_ref.at[:, lanes],
                  v_ref.at[:, lanes], o_ref, *scratch, head_dim=head_dim, nq=nq)


def _fox_head(tab_q, tab_k, count, head_idx, slot, q_ref, k_ref, v_ref, o_ref, m_all, acc_all,
              s0, s1, p0, p1, al0, al1, *, head_dim, nq):
    T = s0.shape[0]
    s_buf, p_buf, al_buf = (s0, s1), (p0, p1), (al0, al1)
    depth = 3

    def rows(blk):
        start = blk * T
        return pl.ds(start if isinstance(blk, int) else pl.multiple_of(start, T), T)

    m_all[rows(nq), :] = jnp.zeros((T, LANES), F32)
    acc_all[rows(nq), :] = jnp.zeros((T, LANES), F32)
    for ref in (s1, p0, al0):
        ref[...] = jnp.zeros_like(ref)

    half = T // 2

    def half_rows(blk, i):
        return pl.ds(pl.multiple_of(blk * T, T) + i * half, half)

    def scores(qi, kj, par, diag):
        qr = jnp.minimum(qi, nq - 1)
        if not diag:
            s_buf[par][...] = _dot_nt(q_ref[rows(qr), :], k_ref[rows(kj), :])
            return
        top = _dot_nt(q_ref[half_rows(qr, 0), :], k_ref[half_rows(kj, 0), :])
        bot = _dot_nt(q_ref[half_rows(qr, 1), :], k_ref[rows(kj), :])
        def causal(shape, offset):
            row = lax.broadcasted_iota(jnp.int32, shape, 0)
            col = lax.broadcasted_iota(jnp.int32, shape, 1)
            return col <= row + offset

        s_buf[par][:half, :half] = jnp.where(causal((half, half), 0), top, -jnp.inf)
        s_buf[par][:half, half:] = jnp.full((half, half), -jnp.inf, F32)
        s_buf[par][half:, :] = jnp.where(causal((half, T), half), bot, -jnp.inf)

    def softmax(qi, par, diag):
        m_cur = jnp.broadcast_to(jnp.max(s_buf[par][...], axis=-1, keepdims=True), (T, LANES))
        if diag:
            m_new = m_cur
        else:
            m_prev = m_all[rows(qi), :]
            m_new = jnp.maximum(m_prev, m_cur)
            al_buf[par][...] = jnp.exp2(m_prev - m_new)
        m_all[rows(qi), :] = m_new
        for g in range(T // LANES):
            cols = slice(g * LANES, (g + 1) * LANES)
            p_buf[par][:, cols] = jnp.exp2(s_buf[par][:, cols] - m_new).astype(BF16)

    def pv(qi, kj, par, diag):
        start = pl.multiple_of(kj * T, T)

        def p_dot_v(r0, r1, n_cols):
            parts = [_dot(p_buf[par][r0:r1, c:c + MXU_K], v_ref[pl.ds(start + c, MXU_K), :])
                     for c in range(0, n_cols, MXU_K)]
            return sum(parts[1:], parts[0])

        if diag:
            acc_all[half_rows(qi, 0), :] = p_dot_v(0, half, half)
            acc_all[half_rows(qi, 1), :] = p_dot_v(half, T, T)
        else:
            acc_all[rows(qi), :] = al_buf[par][...] * acc_all[rows(qi), :] + p_dot_v(0, T, T)

    def step(c, blocks, diag):
        (q0, k0), (q1, _), (q2, k2) = blocks
        pv(q2, k2, c % 2, diag)
        scores(q0, k0, c % 2, diag)
        softmax(q1, (c - 1) % 2, diag)

    def run(n_blocks, block_at, diag, steps_per_trip):
        spare = (jnp.int32(nq), jnp.int32(0))

        def trip(u, carry):
            hist = [carry[0:2], carry[2:4]]
            for c in range(steps_per_trip):
                nxt = block_at(u * steps_per_trip + c)
                step(c, [nxt] + hist, diag)
                hist = [nxt] + hist[:1]
            return tuple(x for h in hist for x in h)

        n_trips = (n_blocks + (depth - 2 + steps_per_trip)) // steps_per_trip
        lax.fori_loop(0, n_trips, trip, spare * 2)

    def diag_block(n):
        return jnp.minimum(n, nq), jnp.minimum(n, nq - 1)

    def lower_block(n):
        return tab_q[head_idx, n], tab_k[head_idx, n]

    run(nq, diag_block, True, FOX_DIAG_UNROLL)
    if nq > 1:
        run(count[head_idx], lower_block, False, FOX_UNROLL)

    per_trip = math.gcd(nq, FOX_FINISH_BLOCKS)

    def finish(u, carry):
        for i in range(per_trip):
            r = rows(u * per_trip + i)
            acc = acc_all[r, :]
            lane = lax.broadcasted_iota(jnp.int32, acc.shape, 1)
            if slot == 0:
                inv_l = 1.0 / acc[:, head_dim:head_dim + 1]
                o_ref[r, :] = jnp.where(lane < head_dim, acc * inv_l, 0.0).astype(BF16)
            else:
                inv_l = 1.0 / acc[:, 0:1]
                o_ref[r, :] = jnp.where(lane < head_dim, o_ref[r, :].astype(F32),
                                        acc * inv_l).astype(BF16)
        return carry

    lax.fori_loop(0, nq // per_trip, finish, 0)


def _fox_block_table(c_edges, qk_bound, n_heads, nq):
    B = c_edges.shape[0]
    c_first = c_edges[:, :, 0, :n_heads]
    c_last = c_edges[:, :, 1, :n_heads]
    gap = c_first[:, :, None, :] - c_last[:, None, :, :]
    qi = np.arange(nq)[:, None]
    kj = np.arange(nq)[None, :]
    lower = jnp.asarray(kj < qi)[None, :, :, None]
    keep = lower & (2.0 * qk_bound + gap + FOX_BOUND_SLACK > -FOX_ZERO_LOG2)
    keep = jnp.transpose(keep, (0, 3, 1, 2)).reshape(B * n_heads, nq * nq)
    count = jnp.sum(keep, axis=1).astype(jnp.int32)
    order = jnp.argsort(~keep, axis=1, stable=True).astype(jnp.int32)
    n_tab = -(-(nq * (nq - 1) // 2 + 2) // FOX_UNROLL) * FOX_UNROLL
    order = jnp.pad(order, ((0, 0), (0, max(0, n_tab - nq * nq))))[:, :n_tab]
    live = jnp.arange(n_tab)[None, :] < count[:, None]
    tab_q = jnp.where(live, order // nq, nq).astype(jnp.int32)
    tab_k = jnp.where(live, order % nq, 0).astype(jnp.int32)
    return tab_q, tab_k, count


def _fox(qa, ka, va, c_edges, qk_bound, head_dim):
    B, S, W = qa.shape
    H = W // LANES
    T = min(FOX_TILE, S)
    assert S % T == 0 and c_edges.shape[1] * T == S
    nq = S // T
    tab_q, tab_k, count = _fox_block_table(c_edges, qk_bound, H, nq)
    assert H % 2 == 0 and 2 * head_dim == LANES
    pair_in = pl.BlockSpec((None, S, 2 * LANES), lambda b, hp, *_: (b, 0, hp))
    pair_out = pl.BlockSpec((None, S, LANES), lambda b, hp, *_: (b, 0, hp))
    state = pltpu.VMEM(((nq + 1) * T, LANES), F32)
    return pl.pallas_call(
        functools.partial(_fox_kernel, head_dim=head_dim, nq=nq),
        out_shape=jax.ShapeDtypeStruct((B, S, H * head_dim), BF16),
        grid_spec=pltpu.PrefetchScalarGridSpec(
            num_scalar_prefetch=3,
            grid=(B, H // 2),
            in_specs=[pair_in, pair_in, pair_in],
            out_specs=pair_out,
            scratch_shapes=([state, state] + [pltpu.VMEM((T, T), F32)] * 2
                            + [pltpu.VMEM((T, T), BF16)] * 2
                            + [pltpu.VMEM((T, LANES), F32)] * 2)),
        compiler_params=_params(("arbitrary", "arbitrary")),
        name="fox",
    )(tab_q, tab_k, count, qa, ka, va)


def _mem_kv_kernel(mem_ref, g_ref, w_ref, kg_ref, k_ref, v_ref, *, xd):
    x = mem_ref[...]
    D = x.shape[1]
    xn = (x * lax.rsqrt(jnp.mean(x * x, axis=-1, keepdims=True) + EPS) * g_ref[...]).astype(BF16)
    kg = kg_ref[...]
    for h in range(D // xd):
        kh = _dot(xn, w_ref[:, h * xd:(h + 1) * xd])
        kn = kh * lax.rsqrt(jnp.mean(kh * kh, axis=-1, keepdims=True) + EPS) * kg
        k_ref[:, h * xd:(h + 1) * xd] = kn.astype(BF16)
    v_ref[...] = _dot(xn, w_ref[:, D:]).astype(BF16)


def _mem_kv(mem, norm_mem_g, wkv, k_norm_g):
    B, M, D = mem.shape
    xd = k_norm_g.shape[0]
    kg = (k_norm_g * (LOG2E / math.sqrt(xd))).reshape(1, xd)
    blk = pl.BlockSpec((None, M, D), lambda b: (b, 0, 0))
    return pl.pallas_call(
        functools.partial(_mem_kv_kernel, xd=xd),
        out_shape=[jax.ShapeDtypeStruct((B, M, D), BF16)] * 2,
        grid=(B,),
        in_specs=[blk, _const_spec((1, D)), _const_spec((D, 2 * D)), _const_spec((1, xd))],
        out_specs=[blk, blk],
        compiler_params=_params(("arbitrary",)),
        name="mem_kv",
    )(mem, norm_mem_g.reshape(1, D), wkv.astype(BF16), kg)


def _mix_kernel(x_ref, fox_ref, gla_ref, wof_ref, wog_ref, g_ref, wq_ref, qg_ref, km_ref, vm_ref,
                wo_ref, h_ref, *, xd):
    tm, D = x_ref.shape
    splits = [slice(i * tm // MIX_SPLITS, (i + 1) * tm // MIX_SPLITS) for i in range(MIX_SPLITS)]
    heads = [slice(hd * xd, (hd + 1) * xd) for hd in range(D // xd)]
    qg = qg_ref[...]

    def rms(v, gain):
        return v * lax.rsqrt(jnp.mean(v * v, axis=-1, keepdims=True) + EPS) * gain

    hs = [x_ref[r, :] + _dot(fox_ref[r, :], wof_ref[...]) + _dot(gla_ref[r, :], wog_ref[...])
          for r in splits]
    hns = [rms(h, g_ref[...]).astype(BF16) for h in hs]
    qs = [_dot(hn, wq_ref[...]) for hn in hns]
    qns = [[rms(q[:, sl], qg).astype(BF16) for sl in heads] for q in qs]
    ss = [[_dot_nt(qn, km_ref[:, sl]) for qn, sl in zip(row, heads)] for row in qns]
    ps = [[jnp.exp2(s - jnp.max(s, axis=-1, keepdims=True)) for s in row] for row in ss]
    os = [[(_dot(p.astype(BF16), vm_ref[:, sl]) * (1.0 / jnp.sum(p, axis=-1, keepdims=True))
            ).astype(BF16) for p, sl in zip(row, heads)] for row in ps]
    for r, h, o in zip(splits, hs, os):
        h_ref[r, :] = h + _dot(jnp.concatenate(o, axis=1), wo_ref[...])


def _mix(x, fox, gla, w_out, norm_g, wq, q_norm_g, km, vm, wo):
    B, S, D = x.shape
    fw = fox.shape[2]
    gw = gla.shape[2]
    M = km.shape[1]
    xd = q_norm_g.shape[0]
    tm = min(ROW_TILE, S)
    w_out_bf = w_out.astype(BF16)
    wof, wog = w_out_bf[:fw], w_out_bf[fw:]
    row = lambda c: pl.BlockSpec((None, tm, c), lambda b, i: (b, i, 0))
    mem = pl.BlockSpec((None, M, D), lambda b, i: (b, 0, 0))
    return pl.pallas_call(
        functools.partial(_mix_kernel, xd=xd),
        out_shape=jax.ShapeDtypeStruct((B, S, D), F32),
        grid=(B, S // tm),
        in_specs=[row(D), row(fw), row(gw), _const_spec((fw, D)), _const_spec((gw, D)),
                  _const_spec((1, D)), _const_spec((D, D)), _const_spec((1, xd)), mem, mem,
                  _const_spec((D, D))],
        out_specs=row(D),
        compiler_params=_params(("arbitrary", "arbitrary")),
        name="mix",
    )(x, fox, gla, wof, wog, norm_g.reshape(1, D), wq.astype(BF16), q_norm_g.reshape(1, xd),
      km, vm, wo.astype(BF16))


def _mlp_kernel(h_ref, g_ref, w1_ref, w2_ref, y_ref, *, slab):
    h = h_ref[...]
    hn = (h * lax.rsqrt(jnp.mean(h * h, axis=-1, keepdims=True) + EPS) * g_ref[...]).astype(BF16)
    n_slabs = w1_ref.shape[1] // slab
    y = h
    act = None
    for j in range(n_slabs + 1):
        nxt = None
        if j < n_slabs:
            u = jnp.maximum(_dot(hn, w1_ref[:, j * slab:(j + 1) * slab]), 0.0)
            nxt = (u * u).astype(BF16)
        if act is not None:
            y = y + _dot(act, w2_ref[(j - 1) * slab:j * slab, :])
        act = nxt
    y_ref[...] = y


def _mlp(h, norm_g, w1, w2):
    B, S, D = h.shape
    F = w1.shape[1]
    tm = min(ROW_TILE, S)
    row = pl.BlockSpec((None, tm, D), lambda b, i: (b, i, 0))
    return pl.pallas_call(
        functools.partial(_mlp_kernel, slab=min(F, 1024)),
        out_shape=jax.ShapeDtypeStruct((B, S, D), F32),
        grid=(B, S // tm),
        in_specs=[row, _const_spec((1, D)), _const_spec((D, F)), _const_spec((F, D))],
        out_specs=row,
        compiler_params=_params(("arbitrary", "arbitrary")),
        name="mlp",
    )(h, norm_g.reshape(1, D), w1.astype(BF16), w2.astype(BF16))


def kernel(x, mem, norm_mix_g, w_in, fox_b_f, fox_q_norm_g, fox_k_norm_g, gla_w_gate2, gla_b_gate,
           gla_out_norm_g, w_out, norm_xattn_g, norm_mem_g, xattn_wq, xattn_wkv, xattn_q_norm_g,
           xattn_k_norm_g, xattn_wo, norm_mlp_g, mlp_w1, mlp_w2):
    head_dim = fox_q_norm_g.shape[0]
    qa, ka, va, gq, gk, gv, gr, gl, c_edges = _in_proj(
        x, norm_mix_g, w_in, fox_b_f, fox_q_norm_g, fox_k_norm_g, gla_w_gate2, gla_b_gate,
        gla_out_norm_g.shape[0])
    gla = _gla(gq, gk, gv, gr, gl, gla_out_norm_g)
    qk_bound = (FOX_ROUNDING_SLACK * head_dim * LOG2E / math.sqrt(head_dim)
                * jnp.max(jnp.abs(fox_q_norm_g)) * jnp.max(jnp.abs(fox_k_norm_g)))
    fox = _fox(qa, ka, va, c_edges, qk_bound, head_dim)
    km, vm = _mem_kv(mem, norm_mem_g, xattn_wkv, xattn_k_norm_g)
    h = _mix(x, fox, gla, w_out, norm_xattn_g, xattn_wq, xattn_q_norm_g, km, vm, xattn_wo)
    return _mlp(h, norm_mlp_g, mlp_w1, mlp_w2)
```

```python
import functools
import math

import numpy as np
import jax
import jax.numpy as jnp
from jax import lax
from jax.experimental import pallas as pl
from jax.experimental.pallas import tpu as pltpu

F32 = jnp.float32
BF16 = jnp.bfloat16

EPS = 1e-6
LOG2E = 1.4426950408889634

GLA_CHUNK = 64
GLA_HEADS = 4
GLA_TAU = 16.0

LANES = 128
MXU_K = 256
VMEM_LIMIT_BYTES = 56 * 1024 * 1024

ROW_TILE = 512
MIX_SPLITS = 2
FOX_TILE = 512
FOX_UNROLL = 16
FOX_DIAG_UNROLL = 6
FOX_FINISH_BLOCKS = 4
FOX_ZERO_LOG2 = 160.0
FOX_BOUND_SLACK = 2.0
FOX_ROUNDING_SLACK = 1.02
AUG0 = 64


def _const_spec(shape):
    return pl.BlockSpec(shape, lambda *_: (0,) * len(shape))


def _params(semantics, flags=None):
    return pltpu.CompilerParams(dimension_semantics=semantics,
                                vmem_limit_bytes=VMEM_LIMIT_BYTES, flags=flags)


def _split3(c):
    hi = c.astype(BF16)
    r = c - hi.astype(F32)
    mid = r.astype(BF16)
    lo = (r - mid.astype(F32)).astype(BF16)
    return hi, mid, lo


def _log_sigmoid(z):
    return -(jnp.maximum(-z, 0.0) + jnp.log1p(jnp.exp(-jnp.abs(z))))


def _dot(a, b):
    return jnp.dot(a, b, preferred_element_type=F32)


def _dot_nt(a, b):
    return lax.dot_general(a, b, (((1,), (1,)), ((), ())), preferred_element_type=F32)


def _dot_tn(a, b):
    return lax.dot_general(a, b, (((0,), (0,)), ((), ())), preferred_element_type=F32)


def _in_proj_kernel(x_ref, g_ref, w_ref, ones_ref, tri_ref, place_ref, qg_ref, kg_ref,
                    bf_ref, wg2_ref, bg_ref, rows_ref,
                    qa_ref, ka_ref, va_ref, gq_ref, gk_ref, gv_ref, gr_ref, gl_ref, ce_ref,
                    carry_ref, *, fox_w, gkw, gvw, n_fox, head_dim):
    @pl.when(pl.program_id(1) == 0)
    def _():
        carry_ref[...] = jnp.zeros_like(carry_ref)

    x = x_ref[...]
    xn = (x * lax.rsqrt(jnp.mean(x * x, axis=-1, keepdims=True) + EPS) * g_ref[...]).astype(BF16)

    o_k = fox_w
    o_v = 2 * fox_w
    o_gq = 3 * fox_w
    o_gk = o_gq + gkw
    o_gv = o_gk + gkw
    o_gr = o_gv + gvw
    o_sm = o_gr + gvw

    def proj(lo, width):
        return _dot(xn, w_ref[:, lo:lo + width])

    def head_norm(f, gain_row):
        f2 = (f * f).astype(BF16)
        ssq = jnp.concatenate([_dot(f2[:, c:c + MXU_K], ones_ref[...])
                               for c in range(0, fox_w, MXU_K)], axis=1)
        return f * lax.rsqrt(ssq * (1.0 / head_dim) + EPS) * gain_row

    def spread(f):
        return jnp.concatenate([f[:, (h // 2) * LANES:(h // 2 + 1) * LANES] for h in range(n_fox)],
                               axis=1)

    def pack3(v):
        hi, mid, lo = _split3(v)
        return (hi.astype(F32) + pltpu.roll(mid.astype(F32), n_fox, axis=1)
                + pltpu.roll(lo.astype(F32), 2 * n_fox, axis=1))

    gr_sm = proj(o_gr, gvw + LANES)
    sm = gr_sm[:, gvw:]
    lane = lax.broadcasted_iota(jnp.int32, sm.shape, 1)
    lf = jnp.where(lane < n_fox, _log_sigmoid(sm + bf_ref[...]), 0.0)
    fq = proj(0, fox_w)
    cs = _dot(tri_ref[...], pack3(lf).astype(BF16))
    fk = proj(o_k, fox_w)
    fq = head_norm(fq, qg_ref[...])
    cs = cs + pltpu.roll(cs, LANES - n_fox, axis=1) + pltpu.roll(cs, LANES - 2 * n_fox, axis=1)
    c = jnp.where(lane < n_fox, cs, 0.0) + carry_ref[...]
    tm = c.shape[0]
    carry_ref[...] = c[tm - 1:tm, :]
    c2 = c * LOG2E
    ce_ref[...] = jnp.concatenate([c2[0:1], c2[tm - 1:tm], jnp.zeros((6, LANES), F32)], axis=0)
    fv = proj(o_v, fox_w)
    fk = head_norm(fk, kg_ref[...])

    wide = lax.broadcasted_iota(jnp.int32, (1, n_fox * LANES), 1)
    rel = (wide & (LANES - 1)) ^ jnp.where((wide // LANES) % 2 == 1, 0, AUG0)
    is_head_dim = rel >= AUG0
    placed = _dot(pack3(c2).astype(BF16), place_ref[...])
    va_ref[...] = jnp.where(is_head_dim, spread(fv), rows_ref[0:1, :]).astype(BF16)
    qa_ref[...] = jnp.where(is_head_dim, spread(fq),
                            jnp.where(rel < 3, placed, rows_ref[1:2, :])).astype(BF16)
    ka_ref[...] = jnp.where(is_head_dim, spread(fk),
                            jnp.where(rel >= 3, placed, rows_ref[2:3, :])).astype(BF16)

    g_qk = proj(o_gq, 2 * gkw)
    gq_ref[...] = g_qk[:, :gkw].astype(BF16)
    gk_ref[...] = g_qk[:, gkw:].astype(BF16)
    gv_ref[...] = proj(o_gv, gvw).astype(BF16)
    gr_ref[...] = gr_sm[:, :gvw].astype(BF16)
    gate = _dot(sm.astype(BF16), wg2_ref[...]) + bg_ref[...]
    gl_ref[...] = _log_sigmoid(gate) * (1.0 / GLA_TAU)


def _in_proj(x, norm_g, w_in, fox_b_f, fox_q_norm_g, fox_k_norm_g, gla_w_gate2, gla_b_gate, gvw):
    B, S, D = x.shape
    n_fox = fox_b_f.shape[0]
    head_dim = fox_q_norm_g.shape[0]
    fox_w = n_fox * head_dim
    rank, gkw = gla_w_gate2.shape
    tm = min(ROW_TILE, S)
    assert S % tm == 0 and head_dim == LANES // 2 and n_fox % 2 == 0
    assert 3 * n_fox <= LANES and n_fox + rank <= LANES

    sizes = (fox_w, fox_w, fox_w, n_fox, gkw, gkw, gvw, rank, gvw)
    offs = np.concatenate([[0], np.cumsum(sizes)])
    w_bf = w_in.astype(BF16)
    wq, wk, wv, wf, wgq, wgk, wgv, wlr, wgr = [w_bf[:, offs[i]:offs[i + 1]] for i in range(9)]
    w_all = jnp.concatenate([wq, wk, wv, wgq, wgk, wgv, wgr, wf, wlr,
                             jnp.zeros((D, LANES - n_fox - rank), BF16)], axis=1)
    width = w_all.shape[1]

    scale = (1.0 / math.sqrt(head_dim)) * LOG2E
    qg = (jnp.tile(fox_q_norm_g, n_fox) * scale).reshape(1, fox_w)
    kg = jnp.tile(fox_k_norm_g, n_fox).reshape(1, fox_w)
    bf = jnp.zeros((1, LANES), F32).at[0, :n_fox].set(fox_b_f)
    wg2 = jnp.zeros((LANES, gkw), F32).at[n_fox:n_fox + rank].set(gla_w_gate2).astype(BF16)
    bg = gla_b_gate.reshape(1, gkw)

    assert fox_w % MXU_K == 0 and MXU_K % head_dim == 0
    grp = np.arange(MXU_K) // head_dim
    ones_blk = jnp.asarray(grp[:, None] == grp[None, :], BF16)
    r = np.arange(tm)
    tri = jnp.asarray(r[None, :] <= r[:, None], BF16)
    place = np.zeros((LANES, n_fox * LANES), np.float32)
    const_rows = np.zeros((8, n_fox * LANES), np.float32)
    for h in range(n_fox):
        base = h * LANES + (AUG0 if h % 2 == 0 else 0)
        for part in range(3):
            place[part * n_fox + h, base + part] = 1.0
            place[part * n_fox + h, base + 3 + part] = -1.0
            const_rows[1, base + 3 + part] = 1.0
            const_rows[2, base + part] = 1.0
        const_rows[0, base] = 1.0
    place = jnp.asarray(place, BF16)
    const_rows = jnp.asarray(const_rows)

    row = lambda c: pl.BlockSpec((None, tm, c), lambda b, i: (b, i, 0))
    kern = functools.partial(_in_proj_kernel, fox_w=fox_w, gkw=gkw, gvw=gvw, n_fox=n_fox,
                             head_dim=head_dim)
    aug_w = n_fox * LANES
    out_shapes = [jax.ShapeDtypeStruct((B, S, aug_w), BF16)] * 3 + [
        jax.ShapeDtypeStruct((B, S, gkw), BF16), jax.ShapeDtypeStruct((B, S, gkw), BF16),
        jax.ShapeDtypeStruct((B, S, gvw), BF16), jax.ShapeDtypeStruct((B, S, gvw), BF16),
        jax.ShapeDtypeStruct((B, S, gkw), F32),
        jax.ShapeDtypeStruct((B, S // tm, 8, LANES), F32)]
    return pl.pallas_call(
        kern,
        out_shape=out_shapes,
        grid=(B, S // tm),
        in_specs=[row(D), _const_spec((1, D)), _const_spec((D, width)),
                  _const_spec((MXU_K, MXU_K)), _const_spec((tm, tm)),
                  _const_spec((LANES, aug_w)),
                  _const_spec((1, fox_w)), _const_spec((1, fox_w)), _const_spec((1, LANES)),
                  _const_spec((LANES, gkw)), _const_spec((1, gkw)), _const_spec((8, aug_w))],
        out_specs=[row(aug_w), row(aug_w), row(aug_w), row(gkw), row(gkw), row(gvw), row(gvw),
                   row(gkw), pl.BlockSpec((None, None, 8, LANES), lambda b, i: (b, i, 0, 0))],
        scratch_shapes=[pltpu.VMEM((1, LANES), F32)],
        compiler_params=_params(("arbitrary", "arbitrary")),
        name="in_proj",
    )(x, norm_g.reshape(1, D), w_all, ones_blk, tri, place, qg, kg, bf, wg2, bg, const_rows)


def _gla_kernel(gq_ref, gk_ref, gv_ref, gr_ref, gl_ref, tri3_ref, gain_ref, out_ref,
                st_ref, o_ref, *, dk, dv):
    @pl.when(pl.program_id(1) == 0)
    def _():
        st_ref[...] = jnp.zeros_like(st_ref)

    T, kw = gl_ref.shape
    vw = gv_ref.shape[1]
    n_heads = kw // dk
    C = GLA_CHUNK

    n_chunks = T // C
    chunks = [slice(c * C, (c + 1) * C) for c in range(n_chunks)]

    hi, mid, lo = _split3(gl_ref[...])
    tri3 = tri3_ref[...]
    bcs = [_dot(tri3, jnp.concatenate([hi[r], mid[r], lo[r]], axis=0)) for r in chunks]
    b_last = [b[C - 1:C] for b in bcs]
    bc = jnp.concatenate(bcs, axis=0)
    k = gk_ref[...].astype(F32)
    q_dec = gq_ref[...].astype(F32) * (dk ** -0.5) * jnp.exp(bc)
    k_dec = (k * jnp.exp(-bc)).astype(BF16)

    klane = lax.broadcasted_iota(jnp.int32, (1, kw), 1) // dk
    row_h = lax.broadcasted_iota(jnp.int32, (n_heads * C, C), 0)
    col = lax.broadcasted_iota(jnp.int32, (n_heads * C, C), 1)
    tril = col <= (row_h & (C - 1))
    bd = (lax.broadcasted_iota(jnp.int32, (vw, kw), 0) // dv
          == lax.broadcasted_iota(jnp.int32, (vw, kw), 1) // dk)

    d_sts = [_dot_tn(gv_ref[r, :], (k[r] * jnp.exp(bl - b)).astype(BF16))
             for r, b, bl in zip(chunks, bcs, b_last)]
    st = st_ref[...]
    sts = []
    for d_st, bl in zip(d_sts, b_last):
        sts.append(st.astype(BF16))
        st = st * jnp.exp(bl) + jnp.where(bd, d_st, 0.0)
    st_ref[...] = st

    for r, st_c in zip(chunks, sts):
        qd = q_dec[r]
        lhs = jnp.concatenate([jnp.where(klane == h, qd, 0.0) for h in range(n_heads)],
                              axis=0).astype(BF16)
        a = jnp.where(tril, _dot_nt(lhs, k_dec[r]), 0.0).astype(BF16)
        v = gv_ref[r, :]
        o_intra = jnp.concatenate(
            [_dot(a[h * C:(h + 1) * C], v[:, h * dv:(h + 1) * dv]) for h in range(n_heads)],
            axis=1)
        o_ref[r, :] = o_intra + _dot_nt(qd.astype(BF16), st_c)

    o = o_ref[...]
    gr = gr_ref[...].astype(F32)
    gain = gain_ref[...]
    outs = []
    for h in range(n_heads):
        oh = o[:, h * dv:(h + 1) * dv]
        on = oh * lax.rsqrt(jnp.mean(oh * oh, axis=-1, keepdims=True) + EPS)
        outs.append(on * gain[:, h * dv:(h + 1) * dv])
    out_ref[...] = (jnp.concatenate(outs, axis=1) * (gr * jax.nn.sigmoid(gr))).astype(BF16)


def _gla(gq, gk, gv, gr, gl, out_norm_g):
    B, S, kw = gq.shape
    vw = gv.shape[2]
    dk, dv = kw // GLA_HEADS, vw // GLA_HEADS
    T = min(ROW_TILE, S)
    assert S % T == 0 and T % GLA_CHUNK == 0
    r = np.arange(GLA_CHUNK)
    tri3 = jnp.asarray(np.tile(r[None, :] <= r[:, None], (1, 3)), BF16)
    row = lambda c: pl.BlockSpec((None, T, c), lambda b, i: (b, i, 0))
    return pl.pallas_call(
        functools.partial(_gla_kernel, dk=dk, dv=dv),
        out_shape=jax.ShapeDtypeStruct((B, S, vw), BF16),
        grid=(B, S // T),
        in_specs=[row(kw), row(kw), row(vw), row(vw), row(kw),
                  _const_spec((GLA_CHUNK, 3 * GLA_CHUNK)), _const_spec((1, vw))],
        out_specs=row(vw),
        scratch_shapes=[pltpu.VMEM((vw, kw), F32), pltpu.VMEM((T, vw), F32)],
        compiler_params=_params(("arbitrary", "arbitrary")),
        name="gla",
    )(gq, gk, gv, gr, gl, tri3, out_norm_g.reshape(1, vw))


def _fox_kernel(tab_q, tab_k, count, q_ref, k_ref, v_ref, o_ref, *scratch, head_dim, nq):
    pair = pl.program_id(0) * pl.num_programs(1) + pl.program_id(1)
    for hh in range(2):
        lanes = pl.ds(hh * LANES, LANES)
        _fox_head(tab_q, tab_k, count, 2 * pair + hh, hh, q_ref.at[:, lanes], k_ref.at[:, lanes],
                  v_ref.at[:, lanes], o_ref, *scratch, head_dim=head_dim, nq=nq)


def _fox_head(tab_q, tab_k, count, head_idx, slot, q_ref, k_ref, v_ref, o_ref, m_all, acc_all,
              s0, s1, p0, p1, al0, al1, *, head_dim, nq):
    T = s0.shape[0]
    s_buf, p_buf, al_buf = (s0, s1), (p0, p1), (al0, al1)
    depth = 3

    def rows(blk):
        start = blk * T
        return pl.ds(start if isinstance(blk, int) else pl.multiple_of(start, T), T)

    m_all[rows(nq), :] = jnp.zeros((T, LANES), F32)
    acc_all[rows(nq), :] = jnp.zeros((T, LANES), F32)
    for ref in (s1, p0, al0):
        ref[...] = jnp.zeros_like(ref)

    half = T // 2

    def half_rows(blk, i):
        return pl.ds(pl.multiple_of(blk * T, T) + i * half, half)

    def scores(qi, kj, par, diag):
        qr = jnp.minimum(qi, nq - 1)
        if not diag:
            s_buf[par][...] = _dot_nt(q_ref[rows(qr), :], k_ref[rows(kj), :])
            return
        top = _dot_nt(q_ref[half_rows(qr, 0), :], k_ref[half_rows(kj, 0), :])
        bot = _dot_nt(q_ref[half_rows(qr, 1), :], k_ref[rows(kj), :])
        def causal(shape, offset):
            row = lax.broadcasted_iota(jnp.int32, shape, 0)
            col = lax.broadcasted_iota(jnp.int32, shape, 1)
            return col <= row + offset

        s_buf[par][:half, :half] = jnp.where(causal((half, half), 0), top, -jnp.inf)
        s_buf[par][:half, half:] = jnp.full((half, half), -jnp.inf, F32)
        s_buf[par][half:, :] = jnp.where(causal((half, T), half), bot, -jnp.inf)

    def softmax(qi, par, diag):
        m_cur = jnp.broadcast_to(jnp.max(s_buf[par][...], axis=-1, keepdims=True), (T, LANES))
        if diag:
            m_new = m_cur
        else:
            m_prev = m_all[rows(qi), :]
            m_new = jnp.maximum(m_prev, m_cur)
            al_buf[par][...] = jnp.exp2(m_prev - m_new)
        m_all[rows(qi), :] = m_new
        for g in range(T // LANES):
            cols = slice(g * LANES, (g + 1) * LANES)
            p_buf[par][:, cols] = jnp.exp2(s_buf[par][:, cols] - m_new).astype(BF16)

    def pv(qi, kj, par, diag):
        start = pl.multiple_of(kj * T, T)

        def p_dot_v(r0, r1, n_cols):
            parts = [_dot(p_buf[par][r0:r1, c:c + MXU_K], v_ref[pl.ds(start + c, MXU_K), :])
                     for c in range(0, n_cols, MXU_K)]
            return sum(parts[1:], parts[0])

        if diag:
            acc_all[half_rows(qi, 0), :] = p_dot_v(0, half, half)
            acc_all[half_rows(qi, 1), :] = p_dot_v(half, T, T)
        else:
            acc_all[rows(qi), :] = al_buf[par][...] * acc_all[rows(qi), :] + p_dot_v(0, T, T)

    def write_first_head(n):
        r = rows(jnp.minimum(n, nq - 1))
        acc = acc_all[r, :]
        lane = lax.broadcasted_iota(jnp.int32, acc.shape, 1)
        inv_l = 1.0 / acc[:, head_dim:head_dim + 1]
        out = jnp.where(lane < head_dim, acc * inv_l, 0.0)
        o_ref[r, :] = jnp.where(n < nq + depth - 1, out, o_ref[r, :].astype(F32)).astype(BF16)

    def step(c, n, blocks, diag):
        (q0, k0), (q1, _), (q2, k2) = blocks
        if diag and slot == 1:
            write_first_head(n)
        pv(q2, k2, c % 2, diag)
        scores(q0, k0, c % 2, diag)
        softmax(q1, (c - 1) % 2, diag)

    def run(n_blocks, block_at, diag, steps_per_trip):
        spare = (jnp.int32(nq), jnp.int32(0))

        def trip(u, carry):
            hist = [carry[0:2], carry[2:4]]
            for c in range(steps_per_trip):
                n = u * steps_per_trip + c
                nxt = block_at(n)
                step(c, n, [nxt] + hist, diag)
                hist = [nxt] + hist[:1]
            return tuple(x for h in hist for x in h)

        n_trips = (n_blocks + (depth - 2 + steps_per_trip)) // steps_per_trip
        lax.fori_loop(0, n_trips, trip, spare * 2)

    def diag_block(n):
        return jnp.minimum(n, nq), jnp.minimum(n, nq - 1)

    def lower_block(n):
        return tab_q[head_idx, n], tab_k[head_idx, n]

    run(nq, diag_block, True, FOX_DIAG_UNROLL)
    if nq > 1:
        run(count[head_idx], lower_block, False, FOX_UNROLL)

    if slot == 0:
        return

    per_trip = math.gcd(nq, FOX_FINISH_BLOCKS)

    def finish(u, carry):
        for i in range(per_trip):
            r = rows(u * per_trip + i)
            acc = acc_all[r, :]
            lane = lax.broadcasted_iota(jnp.int32, acc.shape, 1)
            inv_l = 1.0 / acc[:, 0:1]
            o_ref[r, :] = jnp.where(lane < head_dim, o_ref[r, :].astype(F32),
                                    acc * inv_l).astype(BF16)
        return carry

    lax.fori_loop(0, nq // per_trip, finish, 0)


def _fox_block_table(c_edges, qk_bound, n_heads, nq):
    B = c_edges.shape[0]
    c_first = c_edges[:, :, 0, :n_heads]
    c_last = c_edges[:, :, 1, :n_heads]
    gap = c_first[:, :, None, :] - c_last[:, None, :, :]
    qi = np.arange(nq)[:, None]
    kj = np.arange(nq)[None, :]
    lower = jnp.asarray(kj < qi)[None, :, :, None]
    keep = lower & (2.0 * qk_bound + gap + FOX_BOUND_SLACK > -FOX_ZERO_LOG2)
    keep = jnp.transpose(keep, (0, 3, 1, 2)).reshape(B * n_heads, nq * nq)
    count = jnp.sum(keep, axis=1).astype(jnp.int32)
    order = jnp.argsort(~keep, axis=1, stable=True).astype(jnp.int32)
    n_tab = -(-(nq * (nq - 1) // 2 + 2) // FOX_UNROLL) * FOX_UNROLL
    order = jnp.pad(order, ((0, 0), (0, max(0, n_tab - nq * nq))))[:, :n_tab]
    live = jnp.arange(n_tab)[None, :] < count[:, None]
    tab_q = jnp.where(live, order // nq, nq).astype(jnp.int32)
    tab_k = jnp.where(live, order % nq, 0).astype(jnp.int32)
    return tab_q, tab_k, count


def _fox(qa, ka, va, c_edges, qk_bound, head_dim):
    B, S, W = qa.shape
    H = W // LANES
    T = min(FOX_TILE, S)
    assert S % T == 0 and c_edges.shape[1] * T == S
    nq = S // T
    tab_q, tab_k, count = _fox_block_table(c_edges, qk_bound, H, nq)
    assert H % 2 == 0 and 2 * head_dim == LANES
    pair_in = pl.BlockSpec((None, S, 2 * LANES), lambda b, hp, *_: (b, 0, hp))
    pair_out = pl.BlockSpec((None, S, LANES), lambda b, hp, *_: (b, 0, hp))
    state = pltpu.VMEM(((nq + 1) * T, LANES), F32)
    return pl.pallas_call(
        functools.partial(_fox_kernel, head_dim=head_dim, nq=nq),
        out_shape=jax.ShapeDtypeStruct((B, S, H * head_dim), BF16),
        grid_spec=pltpu.PrefetchScalarGridSpec(
            num_scalar_prefetch=3,
            grid=(B, H // 2),
            in_specs=[pair_in, pair_in, pair_in],
            out_specs=pair_out,
            scratch_shapes=([state, state] + [pltpu.VMEM((T, T), F32)] * 2
                            + [pltpu.VMEM((T, T), BF16)] * 2
                            + [pltpu.VMEM((T, LANES), F32)] * 2)),
        compiler_params=_params(("arbitrary", "arbitrary")),
        name="fox",
    )(tab_q, tab_k, count, qa, ka, va)


def _mem_kv_kernel(mem_ref, g_ref, w_ref, kg_ref, k_ref, v_ref, *, xd):
    x = mem_ref[...]
    D = x.shape[1]
    xn = (x * lax.rsqrt(jnp.mean(x * x, axis=-1, keepdims=True) + EPS) * g_ref[...]).astype(BF16)
    kg = kg_ref[...]
    for h in range(D // xd):
        kh = _dot(xn, w_ref[:, h * xd:(h + 1) * xd])
        kn = kh * lax.rsqrt(jnp.mean(kh * kh, axis=-1, keepdims=True) + EPS) * kg
        k_ref[:, h * xd:(h + 1) * xd] = kn.astype(BF16)
    v_ref[...] = _dot(xn, w_ref[:, D:]).astype(BF16)


def _mem_kv(mem, norm_mem_g, wkv, k_norm_g):
    B, M, D = mem.shape
    xd = k_norm_g.shape[0]
    kg = (k_norm_g * (LOG2E / math.sqrt(xd))).reshape(1, xd)
    blk = pl.BlockSpec((None, M, D), lambda b: (b, 0, 0))
    return pl.pallas_call(
        functools.partial(_mem_kv_kernel, xd=xd),
        out_shape=[jax.ShapeDtypeStruct((B, M, D), BF16)] * 2,
        grid=(B,),
        in_specs=[blk, _const_spec((1, D)), _const_spec((D, 2 * D)), _const_spec((1, xd))],
        out_specs=[blk, blk],
        compiler_params=_params(("arbitrary",)),
        name="mem_kv",
    )(mem, norm_mem_g.reshape(1, D), wkv.astype(BF16), kg)


def _mix_kernel(x_ref, fox_ref, gla_ref, wof_ref, wog_ref, g_ref, wq_ref, qg_ref, km_ref, vm_ref,
                wo_ref, h_ref, *, xd):
    tm, D = x_ref.shape
    splits = [slice(i * tm // MIX_SPLITS, (i + 1) * tm // MIX_SPLITS) for i in range(MIX_SPLITS)]
    heads = [slice(hd * xd, (hd + 1) * xd) for hd in range(D // xd)]
    qg = qg_ref[...]

    def rms(v, gain):
        return v * lax.rsqrt(jnp.mean(v * v, axis=-1, keepdims=True) + EPS) * gain

    hs = [x_ref[r, :] + _dot(fox_ref[r, :], wof_ref[...]) + _dot(gla_ref[r, :], wog_ref[...])
          for r in splits]
    hns = [rms(h, g_ref[...]).astype(BF16) for h in hs]
    qs = [_dot(hn, wq_ref[...]) for hn in hns]
    qns = [[rms(q[:, sl], qg).astype(BF16) for sl in heads] for q in qs]
    ss = [[_dot_nt(qn, km_ref[:, sl]) for qn, sl in zip(row, heads)] for row in qns]
    ps = [[jnp.exp2(s - jnp.max(s, axis=-1, keepdims=True)) for s in row] for row in ss]
    os = [[(_dot(p.astype(BF16), vm_ref[:, sl]) * (1.0 / jnp.sum(p, axis=-1, keepdims=True))
            ).astype(BF16) for p, sl in zip(row, heads)] for row in ps]
    for r, h, o in zip(splits, hs, os):
        h_ref[r, :] = h + _dot(jnp.concatenate(o, axis=1), wo_ref[...])


def _mix(x, fox, gla, w_out, norm_g, wq, q_norm_g, km, vm, wo):
    B, S, D = x.shape
    fw = fox.shape[2]
    gw = gla.shape[2]
    M = km.shape[1]
    xd = q_norm_g.shape[0]
    tm = min(ROW_TILE, S)
    w_out_bf = w_out.astype(BF16)
    wof, wog = w_out_bf[:fw], w_out_bf[fw:]
    row = lambda c: pl.BlockSpec((None, tm, c), lambda b, i: (b, i, 0))
    mem = pl.BlockSpec((None, M, D), lambda b, i: (b, 0, 0))
    return pl.pallas_call(
        functools.partial(_mix_kernel, xd=xd),
        out_shape=jax.ShapeDtypeStruct((B, S, D), F32),
        grid=(B, S // tm),
        in_specs=[row(D), row(fw), row(gw), _const_spec((fw, D)), _const_spec((gw, D)),
                  _const_spec((1, D)), _const_spec((D, D)), _const_spec((1, xd)), mem, mem,
                  _const_spec((D, D))],
        out_specs=row(D),
        compiler_params=_params(("arbitrary", "arbitrary")),
        name="mix",
    )(x, fox, gla, wof, wog, norm_g.reshape(1, D), wq.astype(BF16), q_norm_g.reshape(1, xd),
      km, vm, wo.astype(BF16))


def _mlp_kernel(h_ref, g_ref, w1_ref, w2_ref, y_ref, *, slab):
    h = h_ref[...]
    hn = (h * lax.rsqrt(jnp.mean(h * h, axis=-1, keepdims=True) + EPS) * g_ref[...]).astype(BF16)
    n_slabs = w1_ref.shape[1] // slab
    y = h
    act = None
    for j in range(n_slabs + 1):
        nxt = None
        if j < n_slabs:
            u = jnp.maximum(_dot(hn, w1_ref[:, j * slab:(j + 1) * slab]), 0.0)
            nxt = (u * u).astype(BF16)
        if act is not None:
            y = y + _dot(act, w2_ref[(j - 1) * slab:j * slab, :])
        act = nxt
    y_ref[...] = y


def _mlp(h, norm_g, w1, w2):
    B, S, D = h.shape
    F = w1.shape[1]
    tm = min(ROW_TILE, S)
    row = pl.BlockSpec((None, tm, D), lambda b, i: (b, i, 0))
    return pl.pallas_call(
        functools.partial(_mlp_kernel, slab=min(F, 1024)),
        out_shape=jax.ShapeDtypeStruct((B, S, D), F32),
        grid=(B, S // tm),
        in_specs=[row, _const_spec((1, D)), _const_spec((D, F)), _const_spec((F, D))],
        out_specs=row,
        compiler_params=_params(("arbitrary", "arbitrary")),
        name="mlp",
    )(h, norm_g.reshape(1, D), w1.astype(BF16), w2.astype(BF16))


def kernel(x, mem, norm_mix_g, w_in, fox_b_f, fox_q_norm_g, fox_k_norm_g, gla_w_gate2, gla_b_gate,
           gla_out_norm_g, w_out, norm_xattn_g, norm_mem_g, xattn_wq, xattn_wkv, xattn_q_norm_g,
           xattn_k_norm_g, xattn_wo, norm_mlp_g, mlp_w1, mlp_w2):
    head_dim = fox_q_norm_g.shape[0]
    qa, ka, va, gq, gk, gv, gr, gl, c_edges = _in_proj(
        x, norm_mix_g, w_in, fox_b_f, fox_q_norm_g, fox_k_norm_g, gla_w_gate2, gla_b_gate,
        gla_out_norm_g.shape[0])
    gla = _gla(gq, gk, gv, gr, gl, gla_out_norm_g)
    qk_bound = (FOX_ROUNDING_SLACK * head_dim * LOG2E / math.sqrt(head_dim)
                * jnp.max(jnp.abs(fox_q_norm_g)) * jnp.max(jnp.abs(fox_k_norm_g)))
    fox = _fox(qa, ka, va, c_edges, qk_bound, head_dim)
    km, vm = _mem_kv(mem, norm_mem_g, xattn_wkv, xattn_k_norm_g)
    h = _mix(x, fox, gla, w_out, norm_xattn_g, xattn_wq, xattn_q_norm_g, km, vm, xattn_wo)
    return _mlp(h, norm_mlp_g, mlp_w1, mlp_w2)
```

```python
import functools
import math

import numpy as np
import jax
import jax.numpy as jnp
from jax import lax
from jax.experimental import pallas as pl
from jax.experimental.pallas import tpu as pltpu

F32 = jnp.float32
BF16 = jnp.bfloat16

EPS = 1e-6
LOG2E = 1.4426950408889634

GLA_CHUNK = 64
GLA_HEADS = 4
GLA_TAU = 16.0

LANES = 128
MXU_K = 256
VMEM_LIMIT_BYTES = 56 * 1024 * 1024

ROW_TILE = 512
MIX_SPLITS = 2
FOX_TILE = 512
FOX_UNROLL = 16
FOX_DIAG_UNROLL = 6
FOX_FINISH_BLOCKS = 4
FOX_ZERO_LOG2 = 160.0
FOX_BOUND_SLACK = 2.0
FOX_ROUNDING_SLACK = 1.02
AUG0 = 64


def _const_spec(shape):
    return pl.BlockSpec(shape, lambda *_: (0,) * len(shape))


def _params(semantics, flags=None):
    return pltpu.CompilerParams(dimension_semantics=semantics,
                                vmem_limit_bytes=VMEM_LIMIT_BYTES, flags=flags)


def _split3(c):
    hi = c.astype(BF16)
    r = c - hi.astype(F32)
    mid = r.astype(BF16)
    lo = (r - mid.astype(F32)).astype(BF16)
    return hi, mid, lo


def _log_sigmoid(z):
    return -(jnp.maximum(-z, 0.0) + jnp.log1p(jnp.exp(-jnp.abs(z))))


def _dot(a, b):
    return jnp.dot(a, b, preferred_element_type=F32)


def _dot_nt(a, b):
    return lax.dot_general(a, b, (((1,), (1,)), ((), ())), preferred_element_type=F32)


def _dot_tn(a, b):
    return lax.dot_general(a, b, (((0,), (0,)), ((), ())), preferred_element_type=F32)


def _in_proj_kernel(x_ref, g_ref, w_ref, ones_ref, tri_ref, place_ref, qg_ref, kg_ref,
                    bf_ref, wg2_ref, bg_ref, rows_ref,
                    qa_ref, ka_ref, va_ref, gq_ref, gk_ref, gv_ref, gr_ref, gl_ref, ce_ref,
                    carry_ref, *, fox_w, gkw, gvw, n_fox, head_dim):
    @pl.when(pl.program_id(1) == 0)
    def _():
        carry_ref[...] = jnp.zeros_like(carry_ref)

    tm = x_ref.shape[0]
    xns = []
    for r in (slice(0, tm // 2), slice(tm // 2, tm)):
        x = x_ref[r, :]
        xns.append((x * lax.rsqrt(jnp.mean(x * x, axis=-1, keepdims=True) + EPS)
                    * g_ref[...]).astype(BF16))
    xn = jnp.concatenate(xns, axis=0)

    o_k = fox_w
    o_v = 2 * fox_w
    o_gq = 3 * fox_w
    o_gk = o_gq + gkw
    o_gv = o_gk + gkw
    o_gr = o_gv + gvw
    o_sm = o_gr + gvw

    def proj(lo, width):
        return _dot(xn, w_ref[:, lo:lo + width])

    def head_norm(f, gain_row):
        f2 = (f * f).astype(BF16)
        ssq = jnp.concatenate([_dot(f2[:, c:c + MXU_K], ones_ref[...])
                               for c in range(0, fox_w, MXU_K)], axis=1)
        return f * lax.rsqrt(ssq * (1.0 / head_dim) + EPS) * gain_row

    def spread(f):
        return jnp.concatenate([f[:, (h // 2) * LANES:(h // 2 + 1) * LANES] for h in range(n_fox)],
                               axis=1)

    gr_sm = jnp.concatenate([_dot(h, w_ref[:, o_gr:o_gr + gvw + LANES]) for h in xns], axis=0)
    sm = gr_sm[:, gvw:]
    lane = lax.broadcasted_iota(jnp.int32, sm.shape, 1)
    lf = jnp.where(lane < n_fox, _log_sigmoid(sm + bf_ref[...]), 0.0)
    fq = proj(0, fox_w)
    fk = proj(o_k, fox_w)
    fq = head_norm(fq, qg_ref[...])
    fv = proj(o_v, fox_w)
    fk = head_norm(fk, kg_ref[...])

    def pack3(v):
        hi, mid, lo = _split3(v)
        return (hi.astype(F32) + pltpu.roll(mid.astype(F32), n_fox, axis=1)
                + pltpu.roll(lo.astype(F32), 2 * n_fox, axis=1))

    cs = _dot(tri_ref[...], pack3(lf).astype(BF16))
    cs = cs + pltpu.roll(cs, LANES - n_fox, axis=1) + pltpu.roll(cs, LANES - 2 * n_fox, axis=1)
    c = jnp.where(lane < n_fox, cs, 0.0) + carry_ref[...]
    carry_ref[...] = c[tm - 1:tm, :]
    c2 = c * LOG2E
    cpk = pack3(c2).astype(BF16)
    ce_ref[...] = jnp.concatenate([c2[0:1], c2[tm - 1:tm], jnp.zeros((6, LANES), F32)], axis=0)

    g_qk = proj(o_gq, 2 * gkw)
    gq_ref[...] = g_qk[:, :gkw].astype(BF16)
    gk_ref[...] = g_qk[:, gkw:].astype(BF16)
    gv_ref[...] = proj(o_gv, gvw).astype(BF16)
    gr_ref[...] = gr_sm[:, :gvw].astype(BF16)
    gate = _dot(sm.astype(BF16), wg2_ref[...]) + bg_ref[...]
    gl_ref[...] = _log_sigmoid(gate) * (1.0 / GLA_TAU)

    wide = lax.broadcasted_iota(jnp.int32, (1, n_fox * LANES), 1)
    rel = (wide & (LANES - 1)) ^ jnp.where((wide // LANES) % 2 == 1, 0, AUG0)
    is_head_dim = rel >= AUG0
    placed = _dot(cpk, place_ref[...])
    va_ref[...] = jnp.where(is_head_dim, spread(fv), rows_ref[0:1, :]).astype(BF16)
    qa_ref[...] = jnp.where(is_head_dim, spread(fq),
                            jnp.where(rel < 3, placed, rows_ref[1:2, :])).astype(BF16)
    ka_ref[...] = jnp.where(is_head_dim, spread(fk),
                            jnp.where(rel >= 3, placed, rows_ref[2:3, :])).astype(BF16)


def _in_proj(x, norm_g, w_in, fox_b_f, fox_q_norm_g, fox_k_norm_g, gla_w_gate2, gla_b_gate, gvw):
    B, S, D = x.shape
    n_fox = fox_b_f.shape[0]
    head_dim = fox_q_norm_g.shape[0]
    fox_w = n_fox * head_dim
    rank, gkw = gla_w_gate2.shape
    tm = min(ROW_TILE, S)
    assert S % tm == 0 and head_dim == LANES // 2 and n_fox % 2 == 0
    assert 3 * n_fox <= LANES and n_fox + rank <= LANES

    sizes = (fox_w, fox_w, fox_w, n_fox, gkw, gkw, gvw, rank, gvw)
    offs = np.concatenate([[0], np.cumsum(sizes)])
    w_bf = w_in.astype(BF16)
    wq, wk, wv, wf, wgq, wgk, wgv, wlr, wgr = [w_bf[:, offs[i]:offs[i + 1]] for i in range(9)]
    w_all = jnp.concatenate([wq, wk, wv, wgq, wgk, wgv, wgr, wf, wlr,
                             jnp.zeros((D, LANES - n_fox - rank), BF16)], axis=1)
    width = w_all.shape[1]

    scale = (1.0 / math.sqrt(head_dim)) * LOG2E
    qg = (jnp.tile(fox_q_norm_g, n_fox) * scale).reshape(1, fox_w)
    kg = jnp.tile(fox_k_norm_g, n_fox).reshape(1, fox_w)
    bf = jnp.zeros((1, LANES), F32).at[0, :n_fox].set(fox_b_f)
    wg2 = jnp.zeros((LANES, gkw), F32).at[n_fox:n_fox + rank].set(gla_w_gate2).astype(BF16)
    bg = gla_b_gate.reshape(1, gkw)

    assert fox_w % MXU_K == 0 and MXU_K % head_dim == 0
    grp = np.arange(MXU_K) // head_dim
    ones_blk = jnp.asarray(grp[:, None] == grp[None, :], BF16)
    r = np.arange(tm)
    tri = jnp.asarray(r[None, :] <= r[:, None], BF16)
    place = np.zeros((LANES, n_fox * LANES), np.float32)
    const_rows = np.zeros((8, n_fox * LANES), np.float32)
    for h in range(n_fox):
        base = h * LANES + (AUG0 if h % 2 == 0 else 0)
        for part in range(3):
            place[part * n_fox + h, base + part] = 1.0
            place[part * n_fox + h, base + 3 + part] = -1.0
            const_rows[1, base + 3 + part] = 1.0
            const_rows[2, base + part] = 1.0
        const_rows[0, base] = 1.0
    place = jnp.asarray(place, BF16)
    const_rows = jnp.asarray(const_rows)

    row = lambda c: pl.BlockSpec((None, tm, c), lambda b, i: (b, i, 0))
    kern = functools.partial(_in_proj_kernel, fox_w=fox_w, gkw=gkw, gvw=gvw, n_fox=n_fox,
                             head_dim=head_dim)
    aug_w = n_fox * LANES
    out_shapes = [jax.ShapeDtypeStruct((B, S, aug_w), BF16)] * 3 + [
        jax.ShapeDtypeStruct((B, S, gkw), BF16), jax.ShapeDtypeStruct((B, S, gkw), BF16),
        jax.ShapeDtypeStruct((B, S, gvw), BF16), jax.ShapeDtypeStruct((B, S, gvw), BF16),
        jax.ShapeDtypeStruct((B, S, gkw), F32),
        jax.ShapeDtypeStruct((B, S // tm, 8, LANES), F32)]
    return pl.pallas_call(
        kern,
        out_shape=out_shapes,
        grid=(B, S // tm),
        in_specs=[row(D), _const_spec((1, D)), _const_spec((D, width)),
                  _const_spec((MXU_K, MXU_K)), _const_spec((tm, tm)),
                  _const_spec((LANES, aug_w)),
                  _const_spec((1, fox_w)), _const_spec((1, fox_w)), _const_spec((1, LANES)),
                  _const_spec((LANES, gkw)), _const_spec((1, gkw)), _const_spec((8, aug_w))],
        out_specs=[row(aug_w), row(aug_w), row(aug_w), row(gkw), row(gkw), row(gvw), row(gvw),
                   row(gkw), pl.BlockSpec((None, None, 8, LANES), lambda b, i: (b, i, 0, 0))],
        scratch_shapes=[pltpu.VMEM((1, LANES), F32)],
        compiler_params=_params(("arbitrary", "arbitrary")),
        name="in_proj",
    )(x, norm_g.reshape(1, D), w_all, ones_blk, tri, place, qg, kg, bf, wg2, bg, const_rows)


def _gla_kernel(gq_ref, gk_ref, gv_ref, gr_ref, gl_ref, tri3_ref, gain_ref, out_ref,
                st_ref, o_ref, *, dk, dv):
    @pl.when(pl.program_id(1) == 0)
    def _():
        st_ref[...] = jnp.zeros_like(st_ref)

    T, kw = gl_ref.shape
    vw = gv_ref.shape[1]
    n_heads = kw // dk
    C = GLA_CHUNK

    n_chunks = T // C
    chunks = [slice(c * C, (c + 1) * C) for c in range(n_chunks)]

    hi, mid, lo = _split3(gl_ref[...])
    tri3 = tri3_ref[...]
    bcs = [_dot(tri3, jnp.concatenate([hi[r], mid[r], lo[r]], axis=0)) for r in chunks]
    b_last = [b[C - 1:C] for b in bcs]
    bc = jnp.concatenate(bcs, axis=0)
    k = gk_ref[...].astype(F32)
    q_dec = gq_ref[...].astype(F32) * (dk ** -0.5) * jnp.exp(bc)
    k_dec = (k * jnp.exp(-bc)).astype(BF16)

    klane = lax.broadcasted_iota(jnp.int32, (1, kw), 1) // dk
    row_h = lax.broadcasted_iota(jnp.int32, (n_heads * C, C), 0)
    col = lax.broadcasted_iota(jnp.int32, (n_heads * C, C), 1)
    tril = col <= (row_h & (C - 1))
    bd = (lax.broadcasted_iota(jnp.int32, (vw, kw), 0) // dv
          == lax.broadcasted_iota(jnp.int32, (vw, kw), 1) // dk)

    d_sts = [_dot_tn(gv_ref[r, :], (k[r] * jnp.exp(bl - b)).astype(BF16))
             for r, b, bl in zip(chunks, bcs, b_last)]
    st = st_ref[...]
    sts = []
    for d_st, bl in zip(d_sts, b_last):
        sts.append(st.astype(BF16))
        st = st * jnp.exp(bl) + jnp.where(bd, d_st, 0.0)
    st_ref[...] = st

    for r, st_c in zip(chunks, sts):
        qd = q_dec[r]
        lhs = jnp.concatenate([jnp.where(klane == h, qd, 0.0) for h in range(n_heads)],
                              axis=0).astype(BF16)
        a = jnp.where(tril, _dot_nt(lhs, k_dec[r]), 0.0).astype(BF16)
        v = gv_ref[r, :]
        o_intra = jnp.concatenate(
            [_dot(a[h * C:(h + 1) * C], v[:, h * dv:(h + 1) * dv]) for h in range(n_heads)],
            axis=1)
        o_ref[r, :] = o_intra + _dot_nt(qd.astype(BF16), st_c)

    o = o_ref[...]
    gr = gr_ref[...].astype(F32)
    gain = gain_ref[...]
    outs = []
    for h in range(n_heads):
        oh = o[:, h * dv:(h + 1) * dv]
        on = oh * lax.rsqrt(jnp.mean(oh * oh, axis=-1, keepdims=True) + EPS)
        outs.append(on * gain[:, h * dv:(h + 1) * dv])
    out_ref[...] = (jnp.concatenate(outs, axis=1) * (gr * jax.nn.sigmoid(gr))).astype(BF16)


def _gla(gq, gk, gv, gr, gl, out_norm_g):
    B, S, kw = gq.shape
    vw = gv.shape[2]
    dk, dv = kw // GLA_HEADS, vw // GLA_HEADS
    T = min(ROW_TILE, S)
    assert S % T == 0 and T % GLA_CHUNK == 0
    r = np.arange(GLA_CHUNK)
    tri3 = jnp.asarray(np.tile(r[None, :] <= r[:, None], (1, 3)), BF16)
    row = lambda c: pl.BlockSpec((None, T, c), lambda b, i: (b, i, 0))
    return pl.pallas_call(
        functools.partial(_gla_kernel, dk=dk, dv=dv),
        out_shape=jax.ShapeDtypeStruct((B, S, vw), BF16),
        grid=(B, S // T),
        in_specs=[row(kw), row(kw), row(vw), row(vw), row(kw),
                  _const_spec((GLA_CHUNK, 3 * GLA_CHUNK)), _const_spec((1, vw))],
        out_specs=row(vw),
        scratch_shapes=[pltpu.VMEM((vw, kw), F32), pltpu.VMEM((T, vw), F32)],
        compiler_params=_params(("arbitrary", "arbitrary")),
        name="gla",
    )(gq, gk, gv, gr, gl, tri3, out_norm_g.reshape(1, vw))


def _fox_kernel(tab_q, tab_k, count, q_ref, k_ref, v_ref, o_ref, *scratch, head_dim, nq):
    pair = pl.program_id(0) * pl.num_programs(1) + pl.program_id(1)
    for hh in range(2):
        lanes = pl.ds(hh * LANES, LANES)
        _fox_head(tab_q, tab_k, count, 2 * pair + hh, hh, q_ref.at[:, lanes], k_ref.at[:, lanes],
                  v_ref.at[:, lanes], o_ref, *scratch, head_dim=head_dim, nq=nq)


def _fox_head(tab_q, tab_k, count, head_idx, slot, q_ref, k_ref, v_ref, o_ref, m_all, acc_all,
              s0, s1, p0, p1, al0, al1, *, head_dim, nq):
    T = s0.shape[0]
    s_buf, p_buf, al_buf = (s0, s1), (p0, p1), (al0, al1)
    depth = 3

    def rows(blk):
        start = blk * T
        return pl.ds(start if isinstance(blk, int) else pl.multiple_of(start, T), T)

    m_all[rows(nq), :] = jnp.zeros((T, LANES), F32)
    acc_all[rows(nq), :] = jnp.zeros((T, LANES), F32)
    for ref in (s1, p0, al0):
        ref[...] = jnp.zeros_like(ref)

    half = T // 2

    def half_rows(blk, i):
        return pl.ds(pl.multiple_of(blk * T, T) + i * half, half)

    def scores(qi, kj, par, diag):
        qr = jnp.minimum(qi, nq - 1)
        if not diag:
            s_buf[par][...] = _dot_nt(q_ref[rows(qr), :], k_ref[rows(kj), :])
            return
        top = _dot_nt(q_ref[half_rows(qr, 0), :], k_ref[half_rows(kj, 0), :])
        bot = _dot_nt(q_ref[half_rows(qr, 1), :], k_ref[rows(kj), :])
        def causal(shape, offset):
            row = lax.broadcasted_iota(jnp.int32, shape, 0)
            col = lax.broadcasted_iota(jnp.int32, shape, 1)
            return col <= row + offset

        s_buf[par][:half, :half] = jnp.where(causal((half, half), 0), top, -jnp.inf)
        s_buf[par][:half, half:] = jnp.full((half, half), -jnp.inf, F32)
        s_buf[par][half:, :] = jnp.where(causal((half, T), half), bot, -jnp.inf)

    def softmax(qi, par, diag):
        m_cur = jnp.broadcast_to(jnp.max(s_buf[par][...], axis=-1, keepdims=True), (T, LANES))
        if diag:
            m_new = m_cur
        else:
            m_prev = m_all[rows(qi), :]
            m_new = jnp.maximum(m_prev, m_cur)
            al_buf[par][...] = jnp.exp2(m_prev - m_new)
        m_all[rows(qi), :] = m_new
        for g in range(T // LANES):
            cols = slice(g * LANES, (g + 1) * LANES)
            p_buf[par][:, cols] = jnp.exp2(s_buf[par][:, cols] - m_new).astype(BF16)

    def pv(qi, kj, par, diag):
        start = pl.multiple_of(kj * T, T)

        def p_dot_v(r0, r1, n_cols):
            parts = [_dot(p_buf[par][r0:r1, c:c + MXU_K], v_ref[pl.ds(start + c, MXU_K), :])
                     for c in range(0, n_cols, MXU_K)]
            return sum(parts[1:], parts[0])

        if diag:
            acc_all[half_rows(qi, 0), :] = p_dot_v(0, half, half)
            acc_all[half_rows(qi, 1), :] = p_dot_v(half, T, T)
        else:
            acc_all[rows(qi), :] = al_buf[par][...] * acc_all[rows(qi), :] + p_dot_v(0, T, T)

    def write_first_head(n):
        r = rows(jnp.minimum(n, nq - 1))
        acc = acc_all[r, :]
        lane = lax.broadcasted_iota(jnp.int32, acc.shape, 1)
        inv_l = 1.0 / acc[:, head_dim:head_dim + 1]
        out = jnp.where(lane < head_dim, acc * inv_l, 0.0)
        o_ref[r, :] = jnp.where(n < nq + depth - 1, out, o_ref[r, :].astype(F32)).astype(BF16)

    def step(c, n, blocks, diag):
        (q0, k0), (q1, _), (q2, k2) = blocks
        if diag and slot == 1:
            write_first_head(n)
        pv(q2, k2, c % 2, diag)
        scores(q0, k0, c % 2, diag)
        softmax(q1, (c - 1) % 2, diag)

    def run(n_blocks, block_at, diag, steps_per_trip):
        spare = (jnp.int32(nq), jnp.int32(0))

        def trip(u, carry):
            hist = [carry[0:2], carry[2:4]]
            for c in range(steps_per_trip):
                n = u * steps_per_trip + c
                nxt = block_at(n)
                step(c, n, [nxt] + hist, diag)
                hist = [nxt] + hist[:1]
            return tuple(x for h in hist for x in h)

        n_trips = (n_blocks + (depth - 2 + steps_per_trip)) // steps_per_trip
        lax.fori_loop(0, n_trips, trip, spare * 2)

    def diag_block(n):
        return jnp.minimum(n, nq), jnp.minimum(n, nq - 1)

    def lower_block(n):
        return tab_q[head_idx, n], tab_k[head_idx, n]

    run(nq, diag_block, True, FOX_DIAG_UNROLL)
    if nq > 1:
        run(count[head_idx], lower_block, False, FOX_UNROLL)

    if slot == 0:
        return

    per_trip = math.gcd(nq, FOX_FINISH_BLOCKS)

    def finish(u, carry):
        for i in range(per_trip):
            r = rows(u * per_trip + i)
            acc = acc_all[r, :]
            lane = lax.broadcasted_iota(jnp.int32, acc.shape, 1)
            inv_l = 1.0 / acc[:, 0:1]
            o_ref[r, :] = jnp.where(lane < head_dim, o_ref[r, :].astype(F32),
                                    acc * inv_l).astype(BF16)
        return carry

    lax.fori_loop(0, nq // per_trip, finish, 0)


def _fox_block_table(c_edges, qk_bound, n_heads, nq):
    B = c_edges.shape[0]
    c_first = c_edges[:, :, 0, :n_heads]
    c_last = c_edges[:, :, 1, :n_heads]
    gap = c_first[:, :, None, :] - c_last[:, None, :, :]
    qi = np.arange(nq)[:, None]
    kj = np.arange(nq)[None, :]
    lower = jnp.asarray(kj < qi)[None, :, :, None]
    keep = lower & (2.0 * qk_bound + gap + FOX_BOUND_SLACK > -FOX_ZERO_LOG2)
    keep = jnp.transpose(keep, (0, 3, 1, 2)).reshape(B * n_heads, nq * nq)
    count = jnp.sum(keep, axis=1).astype(jnp.int32)
    order = jnp.argsort(~keep, axis=1, stable=True).astype(jnp.int32)
    n_tab = -(-(nq * (nq - 1) // 2 + 2) // FOX_UNROLL) * FOX_UNROLL
    order = jnp.pad(order, ((0, 0), (0, max(0, n_tab - nq * nq))))[:, :n_tab]
    live = jnp.arange(n_tab)[None, :] < count[:, None]
    tab_q = jnp.where(live, order // nq, nq).astype(jnp.int32)
    tab_k = jnp.where(live, order % nq, 0).astype(jnp.int32)
    return tab_q, tab_k, count


def _fox(qa, ka, va, c_edges, qk_bound, head_dim):
    B, S, W = qa.shape
    H = W // LANES
    T = min(FOX_TILE, S)
    assert S % T == 0 and c_edges.shape[1] * T == S
    nq = S // T
    tab_q, tab_k, count = _fox_block_table(c_edges, qk_bound, H, nq)
    assert H % 2 == 0 and 2 * head_dim == LANES
    pair_in = pl.BlockSpec((None, S, 2 * LANES), lambda b, hp, *_: (b, 0, hp))
    pair_out = pl.BlockSpec((None, S, LANES), lambda b, hp, *_: (b, 0, hp))
    state = pltpu.VMEM(((nq + 1) * T, LANES), F32)
    return pl.pallas_call(
        functools.partial(_fox_kernel, head_dim=head_dim, nq=nq),
        out_shape=jax.ShapeDtypeStruct((B, S, H * head_dim), BF16),
        grid_spec=pltpu.PrefetchScalarGridSpec(
            num_scalar_prefetch=3,
            grid=(B, H // 2),
            in_specs=[pair_in, pair_in, pair_in],
            out_specs=pair_out,
            scratch_shapes=([state, state] + [pltpu.VMEM((T, T), F32)] * 2
                            + [pltpu.VMEM((T, T), BF16)] * 2
                            + [pltpu.VMEM((T, LANES), F32)] * 2)),
        compiler_params=_params(("arbitrary", "arbitrary")),
        name="fox",
    )(tab_q, tab_k, count, qa, ka, va)


def _mem_kv_kernel(mem_ref, g_ref, w_ref, kg_ref, k_ref, v_ref, *, xd):
    x = mem_ref[...]
    D = x.shape[1]
    xn = (x * lax.rsqrt(jnp.mean(x * x, axis=-1, keepdims=True) + EPS) * g_ref[...]).astype(BF16)
    kg = kg_ref[...]
    for h in range(D // xd):
        kh = _dot(xn, w_ref[:, h * xd:(h + 1) * xd])
        kn = kh * lax.rsqrt(jnp.mean(kh * kh, axis=-1, keepdims=True) + EPS) * kg
        k_ref[:, h * xd:(h + 1) * xd] = kn.astype(BF16)
    v_ref[...] = _dot(xn, w_ref[:, D:]).astype(BF16)


def _mem_kv(mem, norm_mem_g, wkv, k_norm_g):
    B, M, D = mem.shape
    xd = k_norm_g.shape[0]
    kg = (k_norm_g * (LOG2E / math.sqrt(xd))).reshape(1, xd)
    blk = pl.BlockSpec((None, M, D), lambda b: (b, 0, 0))
    return pl.pallas_call(
        functools.partial(_mem_kv_kernel, xd=xd),
        out_shape=[jax.ShapeDtypeStruct((B, M, D), BF16)] * 2,
        grid=(B,),
        in_specs=[blk, _const_spec((1, D)), _const_spec((D, 2 * D)), _const_spec((1, xd))],
        out_specs=[blk, blk],
        compiler_params=_params(("arbitrary",)),
        name="mem_kv",
    )(mem, norm_mem_g.reshape(1, D), wkv.astype(BF16), kg)


def _mix_kernel(x_ref, fox_ref, gla_ref, wof_ref, wog_ref, g_ref, wq_ref, qg_ref, km_ref, vm_ref,
                wo_ref, h_ref, *, xd):
    tm, D = x_ref.shape
    splits = [slice(i * tm // MIX_SPLITS, (i + 1) * tm // MIX_SPLITS) for i in range(MIX_SPLITS)]
    heads = [slice(hd * xd, (hd + 1) * xd) for hd in range(D // xd)]
    qg = qg_ref[...]

    def rms(v, gain):
        return v * lax.rsqrt(jnp.mean(v * v, axis=-1, keepdims=True) + EPS) * gain

    hs = [x_ref[r, :] + _dot(fox_ref[r, :], wof_ref[...]) + _dot(gla_ref[r, :], wog_ref[...])
          for r in splits]
    hns = [rms(h, g_ref[...]).astype(BF16) for h in hs]
    qs = [_dot(hn, wq_ref[...]) for hn in hns]
    qns = [[rms(q[:, sl], qg).astype(BF16) for sl in heads] for q in qs]
    ss = [[_dot_nt(qn, km_ref[:, sl]) for qn, sl in zip(row, heads)] for row in qns]
    ps = [[jnp.exp2(s - jnp.max(s, axis=-1, keepdims=True)) for s in row] for row in ss]
    os = [[(_dot(p.astype(BF16), vm_ref[:, sl]) * (1.0 / jnp.sum(p, axis=-1, keepdims=True))
            ).astype(BF16) for p, sl in zip(row, heads)] for row in ps]
    for r, h, o in zip(splits, hs, os):
        h_ref[r, :] = h + _dot(jnp.concatenate(o, axis=1), wo_ref[...])


def _mix(x, fox, gla, w_out, norm_g, wq, q_norm_g, km, vm, wo):
    B, S, D = x.shape
    fw = fox.shape[2]
    gw = gla.shape[2]
    M = km.shape[1]
    xd = q_norm_g.shape[0]
    tm = min(ROW_TILE, S)
    w_out_bf = w_out.astype(BF16)
    wof, wog = w_out_bf[:fw], w_out_bf[fw:]
    row = lambda c: pl.BlockSpec((None, tm, c), lambda b, i: (b, i, 0))
    mem = pl.BlockSpec((None, M, D), lambda b, i: (b, 0, 0))
    return pl.pallas_call(
        functools.partial(_mix_kernel, xd=xd),
        out_shape=jax.ShapeDtypeStruct((B, S, D), F32),
        grid=(B, S // tm),
        in_specs=[row(D), row(fw), row(gw), _const_spec((fw, D)), _const_spec((gw, D)),
                  _const_spec((1, D)), _const_spec((D, D)), _const_spec((1, xd)), mem, mem,
                  _const_spec((D, D))],
        out_specs=row(D),
        compiler_params=_params(("arbitrary", "arbitrary")),
        name="mix",
    )(x, fox, gla, wof, wog, norm_g.reshape(1, D), wq.astype(BF16), q_norm_g.reshape(1, xd),
      km, vm, wo.astype(BF16))


def _mlp_kernel(h_ref, g_ref, w1_ref, w2_ref, y_ref, *, slab):
    h = h_ref[...]
    hn = (h * lax.rsqrt(jnp.mean(h * h, axis=-1, keepdims=True) + EPS) * g_ref[...]).astype(BF16)
    n_slabs = w1_ref.shape[1] // slab
    y = h
    act = None
    for j in range(n_slabs + 1):
        nxt = None
        if j < n_slabs:
            u = jnp.maximum(_dot(hn, w1_ref[:, j * slab:(j + 1) * slab]), 0.0)
            nxt = (u * u).astype(BF16)
        if act is not None:
            y = y + _dot(act, w2_ref[(j - 1) * slab:j * slab, :])
        act = nxt
    y_ref[...] = y


def _mlp(h, norm_g, w1, w2):
    B, S, D = h.shape
    F = w1.shape[1]
    tm = min(ROW_TILE, S)
    row = pl.BlockSpec((None, tm, D), lambda b, i: (b, i, 0))
    return pl.pallas_call(
        functools.partial(_mlp_kernel, slab=min(F, 1024)),
        out_shape=jax.ShapeDtypeStruct((B, S, D), F32),
        grid=(B, S // tm),
        in_specs=[row, _const_spec((1, D)), _const_spec((D, F)), _const_spec((F, D))],
        out_specs=row,
        compiler_params=_params(("arbitrary", "arbitrary")),
        name="mlp",
    )(h, norm_g.reshape(1, D), w1.astype(BF16), w2.astype(BF16))


def kernel(x, mem, norm_mix_g, w_in, fox_b_f, fox_q_norm_g, fox_k_norm_g, gla_w_gate2, gla_b_gate,
           gla_out_norm_g, w_out, norm_xattn_g, norm_mem_g, xattn_wq, xattn_wkv, xattn_q_norm_g,
           xattn_k_norm_g, xattn_wo, norm_mlp_g, mlp_w1, mlp_w2):
    head_dim = fox_q_norm_g.shape[0]
    qa, ka, va, gq, gk, gv, gr, gl, c_edges = _in_proj(
        x, norm_mix_g, w_in, fox_b_f, fox_q_norm_g, fox_k_norm_g, gla_w_gate2, gla_b_gate,
        gla_out_norm_g.shape[0])
    gla = _gla(gq, gk, gv, gr, gl, gla_out_norm_g)
    qk_bound = (FOX_ROUNDING_SLACK * head_dim * LOG2E / math.sqrt(head_dim)
                * jnp.max(jnp.abs(fox_q_norm_g)) * jnp.max(jnp.abs(fox_k_norm_g)))
    fox = _fox(qa, ka, va, c_edges, qk_bound, head_dim)
    km, vm = _mem_kv(mem, norm_mem_g, xattn_wkv, xattn_k_norm_g)
    h = _mix(x, fox, gla, w_out, norm_xattn_g, xattn_wq, xattn_q_norm_g, km, vm, xattn_wo)
    return _mlp(h, norm_mlp_g, mlp_w1, mlp_w2)
```

```python
import functools
import math

import numpy as np
import jax
import jax.numpy as jnp
from jax import lax
from jax.experimental import pallas as pl
from jax.experimental.pallas import tpu as pltpu

F32 = jnp.float32
BF16 = jnp.bfloat16

EPS = 1e-6
LOG2E = 1.4426950408889634

GLA_CHUNK = 64
GLA_HEADS = 4
GLA_TAU = 16.0

LANES = 128
MXU_K = 256
VMEM_LIMIT_BYTES = 56 * 1024 * 1024

ROW_TILE = 512
WIDE_ROW_TILE = 1024
MIX_SPLITS = 4
FOX_TILE = 512
FOX_UNROLL = 16
FOX_DIAG_UNROLL = 6
FOX_FINISH_BLOCKS = 4
FOX_ZERO_LOG2 = 160.0
FOX_BOUND_SLACK = 2.0
FOX_ROUNDING_SLACK = 1.02
AUG0 = 64


def _const_spec(shape):
    return pl.BlockSpec(shape, lambda *_: (0,) * len(shape))


def _params(semantics, flags=None):
    return pltpu.CompilerParams(dimension_semantics=semantics,
                                vmem_limit_bytes=VMEM_LIMIT_BYTES, flags=flags)


def _split3(c):
    hi = c.astype(BF16)
    r = c - hi.astype(F32)
    mid = r.astype(BF16)
    lo = (r - mid.astype(F32)).astype(BF16)
    return hi, mid, lo


def _log_sigmoid(z):
    return -(jnp.maximum(-z, 0.0) + jnp.log1p(jnp.exp(-jnp.abs(z))))


def _dot(a, b):
    return jnp.dot(a, b, preferred_element_type=F32)


def _dot_nt(a, b):
    return lax.dot_general(a, b, (((1,), (1,)), ((), ())), preferred_element_type=F32)


def _dot_tn(a, b):
    return lax.dot_general(a, b, (((0,), (0,)), ((), ())), preferred_element_type=F32)


def _in_proj_kernel(x_ref, g_ref, w_ref, wg_ref, ones_ref, tri_ref, place_ref, qg_ref, kg_ref,
                    bf_ref, wg2_ref, bg_ref, rows_ref,
                    qa_ref, ka_ref, va_ref, gq_ref, gk_ref, gv_ref, gr_ref, gl_ref, ce_ref,
                    carry_ref, *, fox_w, gkw, gvw, n_fox, head_dim):
    @pl.when(pl.program_id(1) == 0)
    def _():
        carry_ref[...] = jnp.zeros_like(carry_ref)

    tm = x_ref.shape[0]
    xns = []
    for r in (slice(0, tm // 2), slice(tm // 2, tm)):
        x = x_ref[r, :]
        xns.append((x * lax.rsqrt(jnp.mean(x * x, axis=-1, keepdims=True) + EPS)
                    * g_ref[...]).astype(BF16))
    xn = jnp.concatenate(xns, axis=0)

    o_k = fox_w
    o_v = 2 * fox_w
    o_gq = 3 * fox_w
    o_gk = o_gq + gkw
    o_gv = o_gk + gkw
    o_gr = o_gv + gvw
    o_sm = o_gr + gvw

    def wcols(lo, width):
        if lo + width <= o_gq:
            return w_ref[:, lo:lo + width]
        return wg_ref[:, lo - o_gq:lo - o_gq + width]

    def proj(lo, width):
        return _dot(xn, wcols(lo, width))

    def head_norm(f, gain_row):
        f2 = (f * f).astype(BF16)
        ssq = jnp.concatenate([_dot(f2[:, c:c + MXU_K], ones_ref[...])
                               for c in range(0, fox_w, MXU_K)], axis=1)
        return f * lax.rsqrt(ssq * (1.0 / head_dim) + EPS) * gain_row

    def spread(f):
        return jnp.concatenate([f[:, (h // 2) * LANES:(h // 2 + 1) * LANES] for h in range(n_fox)],
                               axis=1)

    gr_sm = jnp.concatenate([_dot(h, wcols(o_gr, gvw + LANES)) for h in xns], axis=0)
    sm = gr_sm[:, gvw:]
    lane = lax.broadcasted_iota(jnp.int32, sm.shape, 1)
    lf = jnp.where(lane < n_fox, _log_sigmoid(sm + bf_ref[...]), 0.0)
    fq = proj(0, fox_w)
    fk = proj(o_k, fox_w)
    fq = head_norm(fq, qg_ref[...])
    fv = proj(o_v, fox_w)
    fk = head_norm(fk, kg_ref[...])

    def pack3(v):
        hi, mid, lo = _split3(v)
        return (hi.astype(F32) + pltpu.roll(mid.astype(F32), n_fox, axis=1)
                + pltpu.roll(lo.astype(F32), 2 * n_fox, axis=1))

    cs = _dot(tri_ref[...], pack3(lf).astype(BF16))
    cs = cs + pltpu.roll(cs, LANES - n_fox, axis=1) + pltpu.roll(cs, LANES - 2 * n_fox, axis=1)
    c = jnp.where(lane < n_fox, cs, 0.0) + carry_ref[...]
    carry_ref[...] = c[tm - 1:tm, :]
    c2 = c * LOG2E
    cpk = pack3(c2).astype(BF16)
    ce_ref[...] = jnp.concatenate([c2[0:1], c2[tm - 1:tm], jnp.zeros((6, LANES), F32)], axis=0)

    g_qk = proj(o_gq, 2 * gkw)
    gq_ref[...] = g_qk[:, :gkw].astype(BF16)
    gk_ref[...] = g_qk[:, gkw:].astype(BF16)
    gv_ref[...] = proj(o_gv, gvw).astype(BF16)
    gr_ref[...] = gr_sm[:, :gvw].astype(BF16)
    gate = _dot(sm.astype(BF16), wg2_ref[...]) + bg_ref[...]
    gl_ref[...] = _log_sigmoid(gate) * (1.0 / GLA_TAU)

    wide = lax.broadcasted_iota(jnp.int32, (1, n_fox * LANES), 1)
    rel = (wide & (LANES - 1)) ^ jnp.where((wide // LANES) % 2 == 1, 0, AUG0)
    is_head_dim = rel >= AUG0
    placed = _dot(cpk, place_ref[...])
    va_ref[...] = jnp.where(is_head_dim, spread(fv), rows_ref[0:1, :]).astype(BF16)
    qa_ref[...] = jnp.where(is_head_dim, spread(fq),
                            jnp.where(rel < 3, placed, rows_ref[1:2, :])).astype(BF16)
    ka_ref[...] = jnp.where(is_head_dim, spread(fk),
                            jnp.where(rel >= 3, placed, rows_ref[2:3, :])).astype(BF16)


def _in_proj(x, norm_g, w_in, fox_b_f, fox_q_norm_g, fox_k_norm_g, gla_w_gate2, gla_b_gate, gvw):
    B, S, D = x.shape
    n_fox = fox_b_f.shape[0]
    head_dim = fox_q_norm_g.shape[0]
    fox_w = n_fox * head_dim
    rank, gkw = gla_w_gate2.shape
    tm = min(ROW_TILE, S)
    assert S % tm == 0 and head_dim == LANES // 2 and n_fox % 2 == 0
    assert 3 * n_fox <= LANES and n_fox + rank <= LANES

    sizes = (fox_w, fox_w, fox_w, n_fox, gkw, gkw, gvw, rank, gvw)
    offs = np.concatenate([[0], np.cumsum(sizes)])
    w_bf = w_in.astype(BF16)
    wq, wk, wv, wf, wgq, wgk, wgv, wlr, wgr = [w_bf[:, offs[i]:offs[i + 1]] for i in range(9)]
    assert offs[3] == 3 * fox_w
    w_gla = jnp.concatenate([wgq, wgk, wgv, wgr, wf, wlr,
                             jnp.zeros((D, LANES - n_fox - rank), BF16)], axis=1)

    scale = (1.0 / math.sqrt(head_dim)) * LOG2E
    qg = (jnp.tile(fox_q_norm_g, n_fox) * scale).reshape(1, fox_w)
    kg = jnp.tile(fox_k_norm_g, n_fox).reshape(1, fox_w)
    bf = jnp.zeros((1, LANES), F32).at[0, :n_fox].set(fox_b_f)
    wg2 = jnp.zeros((LANES, gkw), F32).at[n_fox:n_fox + rank].set(gla_w_gate2).astype(BF16)
    bg = gla_b_gate.reshape(1, gkw)

    assert fox_w % MXU_K == 0 and MXU_K % head_dim == 0
    grp = np.arange(MXU_K) // head_dim
    ones_blk = jnp.asarray(grp[:, None] == grp[None, :], BF16)
    r = np.arange(tm)
    tri = jnp.asarray(r[None, :] <= r[:, None], BF16)
    place = np.zeros((LANES, n_fox * LANES), np.float32)
    const_rows = np.zeros((8, n_fox * LANES), np.float32)
    for h in range(n_fox):
        base = h * LANES + (AUG0 if h % 2 == 0 else 0)
        for part in range(3):
            place[part * n_fox + h, base + part] = 1.0
            place[part * n_fox + h, base + 3 + part] = -1.0
            const_rows[1, base + 3 + part] = 1.0
            const_rows[2, base + part] = 1.0
        const_rows[0, base] = 1.0
    place = jnp.asarray(place, BF16)
    const_rows = jnp.asarray(const_rows)

    row = lambda c: pl.BlockSpec((None, tm, c), lambda b, i: (b, i, 0))
    kern = functools.partial(_in_proj_kernel, fox_w=fox_w, gkw=gkw, gvw=gvw, n_fox=n_fox,
                             head_dim=head_dim)
    aug_w = n_fox * LANES
    out_shapes = [jax.ShapeDtypeStruct((B, S, aug_w), BF16)] * 3 + [
        jax.ShapeDtypeStruct((B, S, gkw), BF16), jax.ShapeDtypeStruct((B, S, gkw), BF16),
        jax.ShapeDtypeStruct((B, S, gvw), BF16), jax.ShapeDtypeStruct((B, S, gvw), BF16),
        jax.ShapeDtypeStruct((B, S, gkw), F32),
        jax.ShapeDtypeStruct((B, S // tm, 8, LANES), F32)]
    return pl.pallas_call(
        kern,
        out_shape=out_shapes,
        grid=(B, S // tm),
        in_specs=[row(D), _const_spec((1, D)), _const_spec((D, 3 * fox_w)),
                  _const_spec((D, w_gla.shape[1])),
                  _const_spec((MXU_K, MXU_K)), _const_spec((tm, tm)),
                  _const_spec((LANES, aug_w)),
                  _const_spec((1, fox_w)), _const_spec((1, fox_w)), _const_spec((1, LANES)),
                  _const_spec((LANES, gkw)), _const_spec((1, gkw)), _const_spec((8, aug_w))],
        out_specs=[row(aug_w), row(aug_w), row(aug_w), row(gkw), row(gkw), row(gvw), row(gvw),
                   row(gkw), pl.BlockSpec((None, None, 8, LANES), lambda b, i: (b, i, 0, 0))],
        scratch_shapes=[pltpu.VMEM((1, LANES), F32)],
        compiler_params=_params(("arbitrary", "arbitrary")),
        name="in_proj",
    )(x, norm_g.reshape(1, D), w_bf, w_gla, ones_blk, tri, place, qg, kg, bf, wg2, bg, const_rows)


def _gla_kernel(gq_ref, gk_ref, gv_ref, gr_ref, gl_ref, tri3_ref, gain_ref, out_ref,
                st_ref, o_ref, *, dk, dv):
    @pl.when(pl.program_id(1) == 0)
    def _():
        st_ref[...] = jnp.zeros_like(st_ref)

    T, kw = gl_ref.shape
    vw = gv_ref.shape[1]
    n_heads = kw // dk
    C = GLA_CHUNK

    n_chunks = T // C
    chunks = [slice(c * C, (c + 1) * C) for c in range(n_chunks)]

    hi, mid, lo = _split3(gl_ref[...])
    tri3 = tri3_ref[...]
    bcs = [_dot(tri3, jnp.concatenate([hi[r], mid[r], lo[r]], axis=0)) for r in chunks]
    b_last = [b[C - 1:C] for b in bcs]
    bc = jnp.concatenate(bcs, axis=0)
    k = gk_ref[...].astype(F32)
    q_dec = gq_ref[...].astype(F32) * (dk ** -0.5) * jnp.exp(bc)
    k_dec = (k * jnp.exp(-bc)).astype(BF16)

    klane = lax.broadcasted_iota(jnp.int32, (1, kw), 1) // dk
    row_h = lax.broadcasted_iota(jnp.int32, (n_heads * C, C), 0)
    col = lax.broadcasted_iota(jnp.int32, (n_heads * C, C), 1)
    tril = col <= (row_h & (C - 1))
    bd = (lax.broadcasted_iota(jnp.int32, (vw, kw), 0) // dv
          == lax.broadcasted_iota(jnp.int32, (vw, kw), 1) // dk)

    d_sts = [_dot_tn(gv_ref[r, :], (k[r] * jnp.exp(bl - b)).astype(BF16))
             for r, b, bl in zip(chunks, bcs, b_last)]
    st = st_ref[...]
    sts = []
    for d_st, bl in zip(d_sts, b_last):
        sts.append(st.astype(BF16))
        st = st * jnp.exp(bl) + jnp.where(bd, d_st, 0.0)
    st_ref[...] = st

    for r, st_c in zip(chunks, sts):
        qd = q_dec[r]
        lhs = jnp.concatenate([jnp.where(klane == h, qd, 0.0) for h in range(n_heads)],
                              axis=0).astype(BF16)
        a = jnp.where(tril, _dot_nt(lhs, k_dec[r]), 0.0).astype(BF16)
        v = gv_ref[r, :]
        o_intra = jnp.concatenate(
            [_dot(a[h * C:(h + 1) * C], v[:, h * dv:(h + 1) * dv]) for h in range(n_heads)],
            axis=1)
        o_ref[r, :] = o_intra + _dot_nt(qd.astype(BF16), st_c)

    o = o_ref[...]
    gr = gr_ref[...].astype(F32)
    gain = gain_ref[...]
    outs = []
    for h in range(n_heads):
        oh = o[:, h * dv:(h + 1) * dv]
        on = oh * lax.rsqrt(jnp.mean(oh * oh, axis=-1, keepdims=True) + EPS)
        outs.append(on * gain[:, h * dv:(h + 1) * dv])
    out_ref[...] = (jnp.concatenate(outs, axis=1) * (gr * jax.nn.sigmoid(gr))).astype(BF16)


def _gla(gq, gk, gv, gr, gl, out_norm_g):
    B, S, kw = gq.shape
    vw = gv.shape[2]
    dk, dv = kw // GLA_HEADS, vw // GLA_HEADS
    T = min(ROW_TILE, S)
    assert S % T == 0 and T % GLA_CHUNK == 0
    r = np.arange(GLA_CHUNK)
    tri3 = jnp.asarray(np.tile(r[None, :] <= r[:, None], (1, 3)), BF16)
    row = lambda c: pl.BlockSpec((None, T, c), lambda b, i: (b, i, 0))
    return pl.pallas_call(
        functools.partial(_gla_kernel, dk=dk, dv=dv),
        out_shape=jax.ShapeDtypeStruct((B, S, vw), BF16),
        grid=(B, S // T),
        in_specs=[row(kw), row(kw), row(vw), row(vw), row(kw),
                  _const_spec((GLA_CHUNK, 3 * GLA_CHUNK)), _const_spec((1, vw))],
        out_specs=row(vw),
        scratch_shapes=[pltpu.VMEM((vw, kw), F32), pltpu.VMEM((T, vw), F32)],
        compiler_params=_params(("arbitrary", "arbitrary")),
        name="gla",
    )(gq, gk, gv, gr, gl, tri3, out_norm_g.reshape(1, vw))


def _fox_kernel(tab_q, tab_k, count, q_ref, k_ref, v_ref, o_ref, *scratch, head_dim, nq):
    pair = pl.program_id(0) * pl.num_programs(1) + pl.program_id(1)
    for hh in range(2):
        lanes = pl.ds(hh * LANES, LANES)
        _fox_head(tab_q, tab_k, count, 2 * pair + hh, hh, q_ref.at[:, lanes], k_ref.at[:, lanes],
                  v_ref.at[:, lanes], o_ref, *scratch, head_dim=head_dim, nq=nq)


def _fox_head(tab_q, tab_k, count, head_idx, slot, q_ref, k_ref, v_ref, o_ref, m_all, acc_all,
              s0, s1, p0, p1, al0, al1, *, head_dim, nq):
    T = s0.shape[0]
    s_buf, p_buf, al_buf = (s0, s1), (p0, p1), (al0, al1)
    depth = 3

    def rows(blk):
        start = blk * T
        return pl.ds(start if isinstance(blk, int) else pl.multiple_of(start, T), T)

    m_all[rows(nq), :] = jnp.zeros((T, LANES), F32)
    acc_all[rows(nq), :] = jnp.zeros((T, LANES), F32)
    for ref in (s1, p0, al0):
        ref[...] = jnp.zeros_like(ref)

    half = T // 2

    def half_rows(blk, i):
        return pl.ds(pl.multiple_of(blk * T, T) + i * half, half)

    def scores(qi, kj, par, diag):
        qr = jnp.minimum(qi, nq - 1)
        if not diag:
            s_buf[par][...] = _dot_nt(q_ref[rows(qr), :], k_ref[rows(kj), :])
            return
        top = _dot_nt(q_ref[half_rows(qr, 0), :], k_ref[half_rows(kj, 0), :])
        bot = _dot_nt(q_ref[half_rows(qr, 1), :], k_ref[rows(kj), :])
        def causal(shape, offset):
            row = lax.broadcasted_iota(jnp.int32, shape, 0)
            col = lax.broadcasted_iota(jnp.int32, shape, 1)
            return col <= row + offset

        s_buf[par][:half, :half] = jnp.where(causal((half, half), 0), top, -jnp.inf)
        s_buf[par][:half, half:] = jnp.full((half, half), -jnp.inf, F32)
        s_buf[par][half:, :] = jnp.where(causal((half, T), half), bot, -jnp.inf)

    def softmax(qi, par, diag):
        m_cur = jnp.broadcast_to(jnp.max(s_buf[par][...], axis=-1, keepdims=True), (T, LANES))
        if diag:
            m_new = m_cur
        else:
            m_prev = m_all[rows(qi), :]
            m_new = jnp.maximum(m_prev, m_cur)
            al_buf[par][...] = jnp.exp2(m_prev - m_new)
        m_all[rows(qi), :] = m_new
        for g in range(T // LANES):
            cols = slice(g * LANES, (g + 1) * LANES)
            p_buf[par][:, cols] = jnp.exp2(s_buf[par][:, cols] - m_new).astype(BF16)

    def pv(qi, kj, par, diag):
        start = pl.multiple_of(kj * T, T)

        def p_dot_v(r0, r1, n_cols):
            parts = [_dot(p_buf[par][r0:r1, c:c + MXU_K], v_ref[pl.ds(start + c, MXU_K), :])
                     for c in range(0, n_cols, MXU_K)]
            return sum(parts[1:], parts[0])

        if diag:
            acc_all[half_rows(qi, 0), :] = p_dot_v(0, half, half)
            acc_all[half_rows(qi, 1), :] = p_dot_v(half, T, T)
        else:
            acc_all[rows(qi), :] = al_buf[par][...] * acc_all[rows(qi), :] + p_dot_v(0, T, T)

    def write_first_head(n):
        r = rows(jnp.minimum(n, nq - 1))
        acc = acc_all[r, :]
        lane = lax.broadcasted_iota(jnp.int32, acc.shape, 1)
        inv_l = 1.0 / acc[:, head_dim:head_dim + 1]
        out = jnp.where(lane < head_dim, acc * inv_l, 0.0)
        o_ref[r, :] = jnp.where(n < nq + depth - 1, out, o_ref[r, :].astype(F32)).astype(BF16)

    def step(c, n, blocks, diag):
        (q0, k0), (q1, _), (q2, k2) = blocks
        if diag and slot == 1:
            write_first_head(n)
        pv(q2, k2, c % 2, diag)
        scores(q0, k0, c % 2, diag)
        softmax(q1, (c - 1) % 2, diag)

    def run(n_blocks, block_at, diag, steps_per_trip):
        spare = (jnp.int32(nq), jnp.int32(0))

        def trip(u, carry):
            hist = [carry[0:2], carry[2:4]]
            for c in range(steps_per_trip):
                n = u * steps_per_trip + c
                nxt = block_at(n)
                step(c, n, [nxt] + hist, diag)
                hist = [nxt] + hist[:1]
            return tuple(x for h in hist for x in h)

        n_trips = (n_blocks + (depth - 2 + steps_per_trip)) // steps_per_trip
        lax.fori_loop(0, n_trips, trip, spare * 2)

    def diag_block(n):
        return jnp.minimum(n, nq), jnp.minimum(n, nq - 1)

    def lower_block(n):
        return tab_q[head_idx, n], tab_k[head_idx, n]

    run(nq, diag_block, True, FOX_DIAG_UNROLL)
    if nq > 1:
        run(count[head_idx], lower_block, False, FOX_UNROLL)

    if slot == 0:
        return

    per_trip = math.gcd(nq, FOX_FINISH_BLOCKS)

    def finish(u, carry):
        for i in range(per_trip):
            r = rows(u * per_trip + i)
            acc = acc_all[r, :]
            lane = lax.broadcasted_iota(jnp.int32, acc.shape, 1)
            inv_l = 1.0 / acc[:, 0:1]
            o_ref[r, :] = jnp.where(lane < head_dim, o_ref[r, :].astype(F32),
                                    acc * inv_l).astype(BF16)
        return carry

    lax.fori_loop(0, nq // per_trip, finish, 0)


def _fox_block_table(c_edges, qk_bound, n_heads, nq):
    B = c_edges.shape[0]
    c_first = c_edges[:, :, 0, :n_heads]
    c_last = c_edges[:, :, 1, :n_heads]
    gap = c_first[:, :, None, :] - c_last[:, None, :, :]
    qi = np.arange(nq)[:, None]
    kj = np.arange(nq)[None, :]
    lower = jnp.asarray(kj < qi)[None, :, :, None]
    keep = lower & (2.0 * qk_bound + gap + FOX_BOUND_SLACK > -FOX_ZERO_LOG2)
    keep = jnp.transpose(keep, (0, 3, 1, 2)).reshape(B * n_heads, nq * nq)
    count = jnp.sum(keep, axis=1).astype(jnp.int32)
    order = jnp.argsort(~keep, axis=1, stable=True).astype(jnp.int32)
    n_tab = -(-(nq * (nq - 1) // 2 + 2) // FOX_UNROLL) * FOX_UNROLL
    order = jnp.pad(order, ((0, 0), (0, max(0, n_tab - nq * nq))))[:, :n_tab]
    live = jnp.arange(n_tab)[None, :] < count[:, None]
    tab_q = jnp.where(live, order // nq, nq).astype(jnp.int32)
    tab_k = jnp.where(live, order % nq, 0).astype(jnp.int32)
    return tab_q, tab_k, count


def _fox(qa, ka, va, c_edges, qk_bound, head_dim):
    B, S, W = qa.shape
    H = W // LANES
    T = min(FOX_TILE, S)
    assert S % T == 0 and c_edges.shape[1] * T == S
    nq = S // T
    tab_q, tab_k, count = _fox_block_table(c_edges, qk_bound, H, nq)
    assert H % 2 == 0 and 2 * head_dim == LANES
    pair_in = pl.BlockSpec((None, S, 2 * LANES), lambda b, hp, *_: (b, 0, hp))
    pair_out = pl.BlockSpec((None, S, LANES), lambda b, hp, *_: (b, 0, hp))
    state = pltpu.VMEM(((nq + 1) * T, LANES), F32)
    return pl.pallas_call(
        functools.partial(_fox_kernel, head_dim=head_dim, nq=nq),
        out_shape=jax.ShapeDtypeStruct((B, S, H * head_dim), BF16),
        grid_spec=pltpu.PrefetchScalarGridSpec(
            num_scalar_prefetch=3,
            grid=(B, H // 2),
            in_specs=[pair_in, pair_in, pair_in],
            out_specs=pair_out,
            scratch_shapes=([state, state] + [pltpu.VMEM((T, T), F32)] * 2
                            + [pltpu.VMEM((T, T), BF16)] * 2
                            + [pltpu.VMEM((T, LANES), F32)] * 2)),
        compiler_params=_params(("arbitrary", "arbitrary")),
        name="fox",
    )(tab_q, tab_k, count, qa, ka, va)


def _mem_kv_kernel(mem_ref, g_ref, w_ref, kg_ref, k_ref, v_ref, *, xd):
    x = mem_ref[...]
    D = x.shape[1]
    xn = (x * lax.rsqrt(jnp.mean(x * x, axis=-1, keepdims=True) + EPS) * g_ref[...]).astype(BF16)
    kg = kg_ref[...]
    for h in range(D // xd):
        kh = _dot(xn, w_ref[:, h * xd:(h + 1) * xd])
        kn = kh * lax.rsqrt(jnp.mean(kh * kh, axis=-1, keepdims=True) + EPS) * kg
        k_ref[:, h * xd:(h + 1) * xd] = kn.astype(BF16)
    v_ref[...] = _dot(xn, w_ref[:, D:]).astype(BF16)


def _mem_kv(mem, norm_mem_g, wkv, k_norm_g):
    B, M, D = mem.shape
    xd = k_norm_g.shape[0]
    kg = (k_norm_g * (LOG2E / math.sqrt(xd))).reshape(1, xd)
    blk = pl.BlockSpec((None, M, D), lambda b: (b, 0, 0))
    return pl.pallas_call(
        functools.partial(_mem_kv_kernel, xd=xd),
        out_shape=[jax.ShapeDtypeStruct((B, M, D), BF16)] * 2,
        grid=(B,),
        in_specs=[blk, _const_spec((1, D)), _const_spec((D, 2 * D)), _const_spec((1, xd))],
        out_specs=[blk, blk],
        compiler_params=_params(("arbitrary",)),
        name="mem_kv",
    )(mem, norm_mem_g.reshape(1, D), wkv.astype(BF16), kg)


def _mix_kernel(x_ref, fox_ref, gla_ref, wof_ref, wog_ref, g_ref, wq_ref, qg_ref, km_ref, vm_ref,
                wo_ref, h_ref, *, xd):
    tm, D = x_ref.shape
    splits = [slice(i * tm // MIX_SPLITS, (i + 1) * tm // MIX_SPLITS) for i in range(MIX_SPLITS)]
    heads = [slice(hd * xd, (hd + 1) * xd) for hd in range(D // xd)]
    qg = qg_ref[...]

    def rms(v, gain):
        return v * lax.rsqrt(jnp.mean(v * v, axis=-1, keepdims=True) + EPS) * gain

    hs = [x_ref[r, :] + _dot(fox_ref[r, :], wof_ref[...]) + _dot(gla_ref[r, :], wog_ref[...])
          for r in splits]
    hns = [rms(h, g_ref[...]).astype(BF16) for h in hs]
    qs = [_dot(hn, wq_ref[...]) for hn in hns]
    qns = [[rms(q[:, sl], qg).astype(BF16) for sl in heads] for q in qs]
    ss = [[_dot_nt(qn, km_ref[:, sl]) for qn, sl in zip(row, heads)] for row in qns]
    ps = [[jnp.exp2(s - jnp.max(s, axis=-1, keepdims=True)) for s in row] for row in ss]
    os = [[(_dot(p.astype(BF16), vm_ref[:, sl]) * (1.0 / jnp.sum(p, axis=-1, keepdims=True))
            ).astype(BF16) for p, sl in zip(row, heads)] for row in ps]
    for r, h, o in zip(splits, hs, os):
        h_ref[r, :] = h + _dot(jnp.concatenate(o, axis=1), wo_ref[...])


def _mix(x, fox, gla, w_out, norm_g, wq, q_norm_g, km, vm, wo):
    B, S, D = x.shape
    fw = fox.shape[2]
    gw = gla.shape[2]
    M = km.shape[1]
    xd = q_norm_g.shape[0]
    tm = min(WIDE_ROW_TILE, S)
    assert S % tm == 0
    w_out_bf = w_out.astype(BF16)
    wof, wog = w_out_bf[:fw], w_out_bf[fw:]
    row = lambda c: pl.BlockSpec((None, tm, c), lambda b, i: (b, i, 0))
    mem = pl.BlockSpec((None, M, D), lambda b, i: (b, 0, 0))
    return pl.pallas_call(
        functools.partial(_mix_kernel, xd=xd),
        out_shape=jax.ShapeDtypeStruct((B, S, D), F32),
        grid=(B, S // tm),
        in_specs=[row(D), row(fw), row(gw), _const_spec((fw, D)), _const_spec((gw, D)),
                  _const_spec((1, D)), _const_spec((D, D)), _const_spec((1, xd)), mem, mem,
                  _const_spec((D, D))],
        out_specs=row(D),
        compiler_params=_params(("arbitrary", "arbitrary")),
        name="mix",
    )(x, fox, gla, wof, wog, norm_g.reshape(1, D), wq.astype(BF16), q_norm_g.reshape(1, xd),
      km, vm, wo.astype(BF16))


def _mlp_kernel(h_ref, g_ref, w1_ref, w2_ref, y_ref, *, slab):
    h = h_ref[...]
    hn = (h * lax.rsqrt(jnp.mean(h * h, axis=-1, keepdims=True) + EPS) * g_ref[...]).astype(BF16)
    n_slabs = w1_ref.shape[1] // slab
    y = h
    act = None
    for j in range(n_slabs + 1):
        nxt = None
        if j < n_slabs:
            u = jnp.maximum(_dot(hn, w1_ref[:, j * slab:(j + 1) * slab]), 0.0)
            nxt = (u * u).astype(BF16)
        if act is not None:
            y = y + _dot(act, w2_ref[(j - 1) * slab:j * slab, :])
        act = nxt
    y_ref[...] = y


def _mlp(h, norm_g, w1, w2):
    B, S, D = h.shape
    F = w1.shape[1]
    tm = min(WIDE_ROW_TILE, S)
    assert S % tm == 0
    row = pl.BlockSpec((None, tm, D), lambda b, i: (b, i, 0))
    return pl.pallas_call(
        functools.partial(_mlp_kernel, slab=min(F, 1024)),
        out_shape=jax.ShapeDtypeStruct((B, S, D), F32),
        grid=(B, S // tm),
        in_specs=[row, _const_spec((1, D)), _const_spec((D, F)), _const_spec((F, D))],
        out_specs=row,
        compiler_params=_params(("arbitrary", "arbitrary")),
        name="mlp",
    )(h, norm_g.reshape(1, D), w1.astype(BF16), w2.astype(BF16))


def kernel(x, mem, norm_mix_g, w_in, fox_b_f, fox_q_norm_g, fox_k_norm_g, gla_w_gate2, gla_b_gate,
           gla_out_norm_g, w_out, norm_xattn_g, norm_mem_g, xattn_wq, xattn_wkv, xattn_q_norm_g,
           xattn_k_norm_g, xattn_wo, norm_mlp_g, mlp_w1, mlp_w2):
    head_dim = fox_q_norm_g.shape[0]
    qa, ka, va, gq, gk, gv, gr, gl, c_edges = _in_proj(
        x, norm_mix_g, w_in, fox_b_f, fox_q_norm_g, fox_k_norm_g, gla_w_gate2, gla_b_gate,
        gla_out_norm_g.shape[0])
    gla = _gla(gq, gk, gv, gr, gl, gla_out_norm_g)
    qk_bound = (FOX_ROUNDING_SLACK * head_dim * LOG2E / math.sqrt(head_dim)
                * jnp.max(jnp.abs(fox_q_norm_g)) * jnp.max(jnp.abs(fox_k_norm_g)))
    fox = _fox(qa, ka, va, c_edges, qk_bound, head_dim)
    km, vm = _mem_kv(mem, norm_mem_g, xattn_wkv, xattn_k_norm_g)
    h = _mix(x, fox, gla, w_out, norm_xattn_g, xattn_wq, xattn_q_norm_g, km, vm, xattn_wo)
    return _mlp(h, norm_mlp_g, mlp_w1, mlp_w2)
```

```python
import functools
import math

import numpy as np
import jax
import jax.numpy as jnp
from jax import lax
from jax.experimental import pallas as pl
from jax.experimental.pallas import tpu as pltpu

F32 = jnp.float32
BF16 = jnp.bfloat16

EPS = 1e-6
LOG2E = 1.4426950408889634

GLA_CHUNK = 64
GLA_HEADS = 4
GLA_TAU = 16.0

LANES = 128
MXU_K = 256
VMEM_LIMIT_BYTES = 56 * 1024 * 1024

ROW_TILE = 512
WIDE_ROW_TILE = 1024
MIX_SPLITS = 4
FOX_TILE = 512
FOX_UNROLL = 16
FOX_DIAG_UNROLL = 6
FOX_FINISH_BLOCKS = 4
FOX_ZERO_LOG2 = 160.0
FOX_BOUND_SLACK = 2.0
FOX_ROUNDING_SLACK = 1.02
AUG0 = 64


def _const_spec(shape):
    return pl.BlockSpec(shape, lambda *_: (0,) * len(shape))


def _params(semantics, flags=None):
    return pltpu.CompilerParams(dimension_semantics=semantics,
                                vmem_limit_bytes=VMEM_LIMIT_BYTES, flags=flags)


def _split3(c):
    hi = c.astype(BF16)
    r = c - hi.astype(F32)
    mid = r.astype(BF16)
    lo = (r - mid.astype(F32)).astype(BF16)
    return hi, mid, lo


def _log_sigmoid(z):
    return -(jnp.maximum(-z, 0.0) + jnp.log1p(jnp.exp(-jnp.abs(z))))


def _dot(a, b):
    return jnp.dot(a, b, preferred_element_type=F32)


def _dot_nt(a, b):
    return lax.dot_general(a, b, (((1,), (1,)), ((), ())), preferred_element_type=F32)


def _dot_tn(a, b):
    return lax.dot_general(a, b, (((0,), (0,)), ((), ())), preferred_element_type=F32)


def _in_proj_kernel(x_ref, g_ref, w_ref, wg_ref, ones_ref, tri_ref, place_ref, qg_ref, kg_ref,
                    bf_ref, wg2_ref, bg_ref, rows_ref,
                    qa_ref, ka_ref, va_ref, gq_ref, gk_ref, gv_ref, gr_ref, gl_ref, ce_ref,
                    carry_ref, *, fox_w, gkw, gvw, n_fox, head_dim):
    @pl.when(pl.program_id(1) == 0)
    def _():
        carry_ref[...] = jnp.zeros_like(carry_ref)

    tm = x_ref.shape[0]
    xns = []
    for r in (slice(0, tm // 2), slice(tm // 2, tm)):
        x = x_ref[r, :]
        xns.append((x * lax.rsqrt(jnp.mean(x * x, axis=-1, keepdims=True) + EPS)
                    * g_ref[...]).astype(BF16))
    xn = jnp.concatenate(xns, axis=0)

    o_k = fox_w
    o_v = 2 * fox_w
    o_gq = 3 * fox_w
    o_gk = o_gq + gkw
    o_gv = o_gk + gkw
    o_gr = o_gv + gvw
    o_sm = o_gr + gvw

    def wcols(lo, width):
        if lo + width <= o_gq:
            return w_ref[:, lo:lo + width]
        return wg_ref[:, lo - o_gq:lo - o_gq + width]

    def proj(lo, width):
        return _dot(xn, wcols(lo, width))

    def head_norm(f, gain_row):
        f2 = (f * f).astype(BF16)
        ssq = jnp.concatenate([_dot(f2[:, c:c + MXU_K], ones_ref[...])
                               for c in range(0, fox_w, MXU_K)], axis=1)
        return f * lax.rsqrt(ssq * (1.0 / head_dim) + EPS) * gain_row

    def spread(f):
        return jnp.concatenate([f[:, (h // 2) * LANES:(h // 2 + 1) * LANES] for h in range(n_fox)],
                               axis=1)

    gr_sm = jnp.concatenate([_dot(h, wcols(o_gr, gvw + LANES)) for h in xns], axis=0)
    sm = gr_sm[:, gvw:]
    lane = lax.broadcasted_iota(jnp.int32, sm.shape, 1)
    lf = jnp.where(lane < n_fox, _log_sigmoid(sm + bf_ref[...]), 0.0)
    fq = proj(0, fox_w)
    fk = proj(o_k, fox_w)
    fq = head_norm(fq, qg_ref[...])
    fv = proj(o_v, fox_w)
    fk = head_norm(fk, kg_ref[...])

    def pack3(v):
        hi, mid, lo = _split3(v)
        return (hi.astype(F32) + pltpu.roll(mid.astype(F32), n_fox, axis=1)
                + pltpu.roll(lo.astype(F32), 2 * n_fox, axis=1))

    cs = _dot(tri_ref[...], pack3(lf).astype(BF16))
    cs = cs + pltpu.roll(cs, LANES - n_fox, axis=1) + pltpu.roll(cs, LANES - 2 * n_fox, axis=1)
    c = jnp.where(lane < n_fox, cs, 0.0) + carry_ref[...]
    carry_ref[...] = c[tm - 1:tm, :]
    c2 = c * LOG2E
    cpk = pack3(c2).astype(BF16)
    ce_ref[...] = jnp.concatenate([c2[0:1], c2[tm - 1:tm], jnp.zeros((6, LANES), F32)], axis=0)

    g_qk = proj(o_gq, 2 * gkw)
    gq_ref[...] = g_qk[:, :gkw].astype(BF16)
    gk_ref[...] = g_qk[:, gkw:].astype(BF16)
    gv_ref[...] = proj(o_gv, gvw).astype(BF16)
    gr_ref[...] = gr_sm[:, :gvw].astype(BF16)
    gate = _dot(sm.astype(BF16), wg2_ref[...]) + bg_ref[...]
    gl_ref[...] = _log_sigmoid(gate) * (1.0 / GLA_TAU)

    wide = lax.broadcasted_iota(jnp.int32, (1, n_fox * LANES), 1)
    rel = (wide & (LANES - 1)) ^ jnp.where((wide // LANES) % 2 == 1, 0, AUG0)
    is_head_dim = rel >= AUG0
    placed = _dot(cpk, place_ref[...])
    va_ref[...] = jnp.where(is_head_dim, spread(fv), rows_ref[0:1, :]).astype(BF16)
    qa_ref[...] = jnp.where(is_head_dim, spread(fq),
                            jnp.where(rel < 3, placed, rows_ref[1:2, :])).astype(BF16)
    ka_ref[...] = jnp.where(is_head_dim, spread(fk),
                            jnp.where(rel >= 3, placed, rows_ref[2:3, :])).astype(BF16)


def _in_proj(x, norm_g, w_in, fox_b_f, fox_q_norm_g, fox_k_norm_g, gla_w_gate2, gla_b_gate, gvw):
    B, S, D = x.shape
    n_fox = fox_b_f.shape[0]
    head_dim = fox_q_norm_g.shape[0]
    fox_w = n_fox * head_dim
    rank, gkw = gla_w_gate2.shape
    tm = min(ROW_TILE, S)
    assert S % tm == 0 and head_dim == LANES // 2 and n_fox % 2 == 0
    assert 3 * n_fox <= LANES and n_fox + rank <= LANES

    sizes = (fox_w, fox_w, fox_w, n_fox, gkw, gkw, gvw, rank, gvw)
    offs = np.concatenate([[0], np.cumsum(sizes)])
    w_bf = w_in.astype(BF16)
    wq, wk, wv, wf, wgq, wgk, wgv, wlr, wgr = [w_bf[:, offs[i]:offs[i + 1]] for i in range(9)]
    assert offs[3] == 3 * fox_w
    w_gla = jnp.concatenate([wgq, wgk, wgv, wgr, wf, wlr,
                             jnp.zeros((D, LANES - n_fox - rank), BF16)], axis=1)

    scale = (1.0 / math.sqrt(head_dim)) * LOG2E
    qg = (jnp.tile(fox_q_norm_g, n_fox) * scale).reshape(1, fox_w)
    kg = jnp.tile(fox_k_norm_g, n_fox).reshape(1, fox_w)
    bf = jnp.zeros((1, LANES), F32).at[0, :n_fox].set(fox_b_f)
    wg2 = jnp.zeros((LANES, gkw), F32).at[n_fox:n_fox + rank].set(gla_w_gate2).astype(BF16)
    bg = gla_b_gate.reshape(1, gkw)

    assert fox_w % MXU_K == 0 and MXU_K % head_dim == 0
    grp = np.arange(MXU_K) // head_dim
    ones_blk = jnp.asarray(grp[:, None] == grp[None, :], BF16)
    r = np.arange(tm)
    tri = jnp.asarray(r[None, :] <= r[:, None], BF16)
    place = np.zeros((LANES, n_fox * LANES), np.float32)
    const_rows = np.zeros((8, n_fox * LANES), np.float32)
    for h in range(n_fox):
        base = h * LANES + (AUG0 if h % 2 == 0 else 0)
        for part in range(3):
            place[part * n_fox + h, base + part] = 1.0
            place[part * n_fox + h, base + 3 + part] = -1.0
            const_rows[1, base + 3 + part] = 1.0
            const_rows[2, base + part] = 1.0
        const_rows[0, base] = 1.0
    place = jnp.asarray(place, BF16)
    const_rows = jnp.asarray(const_rows)

    row = lambda c: pl.BlockSpec((None, tm, c), lambda b, i: (b, i, 0))
    kern = functools.partial(_in_proj_kernel, fox_w=fox_w, gkw=gkw, gvw=gvw, n_fox=n_fox,
                             head_dim=head_dim)
    aug_w = n_fox * LANES
    out_shapes = [jax.ShapeDtypeStruct((B, S, aug_w), BF16)] * 3 + [
        jax.ShapeDtypeStruct((B, S, gkw), BF16), jax.ShapeDtypeStruct((B, S, gkw), BF16),
        jax.ShapeDtypeStruct((B, S, gvw), BF16), jax.ShapeDtypeStruct((B, S, gvw), BF16),
        jax.ShapeDtypeStruct((B, S, gkw), F32),
        jax.ShapeDtypeStruct((B, S // tm, 8, LANES), F32)]
    return pl.pallas_call(
        kern,
        out_shape=out_shapes,
        grid=(B, S // tm),
        in_specs=[row(D), _const_spec((1, D)), _const_spec((D, 3 * fox_w)),
                  _const_spec((D, w_gla.shape[1])),
                  _const_spec((MXU_K, MXU_K)), _const_spec((tm, tm)),
                  _const_spec((LANES, aug_w)),
                  _const_spec((1, fox_w)), _const_spec((1, fox_w)), _const_spec((1, LANES)),
                  _const_spec((LANES, gkw)), _const_spec((1, gkw)), _const_spec((8, aug_w))],
        out_specs=[row(aug_w), row(aug_w), row(aug_w), row(gkw), row(gkw), row(gvw), row(gvw),
                   row(gkw), pl.BlockSpec((None, None, 8, LANES), lambda b, i: (b, i, 0, 0))],
        scratch_shapes=[pltpu.VMEM((1, LANES), F32)],
        compiler_params=_params(("arbitrary", "arbitrary")),
        name="in_proj",
    )(x, norm_g.reshape(1, D), w_bf[:, :3 * fox_w], w_gla, ones_blk, tri, place, qg, kg, bf, wg2, bg, const_rows)


def _gla_kernel(gq_ref, gk_ref, gv_ref, gr_ref, gl_ref, tri3_ref, gain_ref, out_ref,
                st_ref, o_ref, *, dk, dv):
    @pl.when(pl.program_id(1) == 0)
    def _():
        st_ref[...] = jnp.zeros_like(st_ref)

    T, kw = gl_ref.shape
    vw = gv_ref.shape[1]
    n_heads = kw // dk
    C = GLA_CHUNK

    n_chunks = T // C
    chunks = [slice(c * C, (c + 1) * C) for c in range(n_chunks)]

    hi, mid, lo = _split3(gl_ref[...])
    tri3 = tri3_ref[...]
    bcs = [_dot(tri3, jnp.concatenate([hi[r], mid[r], lo[r]], axis=0)) for r in chunks]
    b_last = [b[C - 1:C] for b in bcs]
    bc = jnp.concatenate(bcs, axis=0)
    k = gk_ref[...].astype(F32)
    q_dec = gq_ref[...].astype(F32) * (dk ** -0.5) * jnp.exp(bc)
    k_dec = (k * jnp.exp(-bc)).astype(BF16)

    klane = lax.broadcasted_iota(jnp.int32, (1, kw), 1) // dk
    row_h = lax.broadcasted_iota(jnp.int32, (n_heads * C, C), 0)
    col = lax.broadcasted_iota(jnp.int32, (n_heads * C, C), 1)
    tril = col <= (row_h & (C - 1))
    bd = (lax.broadcasted_iota(jnp.int32, (vw, kw), 0) // dv
          == lax.broadcasted_iota(jnp.int32, (vw, kw), 1) // dk)

    d_sts = [_dot_tn(gv_ref[r, :], (k[r] * jnp.exp(bl - b)).astype(BF16))
             for r, b, bl in zip(chunks, bcs, b_last)]
    st = st_ref[...]
    sts = []
    for d_st, bl in zip(d_sts, b_last):
        sts.append(st.astype(BF16))
        st = st * jnp.exp(bl) + jnp.where(bd, d_st, 0.0)
    st_ref[...] = st

    for r, st_c in zip(chunks, sts):
        qd = q_dec[r]
        lhs = jnp.concatenate([jnp.where(klane == h, qd, 0.0) for h in range(n_heads)],
                              axis=0).astype(BF16)
        a = jnp.where(tril, _dot_nt(lhs, k_dec[r]), 0.0).astype(BF16)
        v = gv_ref[r, :]
        o_intra = jnp.concatenate(
            [_dot(a[h * C:(h + 1) * C], v[:, h * dv:(h + 1) * dv]) for h in range(n_heads)],
            axis=1)
        o_ref[r, :] = o_intra + _dot_nt(qd.astype(BF16), st_c)

    o = o_ref[...]
    gr = gr_ref[...].astype(F32)
    gain = gain_ref[...]
    outs = []
    for h in range(n_heads):
        oh = o[:, h * dv:(h + 1) * dv]
        on = oh * lax.rsqrt(jnp.mean(oh * oh, axis=-1, keepdims=True) + EPS)
        outs.append(on * gain[:, h * dv:(h + 1) * dv])
    out_ref[...] = (jnp.concatenate(outs, axis=1) * (gr * jax.nn.sigmoid(gr))).astype(BF16)


def _gla(gq, gk, gv, gr, gl, out_norm_g):
    B, S, kw = gq.shape
    vw = gv.shape[2]
    dk, dv = kw // GLA_HEADS, vw // GLA_HEADS
    T = min(WIDE_ROW_TILE, S)
    assert S % T == 0 and T % GLA_CHUNK == 0
    r = np.arange(GLA_CHUNK)
    tri3 = jnp.asarray(np.tile(r[None, :] <= r[:, None], (1, 3)), BF16)
    row = lambda c: pl.BlockSpec((None, T, c), lambda b, i: (b, i, 0))
    return pl.pallas_call(
        functools.partial(_gla_kernel, dk=dk, dv=dv),
        out_shape=jax.ShapeDtypeStruct((B, S, vw), BF16),
        grid=(B, S // T),
        in_specs=[row(kw), row(kw), row(vw), row(vw), row(kw),
                  _const_spec((GLA_CHUNK, 3 * GLA_CHUNK)), _const_spec((1, vw))],
        out_specs=row(vw),
        scratch_shapes=[pltpu.VMEM((vw, kw), F32), pltpu.VMEM((T, vw), F32)],
        compiler_params=_params(("arbitrary", "arbitrary")),
        name="gla",
    )(gq, gk, gv, gr, gl, tri3, out_norm_g.reshape(1, vw))


def _fox_kernel(tab_q, tab_k, count, q_ref, k_ref, v_ref, o_ref, *scratch, head_dim, nq):
    pair = pl.program_id(0) * pl.num_programs(1) + pl.program_id(1)
    for hh in range(2):
        lanes = pl.ds(hh * LANES, LANES)
        _fox_head(tab_q, tab_k, count, 2 * pair + hh, hh, q_ref.at[:, lanes], k_ref.at[:, lanes],
                  v_ref.at[:, lanes], o_ref, *scratch, head_dim=head_dim, nq=nq)


def _fox_head(tab_q, tab_k, count, head_idx, slot, q_ref, k_ref, v_ref, o_ref, m_all, acc_all,
              s0, s1, p0, p1, al0, al1, *, head_dim, nq):
    T = s0.shape[0]
    s_buf, p_buf, al_buf = (s0, s1), (p0, p1), (al0, al1)
    depth = 3

    def rows(blk):
        start = blk * T
        return pl.ds(start if isinstance(blk, int) else pl.multiple_of(start, T), T)

    m_all[rows(nq), :] = jnp.zeros((T, LANES), F32)
    acc_all[rows(nq), :] = jnp.zeros((T, LANES), F32)
    for ref in (s1, p0, al0):
        ref[...] = jnp.zeros_like(ref)

    half = T // 2

    def half_rows(blk, i):
        return pl.ds(pl.multiple_of(blk * T, T) + i * half, half)

    def scores(qi, kj, par, diag):
        qr = jnp.minimum(qi, nq - 1)
        if not diag:
            s_buf[par][...] = _dot_nt(q_ref[rows(qr), :], k_ref[rows(kj), :])
            return
        top = _dot_nt(q_ref[half_rows(qr, 0), :], k_ref[half_rows(kj, 0), :])
        bot = _dot_nt(q_ref[half_rows(qr, 1), :], k_ref[rows(kj), :])
        def causal(shape, offset):
            row = lax.broadcasted_iota(jnp.int32, shape, 0)
            col = lax.broadcasted_iota(jnp.int32, shape, 1)
            return col <= row + offset

        s_buf[par][:half, :half] = jnp.where(causal((half, half), 0), top, -jnp.inf)
        s_buf[par][:half, half:] = jnp.full((half, half), -jnp.inf, F32)
        s_buf[par][half:, :] = jnp.where(causal((half, T), half), bot, -jnp.inf)

    def softmax(qi, par, diag):
        m_cur = jnp.broadcast_to(jnp.max(s_buf[par][...], axis=-1, keepdims=True), (T, LANES))
        if diag:
            m_new = m_cur
        else:
            m_prev = m_all[rows(qi), :]
            m_new = jnp.maximum(m_prev, m_cur)
            al_buf[par][...] = jnp.exp2(m_prev - m_new)
        m_all[rows(qi), :] = m_new
        for g in range(T // LANES):
            cols = slice(g * LANES, (g + 1) * LANES)
            p_buf[par][:, cols] = jnp.exp2(s_buf[par][:, cols] - m_new).astype(BF16)

    def pv(qi, kj, par, diag):
        start = pl.multiple_of(kj * T, T)

        def p_dot_v(r0, r1, n_cols):
            parts = [_dot(p_buf[par][r0:r1, c:c + MXU_K], v_ref[pl.ds(start + c, MXU_K), :])
                     for c in range(0, n_cols, MXU_K)]
            return sum(parts[1:], parts[0])

        if diag:
            acc_all[half_rows(qi, 0), :] = p_dot_v(0, half, half)
            acc_all[half_rows(qi, 1), :] = p_dot_v(half, T, T)
        else:
            acc_all[rows(qi), :] = al_buf[par][...] * acc_all[rows(qi), :] + p_dot_v(0, T, T)

    def write_first_head(n):
        r = rows(jnp.minimum(n, nq - 1))
        acc = acc_all[r, :]
        lane = lax.broadcasted_iota(jnp.int32, acc.shape, 1)
        inv_l = 1.0 / acc[:, head_dim:head_dim + 1]
        out = jnp.where(lane < head_dim, acc * inv_l, 0.0)
        o_ref[r, :] = jnp.where(n < nq + depth - 1, out, o_ref[r, :].astype(F32)).astype(BF16)

    def step(c, n, blocks, diag):
        (q0, k0), (q1, _), (q2, k2) = blocks
        if diag and slot == 1:
            write_first_head(n)
        pv(q2, k2, c % 2, diag)
        scores(q0, k0, c % 2, diag)
        softmax(q1, (c - 1) % 2, diag)

    def run(n_blocks, block_at, diag, steps_per_trip):
        spare = (jnp.int32(nq), jnp.int32(0))

        def trip(u, carry):
            hist = [carry[0:2], carry[2:4]]
            for c in range(steps_per_trip):
                n = u * steps_per_trip + c
                nxt = block_at(n)
                step(c, n, [nxt] + hist, diag)
                hist = [nxt] + hist[:1]
            return tuple(x for h in hist for x in h)

        n_trips = (n_blocks + (depth - 2 + steps_per_trip)) // steps_per_trip
        lax.fori_loop(0, n_trips, trip, spare * 2)

    def diag_block(n):
        return jnp.minimum(n, nq), jnp.minimum(n, nq - 1)

    def lower_block(n):
        return tab_q[head_idx, n], tab_k[head_idx, n]

    run(nq, diag_block, True, FOX_DIAG_UNROLL)
    if nq > 1:
        run(count[head_idx], lower_block, False, FOX_UNROLL)

    if slot == 0:
        return

    per_trip = math.gcd(nq, FOX_FINISH_BLOCKS)

    def finish(u, carry):
        for i in range(per_trip):
            r = rows(u * per_trip + i)
            acc = acc_all[r, :]
            lane = lax.broadcasted_iota(jnp.int32, acc.shape, 1)
            inv_l = 1.0 / acc[:, 0:1]
            o_ref[r, :] = jnp.where(lane < head_dim, o_ref[r, :].astype(F32),
                                    acc * inv_l).astype(BF16)
        return carry

    lax.fori_loop(0, nq // per_trip, finish, 0)


def _fox_block_table(c_edges, qk_bound, n_heads, nq):
    B = c_edges.shape[0]
    c_first = c_edges[:, :, 0, :n_heads]
    c_last = c_edges[:, :, 1, :n_heads]
    gap = c_first[:, :, None, :] - c_last[:, None, :, :]
    qi = np.arange(nq)[:, None]
    kj = np.arange(nq)[None, :]
    lower = jnp.asarray(kj < qi)[None, :, :, None]
    keep = lower & (2.0 * qk_bound + gap + FOX_BOUND_SLACK > -FOX_ZERO_LOG2)
    keep = jnp.transpose(keep, (0, 3, 1, 2)).reshape(B * n_heads, nq * nq)
    count = jnp.sum(keep, axis=1).astype(jnp.int32)
    order = jnp.argsort(~keep, axis=1, stable=True).astype(jnp.int32)
    n_tab = -(-(nq * (nq - 1) // 2 + 2) // FOX_UNROLL) * FOX_UNROLL
    order = jnp.pad(order, ((0, 0), (0, max(0, n_tab - nq * nq))))[:, :n_tab]
    live = jnp.arange(n_tab)[None, :] < count[:, None]
    tab_q = jnp.where(live, order // nq, nq).astype(jnp.int32)
    tab_k = jnp.where(live, order % nq, 0).astype(jnp.int32)
    return tab_q, tab_k, count


def _fox(qa, ka, va, c_edges, qk_bound, head_dim):
    B, S, W = qa.shape
    H = W // LANES
    T = min(FOX_TILE, S)
    assert S % T == 0 and c_edges.shape[1] * T == S
    nq = S // T
    tab_q, tab_k, count = _fox_block_table(c_edges, qk_bound, H, nq)
    assert H % 2 == 0 and 2 * head_dim == LANES
    pair_in = pl.BlockSpec((None, S, 2 * LANES), lambda b, hp, *_: (b, 0, hp))
    pair_out = pl.BlockSpec((None, S, LANES), lambda b, hp, *_: (b, 0, hp))
    state = pltpu.VMEM(((nq + 1) * T, LANES), F32)
    return pl.pallas_call(
        functools.partial(_fox_kernel, head_dim=head_dim, nq=nq),
        out_shape=jax.ShapeDtypeStruct((B, S, H * head_dim), BF16),
        grid_spec=pltpu.PrefetchScalarGridSpec(
            num_scalar_prefetch=3,
            grid=(B, H // 2),
            in_specs=[pair_in, pair_in, pair_in],
            out_specs=pair_out,
            scratch_shapes=([state, state] + [pltpu.VMEM((T, T), F32)] * 2
                            + [pltpu.VMEM((T, T), BF16)] * 2
                            + [pltpu.VMEM((T, LANES), F32)] * 2)),
        compiler_params=_params(("arbitrary", "arbitrary")),
        name="fox",
    )(tab_q, tab_k, count, qa, ka, va)


def _mem_kv_kernel(mem_ref, g_ref, w_ref, kg_ref, k_ref, v_ref, *, xd):
    x = mem_ref[...]
    D = x.shape[1]
    xn = (x * lax.rsqrt(jnp.mean(x * x, axis=-1, keepdims=True) + EPS) * g_ref[...]).astype(BF16)
    kg = kg_ref[...]
    for h in range(D // xd):
        kh = _dot(xn, w_ref[:, h * xd:(h + 1) * xd])
        kn = kh * lax.rsqrt(jnp.mean(kh * kh, axis=-1, keepdims=True) + EPS) * kg
        k_ref[:, h * xd:(h + 1) * xd] = kn.astype(BF16)
    v_ref[...] = _dot(xn, w_ref[:, D:]).astype(BF16)


def _mem_kv(mem, norm_mem_g, wkv, k_norm_g):
    B, M, D = mem.shape
    xd = k_norm_g.shape[0]
    kg = (k_norm_g * (LOG2E / math.sqrt(xd))).reshape(1, xd)
    blk = pl.BlockSpec((None, M, D), lambda b: (b, 0, 0))
    return pl.pallas_call(
        functools.partial(_mem_kv_kernel, xd=xd),
        out_shape=[jax.ShapeDtypeStruct((B, M, D), BF16)] * 2,
        grid=(B,),
        in_specs=[blk, _const_spec((1, D)), _const_spec((D, 2 * D)), _const_spec((1, xd))],
        out_specs=[blk, blk],
        compiler_params=_params(("arbitrary",)),
        name="mem_kv",
    )(mem, norm_mem_g.reshape(1, D), wkv.astype(BF16), kg)


def _mix_kernel(x_ref, fox_ref, gla_ref, wof_ref, wog_ref, g_ref, wq_ref, qg_ref, km_ref, vm_ref,
                wo_ref, h_ref, *, xd):
    tm, D = x_ref.shape
    splits = [slice(i * tm // MIX_SPLITS, (i + 1) * tm // MIX_SPLITS) for i in range(MIX_SPLITS)]
    heads = [slice(hd * xd, (hd + 1) * xd) for hd in range(D // xd)]
    qg = qg_ref[...]

    def rms(v, gain):
        return v * lax.rsqrt(jnp.mean(v * v, axis=-1, keepdims=True) + EPS) * gain

    hs = [x_ref[r, :] + _dot(fox_ref[r, :], wof_ref[...]) + _dot(gla_ref[r, :], wog_ref[...])
          for r in splits]
    hns = [rms(h, g_ref[...]).astype(BF16) for h in hs]
    qs = [_dot(hn, wq_ref[...]) for hn in hns]
    qns = [[rms(q[:, sl], qg).astype(BF16) for sl in heads] for q in qs]
    ss = [[_dot_nt(qn, km_ref[:, sl]) for qn, sl in zip(row, heads)] for row in qns]
    ps = [[jnp.exp2(s - jnp.max(s, axis=-1, keepdims=True)) for s in row] for row in ss]
    os = [[(_dot(p.astype(BF16), vm_ref[:, sl]) * (1.0 / jnp.sum(p, axis=-1, keepdims=True))
            ).astype(BF16) for p, sl in zip(row, heads)] for row in ps]
    for r, h, o in zip(splits, hs, os):
        h_ref[r, :] = h + _dot(jnp.concatenate(o, axis=1), wo_ref[...])


def _mix(x, fox, gla, w_out, norm_g, wq, q_norm_g, km, vm, wo):
    B, S, D = x.shape
    fw = fox.shape[2]
    gw = gla.shape[2]
    M = km.shape[1]
    xd = q_norm_g.shape[0]
    tm = min(WIDE_ROW_TILE, S)
    assert S % tm == 0
    w_out_bf = w_out.astype(BF16)
    wof, wog = w_out_bf[:fw], w_out_bf[fw:]
    row = lambda c: pl.BlockSpec((None, tm, c), lambda b, i: (b, i, 0))
    mem = pl.BlockSpec((None, M, D), lambda b, i: (b, 0, 0))
    return pl.pallas_call(
        functools.partial(_mix_kernel, xd=xd),
        out_shape=jax.ShapeDtypeStruct((B, S, D), F32),
        grid=(B, S // tm),
        in_specs=[row(D), row(fw), row(gw), _const_spec((fw, D)), _const_spec((gw, D)),
                  _const_spec((1, D)), _const_spec((D, D)), _const_spec((1, xd)), mem, mem,
                  _const_spec((D, D))],
        out_specs=row(D),
        compiler_params=_params(("arbitrary", "arbitrary")),
        name="mix",
    )(x, fox, gla, wof, wog, norm_g.reshape(1, D), wq.astype(BF16), q_norm_g.reshape(1, xd),
      km, vm, wo.astype(BF16))


def _mlp_kernel(h_ref, g_ref, w1_ref, w2_ref, y_ref, *, slab):
    h = h_ref[...]
    hn = (h * lax.rsqrt(jnp.mean(h * h, axis=-1, keepdims=True) + EPS) * g_ref[...]).astype(BF16)
    n_slabs = w1_ref.shape[1] // slab
    y = h
    act = None
    for j in range(n_slabs + 1):
        nxt = None
        if j < n_slabs:
            u = jnp.maximum(_dot(hn, w1_ref[:, j * slab:(j + 1) * slab]), 0.0)
            nxt = (u * u).astype(BF16)
        if act is not None:
            y = y + _dot(act, w2_ref[(j - 1) * slab:j * slab, :])
        act = nxt
    y_ref[...] = y


def _mlp(h, norm_g, w1, w2):
    B, S, D = h.shape
    F = w1.shape[1]
    tm = min(WIDE_ROW_TILE, S)
    assert S % tm == 0
    row = pl.BlockSpec((None, tm, D), lambda b, i: (b, i, 0))
    return pl.pallas_call(
        functools.partial(_mlp_kernel, slab=min(F, 1024)),
        out_shape=jax.ShapeDtypeStruct((B, S, D), F32),
        grid=(B, S // tm),
        in_specs=[row, _const_spec((1, D)), _const_spec((D, F)), _const_spec((F, D))],
        out_specs=row,
        compiler_params=_params(("arbitrary", "arbitrary")),
        name="mlp",
    )(h, norm_g.reshape(1, D), w1.astype(BF16), w2.astype(BF16))


def kernel(x, mem, norm_mix_g, w_in, fox_b_f, fox_q_norm_g, fox_k_norm_g, gla_w_gate2, gla_b_gate,
           gla_out_norm_g, w_out, norm_xattn_g, norm_mem_g, xattn_wq, xattn_wkv, xattn_q_norm_g,
           xattn_k_norm_g, xattn_wo, norm_mlp_g, mlp_w1, mlp_w2):
    head_dim = fox_q_norm_g.shape[0]
    qa, ka, va, gq, gk, gv, gr, gl, c_edges = _in_proj(
        x, norm_mix_g, w_in, fox_b_f, fox_q_norm_g, fox_k_norm_g, gla_w_gate2, gla_b_gate,
        gla_out_norm_g.shape[0])
    gla = _gla(gq, gk, gv, gr, gl, gla_out_norm_g)
    qk_bound = (FOX_ROUNDING_SLACK * head_dim * LOG2E / math.sqrt(head_dim)
                * jnp.max(jnp.abs(fox_q_norm_g)) * jnp.max(jnp.abs(fox_k_norm_g)))
    fox = _fox(qa, ka, va, c_edges, qk_bound, head_dim)
    km, vm = _mem_kv(mem, norm_mem_g, xattn_wkv, xattn_k_norm_g)
    h = _mix(x, fox, gla, w_out, norm_xattn_g, xattn_wq, xattn_q_norm_g, km, vm, xattn_wo)
    return _mlp(h, norm_mlp_g, mlp_w1, mlp_w2)
```

```python
import functools
import math

import numpy as np
import jax
import jax.numpy as jnp
from jax import lax
from jax.experimental import pallas as pl
from jax.experimental.pallas import tpu as pltpu

F32 = jnp.float32
BF16 = jnp.bfloat16

EPS = 1e-6
LOG2E = 1.4426950408889634

GLA_CHUNK = 64
GLA_HEADS = 4
GLA_TAU = 16.0

LANES = 128
MXU_K = 256
VMEM_LIMIT_BYTES = 56 * 1024 * 1024

ROW_TILE = 512
WIDE_ROW_TILE = 1024
MIX_SPLITS = 4
FOX_TILE = 512
FOX_UNROLL = 16
FOX_FINISH_BLOCKS = 4
FOX_ZERO_LOG2 = 160.0
FOX_BOUND_SLACK = 2.0
FOX_ROUNDING_SLACK = 1.02
AUG0 = 64


def _const_spec(shape):
    return pl.BlockSpec(shape, lambda *_: (0,) * len(shape))


def _params(semantics, flags=None):
    return pltpu.CompilerParams(dimension_semantics=semantics,
                                vmem_limit_bytes=VMEM_LIMIT_BYTES, flags=flags)


def _split3(c):
    hi = c.astype(BF16)
    r = c - hi.astype(F32)
    mid = r.astype(BF16)
    lo = (r - mid.astype(F32)).astype(BF16)
    return hi, mid, lo


def _log_sigmoid(z):
    return -(jnp.maximum(-z, 0.0) + jnp.log1p(jnp.exp(-jnp.abs(z))))


def _dot(a, b):
    return jnp.dot(a, b, preferred_element_type=F32)


def _dot_nt(a, b):
    return lax.dot_general(a, b, (((1,), (1,)), ((), ())), preferred_element_type=F32)


def _dot_tn(a, b):
    return lax.dot_general(a, b, (((0,), (0,)), ((), ())), preferred_element_type=F32)


def _in_proj_kernel(x_ref, g_ref, w_ref, wg_ref, ones_ref, tri_ref, place_ref, qg_ref, kg_ref,
                    bf_ref, wg2_ref, bg_ref, rows_ref,
                    qa_ref, ka_ref, va_ref, gq_ref, gk_ref, gv_ref, gr_ref, gl_ref, ce_ref,
                    carry_ref, *, fox_w, gkw, gvw, n_fox, head_dim):
    @pl.when(pl.program_id(1) == 0)
    def _():
        carry_ref[...] = jnp.zeros_like(carry_ref)

    tm = x_ref.shape[0]
    xns = []
    for r in (slice(0, tm // 2), slice(tm // 2, tm)):
        x = x_ref[r, :]
        xns.append((x * lax.rsqrt(jnp.mean(x * x, axis=-1, keepdims=True) + EPS)
                    * g_ref[...]).astype(BF16))
    xn = jnp.concatenate(xns, axis=0)

    o_k = fox_w
    o_v = 2 * fox_w
    o_gq = 3 * fox_w
    o_gk = o_gq + gkw
    o_gv = o_gk + gkw
    o_gr = o_gv + gvw
    o_sm = o_gr + gvw

    def wcols(lo, width):
        if lo + width <= o_gq:
            return w_ref[:, lo:lo + width]
        return wg_ref[:, lo - o_gq:lo - o_gq + width]

    def proj(lo, width):
        return _dot(xn, wcols(lo, width))

    def head_norm(f, gain_row):
        f2 = (f * f).astype(BF16)
        ssq = jnp.concatenate([_dot(f2[:, c:c + MXU_K], ones_ref[...])
                               for c in range(0, fox_w, MXU_K)], axis=1)
        return f * lax.rsqrt(ssq * (1.0 / head_dim) + EPS) * gain_row

    def spread(f):
        return jnp.concatenate([f[:, (h // 2) * LANES:(h // 2 + 1) * LANES] for h in range(n_fox)],
                               axis=1)

    gr_sm = jnp.concatenate([_dot(h, wcols(o_gr, gvw + LANES)) for h in xns], axis=0)
    sm = gr_sm[:, gvw:]
    lane = lax.broadcasted_iota(jnp.int32, sm.shape, 1)
    lf = jnp.where(lane < n_fox, _log_sigmoid(sm + bf_ref[...]), 0.0)
    fq = proj(0, fox_w)
    fk = proj(o_k, fox_w)
    fq = head_norm(fq, qg_ref[...])
    fv = proj(o_v, fox_w)
    fk = head_norm(fk, kg_ref[...])

    def pack3(v):
        hi, mid, lo = _split3(v)
        return (hi.astype(F32) + pltpu.roll(mid.astype(F32), n_fox, axis=1)
                + pltpu.roll(lo.astype(F32), 2 * n_fox, axis=1))

    cs = _dot(tri_ref[...], pack3(lf).astype(BF16))
    cs = cs + pltpu.roll(cs, LANES - n_fox, axis=1) + pltpu.roll(cs, LANES - 2 * n_fox, axis=1)
    c = jnp.where(lane < n_fox, cs, 0.0) + carry_ref[...]
    carry_ref[...] = c[tm - 1:tm, :]
    c2 = c * LOG2E
    cpk = pack3(c2).astype(BF16)
    ce_ref[...] = jnp.concatenate([c2[0:1], c2[tm - 1:tm], jnp.zeros((6, LANES), F32)], axis=0)

    g_qk = proj(o_gq, 2 * gkw)
    gq_ref[...] = g_qk[:, :gkw].astype(BF16)
    gk_ref[...] = g_qk[:, gkw:].astype(BF16)
    gv_ref[...] = proj(o_gv, gvw).astype(BF16)
    gr_ref[...] = gr_sm[:, :gvw].astype(BF16)
    gate = _dot(sm.astype(BF16), wg2_ref[...]) + bg_ref[...]
    gl_ref[...] = _log_sigmoid(gate) * (1.0 / GLA_TAU)

    wide = lax.broadcasted_iota(jnp.int32, (1, n_fox * LANES), 1)
    rel = (wide & (LANES - 1)) ^ jnp.where((wide // LANES) % 2 == 1, 0, AUG0)
    is_head_dim = rel >= AUG0
    placed = _dot(cpk, place_ref[...])
    va_ref[...] = jnp.where(is_head_dim, spread(fv), rows_ref[0:1, :]).astype(BF16)
    qa_ref[...] = jnp.where(is_head_dim, spread(fq),
                            jnp.where(rel < 3, placed, rows_ref[1:2, :])).astype(BF16)
    ka_ref[...] = jnp.where(is_head_dim, spread(fk),
                            jnp.where(rel >= 3, placed, rows_ref[2:3, :])).astype(BF16)


def _in_proj(x, norm_g, w_in, fox_b_f, fox_q_norm_g, fox_k_norm_g, gla_w_gate2, gla_b_gate, gvw):
    B, S, D = x.shape
    n_fox = fox_b_f.shape[0]
    head_dim = fox_q_norm_g.shape[0]
    fox_w = n_fox * head_dim
    rank, gkw = gla_w_gate2.shape
    tm = min(ROW_TILE, S)
    assert S % tm == 0 and head_dim == LANES // 2 and n_fox % 2 == 0
    assert 3 * n_fox <= LANES and n_fox + rank <= LANES

    sizes = (fox_w, fox_w, fox_w, n_fox, gkw, gkw, gvw, rank, gvw)
    offs = np.concatenate([[0], np.cumsum(sizes)])
    w_bf = w_in.astype(BF16)
    wq, wk, wv, wf, wgq, wgk, wgv, wlr, wgr = [w_bf[:, offs[i]:offs[i + 1]] for i in range(9)]
    assert offs[3] == 3 * fox_w
    w_gla = jnp.concatenate([wgq, wgk, wgv, wgr, wf, wlr,
                             jnp.zeros((D, LANES - n_fox - rank), BF16)], axis=1)

    scale = (1.0 / math.sqrt(head_dim)) * LOG2E
    qg = (jnp.tile(fox_q_norm_g, n_fox) * scale).reshape(1, fox_w)
    kg = jnp.tile(fox_k_norm_g, n_fox).reshape(1, fox_w)
    bf = jnp.zeros((1, LANES), F32).at[0, :n_fox].set(fox_b_f)
    wg2 = jnp.zeros((LANES, gkw), F32).at[n_fox:n_fox + rank].set(gla_w_gate2).astype(BF16)
    bg = gla_b_gate.reshape(1, gkw)

    assert fox_w % MXU_K == 0 and MXU_K % head_dim == 0
    grp = np.arange(MXU_K) // head_dim
    ones_blk = jnp.asarray(grp[:, None] == grp[None, :], BF16)
    r = np.arange(tm)
    tri = jnp.asarray(r[None, :] <= r[:, None], BF16)
    place = np.zeros((LANES, n_fox * LANES), np.float32)
    const_rows = np.zeros((8, n_fox * LANES), np.float32)
    for h in range(n_fox):
        base = h * LANES + (AUG0 if h % 2 == 0 else 0)
        for part in range(3):
            place[part * n_fox + h, base + part] = 1.0
            place[part * n_fox + h, base + 3 + part] = -1.0
            const_rows[1, base + 3 + part] = 1.0
            const_rows[2, base + part] = 1.0
        const_rows[0, base] = 1.0
    place = jnp.asarray(place, BF16)
    const_rows = jnp.asarray(const_rows)

    row = lambda c: pl.BlockSpec((None, tm, c), lambda b, i: (b, i, 0))
    kern = functools.partial(_in_proj_kernel, fox_w=fox_w, gkw=gkw, gvw=gvw, n_fox=n_fox,
                             head_dim=head_dim)
    aug_w = n_fox * LANES
    out_shapes = [jax.ShapeDtypeStruct((B, S, aug_w), BF16)] * 3 + [
        jax.ShapeDtypeStruct((B, S, gkw), BF16), jax.ShapeDtypeStruct((B, S, gkw), BF16),
        jax.ShapeDtypeStruct((B, S, gvw), BF16), jax.ShapeDtypeStruct((B, S, gvw), BF16),
        jax.ShapeDtypeStruct((B, S, gkw), F32),
        jax.ShapeDtypeStruct((B, S // tm, 8, LANES), F32)]
    return pl.pallas_call(
        kern,
        out_shape=out_shapes,
        grid=(B, S // tm),
        in_specs=[row(D), _const_spec((1, D)), _const_spec((D, 3 * fox_w)),
                  _const_spec((D, w_gla.shape[1])),
                  _const_spec((MXU_K, MXU_K)), _const_spec((tm, tm)),
                  _const_spec((LANES, aug_w)),
                  _const_spec((1, fox_w)), _const_spec((1, fox_w)), _const_spec((1, LANES)),
                  _const_spec((LANES, gkw)), _const_spec((1, gkw)), _const_spec((8, aug_w))],
        out_specs=[row(aug_w), row(aug_w), row(aug_w), row(gkw), row(gkw), row(gvw), row(gvw),
                   row(gkw), pl.BlockSpec((None, None, 8, LANES), lambda b, i: (b, i, 0, 0))],
        scratch_shapes=[pltpu.VMEM((1, LANES), F32)],
        compiler_params=_params(("arbitrary", "arbitrary")),
        name="in_proj",
    )(x, norm_g.reshape(1, D), w_bf[:, :3 * fox_w], w_gla, ones_blk, tri, place, qg, kg, bf, wg2, bg, const_rows)


def _gla_kernel(gq_ref, gk_ref, gv_ref, gr_ref, gl_ref, tri3_ref, gain_ref, out_ref,
                st_ref, o_ref, *, dk, dv):
    @pl.when(pl.program_id(1) == 0)
    def _():
        st_ref[...] = jnp.zeros_like(st_ref)

    T, kw = gl_ref.shape
    vw = gv_ref.shape[1]
    n_heads = kw // dk
    C = GLA_CHUNK

    n_chunks = T // C
    chunks = [slice(c * C, (c + 1) * C) for c in range(n_chunks)]

    hi, mid, lo = _split3(gl_ref[...])
    tri3 = tri3_ref[...]
    bcs = [_dot(tri3, jnp.concatenate([hi[r], mid[r], lo[r]], axis=0)) for r in chunks]
    b_last = [b[C - 1:C] for b in bcs]
    bc = jnp.concatenate(bcs, axis=0)
    k = gk_ref[...].astype(F32)
    q_dec = gq_ref[...].astype(F32) * (dk ** -0.5) * jnp.exp(bc)
    k_dec = (k * jnp.exp(-bc)).astype(BF16)

    klane = lax.broadcasted_iota(jnp.int32, (1, kw), 1) // dk
    row_h = lax.broadcasted_iota(jnp.int32, (n_heads * C, C), 0)
    col = lax.broadcasted_iota(jnp.int32, (n_heads * C, C), 1)
    tril = col <= (row_h & (C - 1))
    bd = (lax.broadcasted_iota(jnp.int32, (vw, kw), 0) // dv
          == lax.broadcasted_iota(jnp.int32, (vw, kw), 1) // dk)

    d_sts = [_dot_tn(gv_ref[r, :], (k[r] * jnp.exp(bl - b)).astype(BF16))
             for r, b, bl in zip(chunks, bcs, b_last)]
    st = st_ref[...]
    sts = []
    for d_st, bl in zip(d_sts, b_last):
        sts.append(st.astype(BF16))
        st = st * jnp.exp(bl) + jnp.where(bd, d_st, 0.0)
    st_ref[...] = st

    for r, st_c in zip(chunks, sts):
        qd = q_dec[r]
        lhs = jnp.concatenate([jnp.where(klane == h, qd, 0.0) for h in range(n_heads)],
                              axis=0).astype(BF16)
        a = jnp.where(tril, _dot_nt(lhs, k_dec[r]), 0.0).astype(BF16)
        v = gv_ref[r, :]
        o_intra = jnp.concatenate(
            [_dot(a[h * C:(h + 1) * C], v[:, h * dv:(h + 1) * dv]) for h in range(n_heads)],
            axis=1)
        o_ref[r, :] = o_intra + _dot_nt(qd.astype(BF16), st_c)

    o = o_ref[...]
    gr = gr_ref[...].astype(F32)
    gain = gain_ref[...]
    outs = []
    for h in range(n_heads):
        oh = o[:, h * dv:(h + 1) * dv]
        on = oh * lax.rsqrt(jnp.mean(oh * oh, axis=-1, keepdims=True) + EPS)
        outs.append(on * gain[:, h * dv:(h + 1) * dv])
    out_ref[...] = (jnp.concatenate(outs, axis=1) * (gr * jax.nn.sigmoid(gr))).astype(BF16)


def _gla(gq, gk, gv, gr, gl, out_norm_g):
    B, S, kw = gq.shape
    vw = gv.shape[2]
    dk, dv = kw // GLA_HEADS, vw // GLA_HEADS
    T = min(WIDE_ROW_TILE, S)
    assert S % T == 0 and T % GLA_CHUNK == 0
    r = np.arange(GLA_CHUNK)
    tri3 = jnp.asarray(np.tile(r[None, :] <= r[:, None], (1, 3)), BF16)
    row = lambda c: pl.BlockSpec((None, T, c), lambda b, i: (b, i, 0))
    return pl.pallas_call(
        functools.partial(_gla_kernel, dk=dk, dv=dv),
        out_shape=jax.ShapeDtypeStruct((B, S, vw), BF16),
        grid=(B, S // T),
        in_specs=[row(kw), row(kw), row(vw), row(vw), row(kw),
                  _const_spec((GLA_CHUNK, 3 * GLA_CHUNK)), _const_spec((1, vw))],
        out_specs=row(vw),
        scratch_shapes=[pltpu.VMEM((vw, kw), F32), pltpu.VMEM((T, vw), F32)],
        compiler_params=_params(("arbitrary", "arbitrary")),
        name="gla",
    )(gq, gk, gv, gr, gl, tri3, out_norm_g.reshape(1, vw))


def _fox_kernel(tab_q, tab_k, count, q_ref, k_ref, v_ref, o_ref, *scratch, head_dim, nq):
    pair = pl.program_id(0) * pl.num_programs(1) + pl.program_id(1)
    for hh in range(2):
        lanes = pl.ds(hh * LANES, LANES)
        _fox_head(tab_q, tab_k, count, 2 * pair + hh, hh, q_ref.at[:, lanes], k_ref.at[:, lanes],
                  v_ref.at[:, lanes], o_ref, *scratch, head_dim=head_dim, nq=nq)


def _fox_head(tab_q, tab_k, count, head_idx, slot, q_ref, k_ref, v_ref, o_ref, m_all, acc_all,
              s0, s1, p0, p1, al0, al1, *, head_dim, nq):
    T = s0.shape[0]
    s_buf, p_buf, al_buf = (s0, s1), (p0, p1), (al0, al1)
    depth = 3

    def rows(blk):
        start = blk * T
        return pl.ds(start if isinstance(blk, int) else pl.multiple_of(start, T), T)

    m_all[rows(nq), :] = jnp.zeros((T, LANES), F32)
    acc_all[rows(nq), :] = jnp.zeros((T, LANES), F32)
    for ref in (s1, p0, al0):
        ref[...] = jnp.zeros_like(ref)

    half = T // 2

    def half_rows(blk, i):
        return pl.ds(pl.multiple_of(blk * T, T) + i * half, half)

    def scores(qi, kj, par, diag):
        qr = jnp.minimum(qi, nq - 1)
        if not diag:
            s_buf[par][...] = _dot_nt(q_ref[rows(qr), :], k_ref[rows(kj), :])
            return
        top = _dot_nt(q_ref[half_rows(qr, 0), :], k_ref[half_rows(kj, 0), :])
        bot = _dot_nt(q_ref[half_rows(qr, 1), :], k_ref[rows(kj), :])
        def causal(shape, offset):
            row = lax.broadcasted_iota(jnp.int32, shape, 0)
            col = lax.broadcasted_iota(jnp.int32, shape, 1)
            return col <= row + offset

        s_buf[par][:half, :half] = jnp.where(causal((half, half), 0), top, -jnp.inf)
        s_buf[par][:half, half:] = jnp.full((half, half), -jnp.inf, F32)
        s_buf[par][half:, :] = jnp.where(causal((half, T), half), bot, -jnp.inf)

    def softmax(qi, par, diag):
        m_cur = jnp.broadcast_to(jnp.max(s_buf[par][...], axis=-1, keepdims=True), (T, LANES))
        if diag:
            m_new = m_cur
        else:
            m_prev = m_all[rows(qi), :]
            m_new = jnp.maximum(m_prev, m_cur)
            al_buf[par][...] = jnp.exp2(m_prev - m_new)
        m_all[rows(qi), :] = m_new
        for g in range(T // LANES):
            cols = slice(g * LANES, (g + 1) * LANES)
            p_buf[par][:, cols] = jnp.exp2(s_buf[par][:, cols] - m_new).astype(BF16)

    def pv(qi, kj, par, diag):
        start = pl.multiple_of(kj * T, T)

        def p_dot_v(r0, r1, n_cols):
            parts = [_dot(p_buf[par][r0:r1, c:c + MXU_K], v_ref[pl.ds(start + c, MXU_K), :])
                     for c in range(0, n_cols, MXU_K)]
            return sum(parts[1:], parts[0])

        if diag:
            acc_all[half_rows(qi, 0), :] = p_dot_v(0, half, half)
            acc_all[half_rows(qi, 1), :] = p_dot_v(half, T, T)
        else:
            acc_all[rows(qi), :] = al_buf[par][...] * acc_all[rows(qi), :] + p_dot_v(0, T, T)

    def write_first_head(n):
        r = rows(jnp.minimum(n, nq - 1))
        acc = acc_all[r, :]
        lane = lax.broadcasted_iota(jnp.int32, acc.shape, 1)
        inv_l = 1.0 / acc[:, head_dim:head_dim + 1]
        out = jnp.where(lane < head_dim, acc * inv_l, 0.0)
        o_ref[r, :] = jnp.where(n < nq + depth - 1, out, o_ref[r, :].astype(F32)).astype(BF16)

    def step(c, n, blocks, diag):
        (q0, k0), (q1, _), (q2, k2) = blocks
        if diag and slot == 1:
            write_first_head(n)
        pv(q2, k2, c % 2, diag)
        scores(q0, k0, c % 2, diag)
        softmax(q1, (c - 1) % 2, diag)

    def run(n_blocks, block_at, diag, steps_per_trip):
        spare = (jnp.int32(nq), jnp.int32(0))

        def trip(u, carry):
            hist = [carry[0:2], carry[2:4]]
            for c in range(steps_per_trip):
                n = u * steps_per_trip + c
                nxt = block_at(n)
                step(c, n, [nxt] + hist, diag)
                hist = [nxt] + hist[:1]
            return tuple(x for h in hist for x in h)

        n_trips = (n_blocks + (depth - 2 + steps_per_trip)) // steps_per_trip
        lax.fori_loop(0, n_trips, trip, spare * 2)

    def diag_block(n):
        return jnp.minimum(n, nq), jnp.minimum(n, nq - 1)

    def lower_block(n):
        return tab_q[head_idx, n], tab_k[head_idx, n]

    run(nq, diag_block, True, 2 * ((nq + depth) // 2))
    if nq > 1:
        run(count[head_idx], lower_block, False, FOX_UNROLL)

    if slot == 0:
        return

    per_trip = math.gcd(nq, FOX_FINISH_BLOCKS)

    def finish(u, carry):
        for i in range(per_trip):
            r = rows(u * per_trip + i)
            acc = acc_all[r, :]
            lane = lax.broadcasted_iota(jnp.int32, acc.shape, 1)
            inv_l = 1.0 / acc[:, 0:1]
            o_ref[r, :] = jnp.where(lane < head_dim, o_ref[r, :].astype(F32),
                                    acc * inv_l).astype(BF16)
        return carry

    lax.fori_loop(0, nq // per_trip, finish, 0)


def _fox_block_table(c_edges, qk_bound, n_heads, nq):
    B = c_edges.shape[0]
    c_first = c_edges[:, :, 0, :n_heads]
    c_last = c_edges[:, :, 1, :n_heads]
    gap = c_first[:, :, None, :] - c_last[:, None, :, :]
    qi = np.arange(nq)[:, None]
    kj = np.arange(nq)[None, :]
    lower = jnp.asarray(kj < qi)[None, :, :, None]
    keep = lower & (2.0 * qk_bound + gap + FOX_BOUND_SLACK > -FOX_ZERO_LOG2)
    keep = jnp.transpose(keep, (0, 3, 1, 2)).reshape(B * n_heads, nq * nq)
    count = jnp.sum(keep, axis=1).astype(jnp.int32)
    order = jnp.argsort(~keep, axis=1, stable=True).astype(jnp.int32)
    n_tab = -(-(nq * (nq - 1) // 2 + 2) // FOX_UNROLL) * FOX_UNROLL
    order = jnp.pad(order, ((0, 0), (0, max(0, n_tab - nq * nq))))[:, :n_tab]
    live = jnp.arange(n_tab)[None, :] < count[:, None]
    tab_q = jnp.where(live, order // nq, nq).astype(jnp.int32)
    tab_k = jnp.where(live, order % nq, 0).astype(jnp.int32)
    return tab_q, tab_k, count


def _fox(qa, ka, va, c_edges, qk_bound, head_dim):
    B, S, W = qa.shape
    H = W // LANES
    T = min(FOX_TILE, S)
    assert S % T == 0 and c_edges.shape[1] * T == S
    nq = S // T
    tab_q, tab_k, count = _fox_block_table(c_edges, qk_bound, H, nq)
    assert H % 2 == 0 and 2 * head_dim == LANES
    pair_in = pl.BlockSpec((None, S, 2 * LANES), lambda b, hp, *_: (b, 0, hp))
    pair_out = pl.BlockSpec((None, S, LANES), lambda b, hp, *_: (b, 0, hp))
    state = pltpu.VMEM(((nq + 1) * T, LANES), F32)
    return pl.pallas_call(
        functools.partial(_fox_kernel, head_dim=head_dim, nq=nq),
        out_shape=jax.ShapeDtypeStruct((B, S, H * head_dim), BF16),
        grid_spec=pltpu.PrefetchScalarGridSpec(
            num_scalar_prefetch=3,
            grid=(B, H // 2),
            in_specs=[pair_in, pair_in, pair_in],
            out_specs=pair_out,
            scratch_shapes=([state, state] + [pltpu.VMEM((T, T), F32)] * 2
                            + [pltpu.VMEM((T, T), BF16)] * 2
                            + [pltpu.VMEM((T, LANES), F32)] * 2)),
        compiler_params=_params(("arbitrary", "arbitrary")),
        name="fox",
    )(tab_q, tab_k, count, qa, ka, va)


def _mem_kv_kernel(mem_ref, g_ref, w_ref, kg_ref, k_ref, v_ref, *, xd):
    x = mem_ref[...]
    D = x.shape[1]
    xn = (x * lax.rsqrt(jnp.mean(x * x, axis=-1, keepdims=True) + EPS) * g_ref[...]).astype(BF16)
    kg = kg_ref[...]
    for h in range(D // xd):
        kh = _dot(xn, w_ref[:, h * xd:(h + 1) * xd])
        kn = kh * lax.rsqrt(jnp.mean(kh * kh, axis=-1, keepdims=True) + EPS) * kg
        k_ref[:, h * xd:(h + 1) * xd] = kn.astype(BF16)
    v_ref[...] = _dot(xn, w_ref[:, D:]).astype(BF16)


def _mem_kv(mem, norm_mem_g, wkv, k_norm_g):
    B, M, D = mem.shape
    xd = k_norm_g.shape[0]
    kg = (k_norm_g * (LOG2E / math.sqrt(xd))).reshape(1, xd)
    blk = pl.BlockSpec((None, M, D), lambda b: (b, 0, 0))
    return pl.pallas_call(
        functools.partial(_mem_kv_kernel, xd=xd),
        out_shape=[jax.ShapeDtypeStruct((B, M, D), BF16)] * 2,
        grid=(B,),
        in_specs=[blk, _const_spec((1, D)), _const_spec((D, 2 * D)), _const_spec((1, xd))],
        out_specs=[blk, blk],
        compiler_params=_params(("arbitrary",)),
        name="mem_kv",
    )(mem, norm_mem_g.reshape(1, D), wkv.astype(BF16), kg)


def _mix_kernel(x_ref, fox_ref, gla_ref, wof_ref, wog_ref, g_ref, wq_ref, qg_ref, km_ref, vm_ref,
                wo_ref, h_ref, *, xd):
    tm, D = x_ref.shape
    splits = [slice(i * tm // MIX_SPLITS, (i + 1) * tm // MIX_SPLITS) for i in range(MIX_SPLITS)]
    heads = [slice(hd * xd, (hd + 1) * xd) for hd in range(D // xd)]
    qg = qg_ref[...]

    def rms(v, gain):
        return v * lax.rsqrt(jnp.mean(v * v, axis=-1, keepdims=True) + EPS) * gain

    hs = [x_ref[r, :] + _dot(fox_ref[r, :], wof_ref[...]) + _dot(gla_ref[r, :], wog_ref[...])
          for r in splits]
    hns = [rms(h, g_ref[...]).astype(BF16) for h in hs]
    qs = [_dot(hn, wq_ref[...]) for hn in hns]
    qns = [[rms(q[:, sl], qg).astype(BF16) for sl in heads] for q in qs]
    ss = [[_dot_nt(qn, km_ref[:, sl]) for qn, sl in zip(row, heads)] for row in qns]
    ps = [[jnp.exp2(s - jnp.max(s, axis=-1, keepdims=True)) for s in row] for row in ss]
    os = [[(_dot(p.astype(BF16), vm_ref[:, sl]) * (1.0 / jnp.sum(p, axis=-1, keepdims=True))
            ).astype(BF16) for p, sl in zip(row, heads)] for row in ps]
    for r, h, o in zip(splits, hs, os):
        h_ref[r, :] = h + _dot(jnp.concatenate(o, axis=1), wo_ref[...])


def _mix(x, fox, gla, w_out, norm_g, wq, q_norm_g, km, vm, wo):
    B, S, D = x.shape
    fw = fox.shape[2]
    gw = gla.shape[2]
    M = km.shape[1]
    xd = q_norm_g.shape[0]
    tm = min(WIDE_ROW_TILE, S)
    assert S % tm == 0
    w_out_bf = w_out.astype(BF16)
    wof, wog = w_out_bf[:fw], w_out_bf[fw:]
    row = lambda c: pl.BlockSpec((None, tm, c), lambda b, i: (b, i, 0))
    mem = pl.BlockSpec((None, M, D), lambda b, i: (b, 0, 0))
    return pl.pallas_call(
        functools.partial(_mix_kernel, xd=xd),
        out_shape=jax.ShapeDtypeStruct((B, S, D), F32),
        grid=(B, S // tm),
        in_specs=[row(D), row(fw), row(gw), _const_spec((fw, D)), _const_spec((gw, D)),
                  _const_spec((1, D)), _const_spec((D, D)), _const_spec((1, xd)), mem, mem,
                  _const_spec((D, D))],
        out_specs=row(D),
        compiler_params=_params(("arbitrary", "arbitrary")),
        name="mix",
    )(x, fox, gla, wof, wog, norm_g.reshape(1, D), wq.astype(BF16), q_norm_g.reshape(1, xd),
      km, vm, wo.astype(BF16))


def _mlp_kernel(h_ref, g_ref, w1_ref, w2_ref, y_ref, *, slab):
    h = h_ref[...]
    hn = (h * lax.rsqrt(jnp.mean(h * h, axis=-1, keepdims=True) + EPS) * g_ref[...]).astype(BF16)
    n_slabs = w1_ref.shape[1] // slab
    y = h
    act = None
    for j in range(n_slabs + 1):
        nxt = None
        if j < n_slabs:
            u = jnp.maximum(_dot(hn, w1_ref[:, j * slab:(j + 1) * slab]), 0.0)
            nxt = (u * u).astype(BF16)
        if act is not None:
            y = y + _dot(act, w2_ref[(j - 1) * slab:j * slab, :])
        act = nxt
    y_ref[...] = y


def _mlp(h, norm_g, w1, w2):
    B, S, D = h.shape
    F = w1.shape[1]
    tm = min(WIDE_ROW_TILE, S)
    assert S % tm == 0
    row = pl.BlockSpec((None, tm, D), lambda b, i: (b, i, 0))
    return pl.pallas_call(
        functools.partial(_mlp_kernel, slab=min(F, 1024)),
        out_shape=jax.ShapeDtypeStruct((B, S, D), F32),
        grid=(B, S // tm),
        in_specs=[row, _const_spec((1, D)), _const_spec((D, F)), _const_spec((F, D))],
        out_specs=row,
        compiler_params=_params(("arbitrary", "arbitrary")),
        name="mlp",
    )(h, norm_g.reshape(1, D), w1.astype(BF16), w2.astype(BF16))


def kernel(x, mem, norm_mix_g, w_in, fox_b_f, fox_q_norm_g, fox_k_norm_g, gla_w_gate2, gla_b_gate,
           gla_out_norm_g, w_out, norm_xattn_g, norm_mem_g, xattn_wq, xattn_wkv, xattn_q_norm_g,
           xattn_k_norm_g, xattn_wo, norm_mlp_g, mlp_w1, mlp_w2):
    head_dim = fox_q_norm_g.shape[0]
    qa, ka, va, gq, gk, gv, gr, gl, c_edges = _in_proj(
        x, norm_mix_g, w_in, fox_b_f, fox_q_norm_g, fox_k_norm_g, gla_w_gate2, gla_b_gate,
        gla_out_norm_g.shape[0])
    gla = _gla(gq, gk, gv, gr, gl, gla_out_norm_g)
    qk_bound = (FOX_ROUNDING_SLACK * head_dim * LOG2E / math.sqrt(head_dim)
                * jnp.max(jnp.abs(fox_q_norm_g)) * jnp.max(jnp.abs(fox_k_norm_g)))
    fox = _fox(qa, ka, va, c_edges, qk_bound, head_dim)
    km, vm = _mem_kv(mem, norm_mem_g, xattn_wkv, xattn_k_norm_g)
    h = _mix(x, fox, gla, w_out, norm_xattn_g, xattn_wq, xattn_q_norm_g, km, vm, xattn_wo)
    return _mlp(h, norm_mlp_g, mlp_w1, mlp_w2)
```

```python
import functools
import math

import numpy as np
import jax
import jax.numpy as jnp
from jax import lax
from jax.experimental import pallas as pl
from jax.experimental.pallas import tpu as pltpu

F32 = jnp.float32
BF16 = jnp.bfloat16

EPS = 1e-6
LOG2E = 1.4426950408889634

GLA_CHUNK = 64
GLA_HEADS = 4
GLA_TAU = 16.0

LANES = 128
MXU_K = 256
VMEM_LIMIT_BYTES = 56 * 1024 * 1024

ROW_TILE = 512
WIDE_ROW_TILE = 1024
GLA_ROW_TILE = 2048
MIX_SPLITS = 4
FOX_TILE = 512
FOX_UNROLL = 16
FOX_FINISH_BLOCKS = 4
FOX_ZERO_LOG2 = 160.0
FOX_BOUND_SLACK = 2.0
FOX_ROUNDING_SLACK = 1.02
AUG0 = 64


def _const_spec(shape):
    return pl.BlockSpec(shape, lambda *_: (0,) * len(shape))


def _params(semantics, flags=None):
    return pltpu.CompilerParams(dimension_semantics=semantics,
                                vmem_limit_bytes=VMEM_LIMIT_BYTES, flags=flags)


def _split3(c):
    hi = c.astype(BF16)
    r = c - hi.astype(F32)
    mid = r.astype(BF16)
    lo = (r - mid.astype(F32)).astype(BF16)
    return hi, mid, lo


def _log_sigmoid(z):
    return -(jnp.maximum(-z, 0.0) + jnp.log1p(jnp.exp(-jnp.abs(z))))


def _dot(a, b):
    return jnp.dot(a, b, preferred_element_type=F32)


def _dot_nt(a, b):
    return lax.dot_general(a, b, (((1,), (1,)), ((), ())), preferred_element_type=F32)


def _dot_tn(a, b):
    return lax.dot_general(a, b, (((0,), (0,)), ((), ())), preferred_element_type=F32)


def _in_proj_kernel(x_ref, g_ref, w_ref, wg_ref, ones_ref, tri_ref, place_ref, qg_ref, kg_ref,
                    bf_ref, wg2_ref, bg_ref, rows_ref,
                    qa_ref, ka_ref, va_ref, gq_ref, gk_ref, gv_ref, gr_ref, gl_ref, ce_ref,
                    carry_ref, *, fox_w, gkw, gvw, n_fox, head_dim):
    @pl.when(pl.program_id(1) == 0)
    def _():
        carry_ref[...] = jnp.zeros_like(carry_ref)

    tm = x_ref.shape[0]
    xns = []
    for r in (slice(0, tm // 2), slice(tm // 2, tm)):
        x = x_ref[r, :]
        xns.append((x * lax.rsqrt(jnp.mean(x * x, axis=-1, keepdims=True) + EPS)
                    * g_ref[...]).astype(BF16))
    xn = jnp.concatenate(xns, axis=0)

    o_k = fox_w
    o_v = 2 * fox_w
    o_gq = 3 * fox_w
    o_gk = o_gq + gkw
    o_gv = o_gk + gkw
    o_gr = o_gv + gvw
    o_sm = o_gr + gvw

    def wcols(lo, width):
        if lo + width <= o_gq:
            return w_ref[:, lo:lo + width]
        return wg_ref[:, lo - o_gq:lo - o_gq + width]

    def proj(lo, width):
        return _dot(xn, wcols(lo, width))

    def head_norm(f, gain_row):
        f2 = (f * f).astype(BF16)
        ssq = jnp.concatenate([_dot(f2[:, c:c + MXU_K], ones_ref[...])
                               for c in range(0, fox_w, MXU_K)], axis=1)
        return f * lax.rsqrt(ssq * (1.0 / head_dim) + EPS) * gain_row

    def spread(f):
        return jnp.concatenate([f[:, (h // 2) * LANES:(h // 2 + 1) * LANES] for h in range(n_fox)],
                               axis=1)

    gr_sm = jnp.concatenate([_dot(h, wcols(o_gr, gvw + LANES)) for h in xns], axis=0)
    sm = gr_sm[:, gvw:]
    lane = lax.broadcasted_iota(jnp.int32, sm.shape, 1)
    lf = jnp.where(lane < n_fox, _log_sigmoid(sm + bf_ref[...]), 0.0)
    fq = proj(0, fox_w)
    fk = proj(o_k, fox_w)
    fq = head_norm(fq, qg_ref[...])
    fv = proj(o_v, fox_w)
    fk = head_norm(fk, kg_ref[...])

    def pack3(v):
        hi, mid, lo = _split3(v)
        return (hi.astype(F32) + pltpu.roll(mid.astype(F32), n_fox, axis=1)
                + pltpu.roll(lo.astype(F32), 2 * n_fox, axis=1))

    cs = _dot(tri_ref[...], pack3(lf).astype(BF16))
    cs = cs + pltpu.roll(cs, LANES - n_fox, axis=1) + pltpu.roll(cs, LANES - 2 * n_fox, axis=1)
    c = jnp.where(lane < n_fox, cs, 0.0) + carry_ref[...]
    carry_ref[...] = c[tm - 1:tm, :]
    c2 = c * LOG2E
    cpk = pack3(c2).astype(BF16)
    ce_ref[...] = jnp.concatenate([c2[0:1], c2[tm - 1:tm], jnp.zeros((6, LANES), F32)], axis=0)

    g_qk = proj(o_gq, 2 * gkw)
    gq_ref[...] = g_qk[:, :gkw].astype(BF16)
    gk_ref[...] = g_qk[:, gkw:].astype(BF16)
    gv_ref[...] = proj(o_gv, gvw).astype(BF16)
    gr_ref[...] = gr_sm[:, :gvw].astype(BF16)
    gate = _dot(sm.astype(BF16), wg2_ref[...]) + bg_ref[...]
    gl_ref[...] = _log_sigmoid(gate) * (1.0 / GLA_TAU)

    wide = lax.broadcasted_iota(jnp.int32, (1, n_fox * LANES), 1)
    rel = (wide & (LANES - 1)) ^ jnp.where((wide // LANES) % 2 == 1, 0, AUG0)
    is_head_dim = rel >= AUG0
    placed = _dot(cpk, place_ref[...])
    va_ref[...] = jnp.where(is_head_dim, spread(fv), rows_ref[0:1, :]).astype(BF16)
    qa_ref[...] = jnp.where(is_head_dim, spread(fq),
                            jnp.where(rel < 3, placed, rows_ref[1:2, :])).astype(BF16)
    ka_ref[...] = jnp.where(is_head_dim, spread(fk),
                            jnp.where(rel >= 3, placed, rows_ref[2:3, :])).astype(BF16)


def _in_proj(x, norm_g, w_in, fox_b_f, fox_q_norm_g, fox_k_norm_g, gla_w_gate2, gla_b_gate, gvw):
    B, S, D = x.shape
    n_fox = fox_b_f.shape[0]
    head_dim = fox_q_norm_g.shape[0]
    fox_w = n_fox * head_dim
    rank, gkw = gla_w_gate2.shape
    tm = min(ROW_TILE, S)
    assert S % tm == 0 and head_dim == LANES // 2 and n_fox % 2 == 0
    assert 3 * n_fox <= LANES and n_fox + rank <= LANES

    sizes = (fox_w, fox_w, fox_w, n_fox, gkw, gkw, gvw, rank, gvw)
    offs = np.concatenate([[0], np.cumsum(sizes)])
    w_bf = w_in.astype(BF16)
    wq, wk, wv, wf, wgq, wgk, wgv, wlr, wgr = [w_bf[:, offs[i]:offs[i + 1]] for i in range(9)]
    assert offs[3] == 3 * fox_w
    w_gla = jnp.concatenate([wgq, wgk, wgv, wgr, wf, wlr,
                             jnp.zeros((D, LANES - n_fox - rank), BF16)], axis=1)

    scale = (1.0 / math.sqrt(head_dim)) * LOG2E
    qg = (jnp.tile(fox_q_norm_g, n_fox) * scale).reshape(1, fox_w)
    kg = jnp.tile(fox_k_norm_g, n_fox).reshape(1, fox_w)
    bf = jnp.zeros((1, LANES), F32).at[0, :n_fox].set(fox_b_f)
    wg2 = jnp.zeros((LANES, gkw), F32).at[n_fox:n_fox + rank].set(gla_w_gate2).astype(BF16)
    bg = gla_b_gate.reshape(1, gkw)

    assert fox_w % MXU_K == 0 and MXU_K % head_dim == 0
    grp = np.arange(MXU_K) // head_dim
    ones_blk = jnp.asarray(grp[:, None] == grp[None, :], BF16)
    r = np.arange(tm)
    tri = jnp.asarray(r[None, :] <= r[:, None], BF16)
    place = np.zeros((LANES, n_fox * LANES), np.float32)
    const_rows = np.zeros((8, n_fox * LANES), np.float32)
    for h in range(n_fox):
        base = h * LANES + (AUG0 if h % 2 == 0 else 0)
        for part in range(3):
            place[part * n_fox + h, base + part] = 1.0
            place[part * n_fox + h, base + 3 + part] = -1.0
            const_rows[1, base + 3 + part] = 1.0
            const_rows[2, base + part] = 1.0
        const_rows[0, base] = 1.0
    place = jnp.asarray(place, BF16)
    const_rows = jnp.asarray(const_rows)

    row = lambda c: pl.BlockSpec((None, tm, c), lambda b, i: (b, i, 0))
    kern = functools.partial(_in_proj_kernel, fox_w=fox_w, gkw=gkw, gvw=gvw, n_fox=n_fox,
                             head_dim=head_dim)
    aug_w = n_fox * LANES
    out_shapes = [jax.ShapeDtypeStruct((B, S, aug_w), BF16)] * 3 + [
        jax.ShapeDtypeStruct((B, S, gkw), BF16), jax.ShapeDtypeStruct((B, S, gkw), BF16),
        jax.ShapeDtypeStruct((B, S, gvw), BF16), jax.ShapeDtypeStruct((B, S, gvw), BF16),
        jax.ShapeDtypeStruct((B, S, gkw), F32),
        jax.ShapeDtypeStruct((B, S // tm, 8, LANES), F32)]
    return pl.pallas_call(
        kern,
        out_shape=out_shapes,
        grid=(B, S // tm),
        in_specs=[row(D), _const_spec((1, D)), _const_spec((D, 3 * fox_w)),
                  _const_spec((D, w_gla.shape[1])),
                  _const_spec((MXU_K, MXU_K)), _const_spec((tm, tm)),
                  _const_spec((LANES, aug_w)),
                  _const_spec((1, fox_w)), _const_spec((1, fox_w)), _const_spec((1, LANES)),
                  _const_spec((LANES, gkw)), _const_spec((1, gkw)), _const_spec((8, aug_w))],
        out_specs=[row(aug_w), row(aug_w), row(aug_w), row(gkw), row(gkw), row(gvw), row(gvw),
                   row(gkw), pl.BlockSpec((None, None, 8, LANES), lambda b, i: (b, i, 0, 0))],
        scratch_shapes=[pltpu.VMEM((1, LANES), F32)],
        compiler_params=_params(("arbitrary", "arbitrary")),
        name="in_proj",
    )(x, norm_g.reshape(1, D), w_bf[:, :3 * fox_w], w_gla, ones_blk, tri, place, qg, kg, bf, wg2, bg, const_rows)


def _gla_kernel(gq_ref, gk_ref, gv_ref, gr_ref, gl_ref, tri3_ref, gain_ref, out_ref,
                st_ref, o_ref, *, dk, dv):
    @pl.when(pl.program_id(1) == 0)
    def _():
        st_ref[...] = jnp.zeros_like(st_ref)

    T, kw = gl_ref.shape
    vw = gv_ref.shape[1]
    n_heads = kw // dk
    C = GLA_CHUNK

    n_chunks = T // C
    chunks = [slice(c * C, (c + 1) * C) for c in range(n_chunks)]

    hi, mid, lo = _split3(gl_ref[...])
    tri3 = tri3_ref[...]
    bcs = [_dot(tri3, jnp.concatenate([hi[r], mid[r], lo[r]], axis=0)) for r in chunks]
    b_last = [b[C - 1:C] for b in bcs]
    bc = jnp.concatenate(bcs, axis=0)
    k = gk_ref[...].astype(F32)
    q_dec = gq_ref[...].astype(F32) * (dk ** -0.5) * jnp.exp(bc)
    k_dec = (k * jnp.exp(-bc)).astype(BF16)

    klane = lax.broadcasted_iota(jnp.int32, (1, kw), 1) // dk
    row_h = lax.broadcasted_iota(jnp.int32, (n_heads * C, C), 0)
    col = lax.broadcasted_iota(jnp.int32, (n_heads * C, C), 1)
    tril = col <= (row_h & (C - 1))
    bd = (lax.broadcasted_iota(jnp.int32, (vw, kw), 0) // dv
          == lax.broadcasted_iota(jnp.int32, (vw, kw), 1) // dk)

    d_sts = [_dot_tn(gv_ref[r, :], (k[r] * jnp.exp(bl - b)).astype(BF16))
             for r, b, bl in zip(chunks, bcs, b_last)]
    st = st_ref[...]
    sts = []
    for d_st, bl in zip(d_sts, b_last):
        sts.append(st.astype(BF16))
        st = st * jnp.exp(bl) + jnp.where(bd, d_st, 0.0)
    st_ref[...] = st

    for r, st_c in zip(chunks, sts):
        qd = q_dec[r]
        lhs = jnp.concatenate([jnp.where(klane == h, qd, 0.0) for h in range(n_heads)],
                              axis=0).astype(BF16)
        a = jnp.where(tril, _dot_nt(lhs, k_dec[r]), 0.0).astype(BF16)
        v = gv_ref[r, :]
        o_intra = jnp.concatenate(
            [_dot(a[h * C:(h + 1) * C], v[:, h * dv:(h + 1) * dv]) for h in range(n_heads)],
            axis=1)
        o_ref[r, :] = o_intra + _dot_nt(qd.astype(BF16), st_c)

    o = o_ref[...]
    gr = gr_ref[...].astype(F32)
    gain = gain_ref[...]
    outs = []
    for h in range(n_heads):
        oh = o[:, h * dv:(h + 1) * dv]
        on = oh * lax.rsqrt(jnp.mean(oh * oh, axis=-1, keepdims=True) + EPS)
        outs.append(on * gain[:, h * dv:(h + 1) * dv])
    out_ref[...] = (jnp.concatenate(outs, axis=1) * (gr * jax.nn.sigmoid(gr))).astype(BF16)


def _gla(gq, gk, gv, gr, gl, out_norm_g):
    B, S, kw = gq.shape
    vw = gv.shape[2]
    dk, dv = kw // GLA_HEADS, vw // GLA_HEADS
    T = min(GLA_ROW_TILE, S)
    assert S % T == 0 and T % GLA_CHUNK == 0
    r = np.arange(GLA_CHUNK)
    tri3 = jnp.asarray(np.tile(r[None, :] <= r[:, None], (1, 3)), BF16)
    row = lambda c: pl.BlockSpec((None, T, c), lambda b, i: (b, i, 0))
    return pl.pallas_call(
        functools.partial(_gla_kernel, dk=dk, dv=dv),
        out_shape=jax.ShapeDtypeStruct((B, S, vw), BF16),
        grid=(B, S // T),
        in_specs=[row(kw), row(kw), row(vw), row(vw), row(kw),
                  _const_spec((GLA_CHUNK, 3 * GLA_CHUNK)), _const_spec((1, vw))],
        out_specs=row(vw),
        scratch_shapes=[pltpu.VMEM((vw, kw), F32), pltpu.VMEM((T, vw), F32)],
        compiler_params=_params(("arbitrary", "arbitrary")),
        name="gla",
    )(gq, gk, gv, gr, gl, tri3, out_norm_g.reshape(1, vw))


def _fox_kernel(tab_q, tab_k, count, q_ref, k_ref, v_ref, o_ref, *scratch, head_dim, nq):
    pair = pl.program_id(0) * pl.num_programs(1) + pl.program_id(1)
    for hh in range(2):
        lanes = pl.ds(hh * LANES, LANES)
        _fox_head(tab_q, tab_k, count, 2 * pair + hh, hh, q_ref.at[:, lanes], k_ref.at[:, lanes],
                  v_ref.at[:, lanes], o_ref, *scratch, head_dim=head_dim, nq=nq)


def _fox_head(tab_q, tab_k, count, head_idx, slot, q_ref, k_ref, v_ref, o_ref, m_all, acc_all,
              s0, s1, p0, p1, al0, al1, *, head_dim, nq):
    T = s0.shape[0]
    s_buf, p_buf, al_buf = (s0, s1), (p0, p1), (al0, al1)
    depth = 3

    def rows(blk):
        start = blk * T
        return pl.ds(start if isinstance(blk, int) else pl.multiple_of(start, T), T)

    m_all[rows(nq), :] = jnp.zeros((T, LANES), F32)
    acc_all[rows(nq), :] = jnp.zeros((T, LANES), F32)
    for ref in (s1, p0, al0):
        ref[...] = jnp.zeros_like(ref)

    half = T // 2

    def half_rows(blk, i):
        return pl.ds(pl.multiple_of(blk * T, T) + i * half, half)

    def scores(qi, kj, par, diag):
        qr = jnp.minimum(qi, nq - 1)
        if not diag:
            s_buf[par][...] = _dot_nt(q_ref[rows(qr), :], k_ref[rows(kj), :])
            return
        top = _dot_nt(q_ref[half_rows(qr, 0), :], k_ref[half_rows(kj, 0), :])
        bot = _dot_nt(q_ref[half_rows(qr, 1), :], k_ref[rows(kj), :])
        def causal(shape, offset):
            row = lax.broadcasted_iota(jnp.int32, shape, 0)
            col = lax.broadcasted_iota(jnp.int32, shape, 1)
            return col <= row + offset

        s_buf[par][:half, :half] = jnp.where(causal((half, half), 0), top, -jnp.inf)
        s_buf[par][:half, half:] = jnp.full((half, half), -jnp.inf, F32)
        s_buf[par][half:, :] = jnp.where(causal((half, T), half), bot, -jnp.inf)

    def softmax(qi, par, diag):
        m_cur = jnp.broadcast_to(jnp.max(s_buf[par][...], axis=-1, keepdims=True), (T, LANES))
        if diag:
            m_new = m_cur
        else:
            m_prev = m_all[rows(qi), :]
            m_new = jnp.maximum(m_prev, m_cur)
            al_buf[par][...] = jnp.exp2(m_prev - m_new)
        m_all[rows(qi), :] = m_new
        for g in range(T // LANES):
            cols = slice(g * LANES, (g + 1) * LANES)
            p_buf[par][:, cols] = jnp.exp2(s_buf[par][:, cols] - m_new).astype(BF16)

    def pv(qi, kj, par, diag):
        start = pl.multiple_of(kj * T, T)

        def p_dot_v(r0, r1, n_cols):
            parts = [_dot(p_buf[par][r0:r1, c:c + MXU_K], v_ref[pl.ds(start + c, MXU_K), :])
                     for c in range(0, n_cols, MXU_K)]
            return sum(parts[1:], parts[0])

        if diag:
            acc_all[half_rows(qi, 0), :] = p_dot_v(0, half, half)
            acc_all[half_rows(qi, 1), :] = p_dot_v(half, T, T)
        else:
            acc_all[rows(qi), :] = al_buf[par][...] * acc_all[rows(qi), :] + p_dot_v(0, T, T)

    def write_first_head(n):
        r = rows(jnp.minimum(n, nq - 1))
        acc = acc_all[r, :]
        lane = lax.broadcasted_iota(jnp.int32, acc.shape, 1)
        inv_l = 1.0 / acc[:, head_dim:head_dim + 1]
        out = jnp.where(lane < head_dim, acc * inv_l, 0.0)
        o_ref[r, :] = jnp.where(n < nq + depth - 1, out, o_ref[r, :].astype(F32)).astype(BF16)

    def step(c, n, blocks, diag):
        (q0, k0), (q1, _), (q2, k2) = blocks
        if diag and slot == 1:
            write_first_head(n)
        pv(q2, k2, c % 2, diag)
        scores(q0, k0, c % 2, diag)
        softmax(q1, (c - 1) % 2, diag)

    def run(n_blocks, block_at, diag, steps_per_trip):
        spare = (jnp.int32(nq), jnp.int32(0))

        def trip(u, carry):
            hist = [carry[0:2], carry[2:4]]
            for c in range(steps_per_trip):
                n = u * steps_per_trip + c
                nxt = block_at(n)
                step(c, n, [nxt] + hist, diag)
                hist = [nxt] + hist[:1]
            return tuple(x for h in hist for x in h)

        n_trips = (n_blocks + (depth - 2 + steps_per_trip)) // steps_per_trip
        lax.fori_loop(0, n_trips, trip, spare * 2)

    def diag_block(n):
        return jnp.minimum(n, nq), jnp.minimum(n, nq - 1)

    def lower_block(n):
        return tab_q[head_idx, n], tab_k[head_idx, n]

    run(nq, diag_block, True, 2 * ((nq + depth) // 2))
    if nq > 1:
        run(count[head_idx], lower_block, False, FOX_UNROLL)

    if slot == 0:
        return

    per_trip = math.gcd(nq, FOX_FINISH_BLOCKS)

    def finish(u, carry):
        for i in range(per_trip):
            r = rows(u * per_trip + i)
            acc = acc_all[r, :]
            lane = lax.broadcasted_iota(jnp.int32, acc.shape, 1)
            inv_l = 1.0 / acc[:, 0:1]
            o_ref[r, :] = jnp.where(lane < head_dim, o_ref[r, :].astype(F32),
                                    acc * inv_l).astype(BF16)
        return carry

    lax.fori_loop(0, nq // per_trip, finish, 0)


def _fox_block_table(c_edges, qk_bound, n_heads, nq):
    B = c_edges.shape[0]
    c_first = c_edges[:, :, 0, :n_heads]
    c_last = c_edges[:, :, 1, :n_heads]
    gap = c_first[:, :, None, :] - c_last[:, None, :, :]
    qi = np.arange(nq)[:, None]
    kj = np.arange(nq)[None, :]
    lower = jnp.asarray(kj < qi)[None, :, :, None]
    keep = lower & (2.0 * qk_bound + gap + FOX_BOUND_SLACK > -FOX_ZERO_LOG2)
    keep = jnp.transpose(keep, (0, 3, 1, 2)).reshape(B * n_heads, nq * nq)
    count = jnp.sum(keep, axis=1).astype(jnp.int32)
    order = jnp.argsort(~keep, axis=1, stable=True).astype(jnp.int32)
    n_tab = -(-(nq * (nq - 1) // 2 + 2) // FOX_UNROLL) * FOX_UNROLL
    order = jnp.pad(order, ((0, 0), (0, max(0, n_tab - nq * nq))))[:, :n_tab]
    live = jnp.arange(n_tab)[None, :] < count[:, None]
    tab_q = jnp.where(live, order // nq, nq).astype(jnp.int32)
    tab_k = jnp.where(live, order % nq, 0).astype(jnp.int32)
    return tab_q, tab_k, count


def _fox(qa, ka, va, c_edges, qk_bound, head_dim):
    B, S, W = qa.shape
    H = W // LANES
    T = min(FOX_TILE, S)
    assert S % T == 0 and c_edges.shape[1] * T == S
    nq = S // T
    tab_q, tab_k, count = _fox_block_table(c_edges, qk_bound, H, nq)
    assert H % 2 == 0 and 2 * head_dim == LANES
    pair_in = pl.BlockSpec((None, S, 2 * LANES), lambda b, hp, *_: (b, 0, hp))
    pair_out = pl.BlockSpec((None, S, LANES), lambda b, hp, *_: (b, 0, hp))
    state = pltpu.VMEM(((nq + 1) * T, LANES), F32)
    return pl.pallas_call(
        functools.partial(_fox_kernel, head_dim=head_dim, nq=nq),
        out_shape=jax.ShapeDtypeStruct((B, S, H * head_dim), BF16),
        grid_spec=pltpu.PrefetchScalarGridSpec(
            num_scalar_prefetch=3,
            grid=(B, H // 2),
            in_specs=[pair_in, pair_in, pair_in],
            out_specs=pair_out,
            scratch_shapes=([state, state] + [pltpu.VMEM((T, T), F32)] * 2
                            + [pltpu.VMEM((T, T), BF16)] * 2
                            + [pltpu.VMEM((T, LANES), F32)] * 2)),
        compiler_params=_params(("arbitrary", "arbitrary")),
        name="fox",
    )(tab_q, tab_k, count, qa, ka, va)


def _mem_kv_kernel(mem_ref, g_ref, w_ref, kg_ref, k_ref, v_ref, *, xd):
    x = mem_ref[...]
    D = x.shape[1]
    xn = (x * lax.rsqrt(jnp.mean(x * x, axis=-1, keepdims=True) + EPS) * g_ref[...]).astype(BF16)
    kg = kg_ref[...]
    for h in range(D // xd):
        kh = _dot(xn, w_ref[:, h * xd:(h + 1) * xd])
        kn = kh * lax.rsqrt(jnp.mean(kh * kh, axis=-1, keepdims=True) + EPS) * kg
        k_ref[:, h * xd:(h + 1) * xd] = kn.astype(BF16)
    v_ref[...] = _dot(xn, w_ref[:, D:]).astype(BF16)


def _mem_kv(mem, norm_mem_g, wkv, k_norm_g):
    B, M, D = mem.shape
    xd = k_norm_g.shape[0]
    kg = (k_norm_g * (LOG2E / math.sqrt(xd))).reshape(1, xd)
    blk = pl.BlockSpec((None, M, D), lambda b: (b, 0, 0))
    return pl.pallas_call(
        functools.partial(_mem_kv_kernel, xd=xd),
        out_shape=[jax.ShapeDtypeStruct((B, M, D), BF16)] * 2,
        grid=(B,),
        in_specs=[blk, _const_spec((1, D)), _const_spec((D, 2 * D)), _const_spec((1, xd))],
        out_specs=[blk, blk],
        compiler_params=_params(("arbitrary",)),
        name="mem_kv",
    )(mem, norm_mem_g.reshape(1, D), wkv.astype(BF16), kg)


def _mix_kernel(x_ref, fox_ref, gla_ref, wof_ref, wog_ref, g_ref, wq_ref, qg_ref, km_ref, vm_ref,
                wo_ref, h_ref, *, xd):
    tm, D = x_ref.shape
    splits = [slice(i * tm // MIX_SPLITS, (i + 1) * tm // MIX_SPLITS) for i in range(MIX_SPLITS)]
    heads = [slice(hd * xd, (hd + 1) * xd) for hd in range(D // xd)]
    qg = qg_ref[...]

    def rms(v, gain):
        return v * lax.rsqrt(jnp.mean(v * v, axis=-1, keepdims=True) + EPS) * gain

    hs = [x_ref[r, :] + _dot(fox_ref[r, :], wof_ref[...]) + _dot(gla_ref[r, :], wog_ref[...])
          for r in splits]
    hns = [rms(h, g_ref[...]).astype(BF16) for h in hs]
    qs = [_dot(hn, wq_ref[...]) for hn in hns]
    qns = [[rms(q[:, sl], qg).astype(BF16) for sl in heads] for q in qs]
    ss = [[_dot_nt(qn, km_ref[:, sl]) for qn, sl in zip(row, heads)] for row in qns]
    ps = [[jnp.exp2(s - jnp.max(s, axis=-1, keepdims=True)) for s in row] for row in ss]
    os = [[(_dot(p.astype(BF16), vm_ref[:, sl]) * (1.0 / jnp.sum(p, axis=-1, keepdims=True))
            ).astype(BF16) for p, sl in zip(row, heads)] for row in ps]
    for r, h, o in zip(splits, hs, os):
        h_ref[r, :] = h + _dot(jnp.concatenate(o, axis=1), wo_ref[...])


def _mix(x, fox, gla, w_out, norm_g, wq, q_norm_g, km, vm, wo):
    B, S, D = x.shape
    fw = fox.shape[2]
    gw = gla.shape[2]
    M = km.shape[1]
    xd = q_norm_g.shape[0]
    tm = min(WIDE_ROW_TILE, S)
    assert S % tm == 0
    w_out_bf = w_out.astype(BF16)
    wof, wog = w_out_bf[:fw], w_out_bf[fw:]
    row = lambda c: pl.BlockSpec((None, tm, c), lambda b, i: (b, i, 0))
    mem = pl.BlockSpec((None, M, D), lambda b, i: (b, 0, 0))
    return pl.pallas_call(
        functools.partial(_mix_kernel, xd=xd),
        out_shape=jax.ShapeDtypeStruct((B, S, D), F32),
        grid=(B, S // tm),
        in_specs=[row(D), row(fw), row(gw), _const_spec((fw, D)), _const_spec((gw, D)),
                  _const_spec((1, D)), _const_spec((D, D)), _const_spec((1, xd)), mem, mem,
                  _const_spec((D, D))],
        out_specs=row(D),
        compiler_params=_params(("arbitrary", "arbitrary")),
        name="mix",
    )(x, fox, gla, wof, wog, norm_g.reshape(1, D), wq.astype(BF16), q_norm_g.reshape(1, xd),
      km, vm, wo.astype(BF16))


def _mlp_kernel(h_ref, g_ref, w1_ref, w2_ref, y_ref, *, slab):
    h = h_ref[...]
    hn = (h * lax.rsqrt(jnp.mean(h * h, axis=-1, keepdims=True) + EPS) * g_ref[...]).astype(BF16)
    n_slabs = w1_ref.shape[1] // slab
    y = h
    act = None
    for j in range(n_slabs + 1):
        nxt = None
        if j < n_slabs:
            u = jnp.maximum(_dot(hn, w1_ref[:, j * slab:(j + 1) * slab]), 0.0)
            nxt = (u * u).astype(BF16)
        if act is not None:
            y = y + _dot(act, w2_ref[(j - 1) * slab:j * slab, :])
        act = nxt
    y_ref[...] = y


def _mlp(h, norm_g, w1, w2):
    B, S, D = h.shape
    F = w1.shape[1]
    tm = min(WIDE_ROW_TILE, S)
    assert S % tm == 0
    row = pl.BlockSpec((None, tm, D), lambda b, i: (b, i, 0))
    return pl.pallas_call(
        functools.partial(_mlp_kernel, slab=min(F, 1024)),
        out_shape=jax.ShapeDtypeStruct((B, S, D), F32),
        grid=(B, S // tm),
        in_specs=[row, _const_spec((1, D)), _const_spec((D, F)), _const_spec((F, D))],
        out_specs=row,
        compiler_params=_params(("arbitrary", "arbitrary")),
        name="mlp",
    )(h, norm_g.reshape(1, D), w1.astype(BF16), w2.astype(BF16))


def kernel(x, mem, norm_mix_g, w_in, fox_b_f, fox_q_norm_g, fox_k_norm_g, gla_w_gate2, gla_b_gate,
           gla_out_norm_g, w_out, norm_xattn_g, norm_mem_g, xattn_wq, xattn_wkv, xattn_q_norm_g,
           xattn_k_norm_g, xattn_wo, norm_mlp_g, mlp_w1, mlp_w2):
    head_dim = fox_q_norm_g.shape[0]
    qa, ka, va, gq, gk, gv, gr, gl, c_edges = _in_proj(
        x, norm_mix_g, w_in, fox_b_f, fox_q_norm_g, fox_k_norm_g, gla_w_gate2, gla_b_gate,
        gla_out_norm_g.shape[0])
    gla = _gla(gq, gk, gv, gr, gl, gla_out_norm_g)
    qk_bound = (FOX_ROUNDING_SLACK * head_dim * LOG2E / math.sqrt(head_dim)
                * jnp.max(jnp.abs(fox_q_norm_g)) * jnp.max(jnp.abs(fox_k_norm_g)))
    fox = _fox(qa, ka, va, c_edges, qk_bound, head_dim)
    km, vm = _mem_kv(mem, norm_mem_g, xattn_wkv, xattn_k_norm_g)
    h = _mix(x, fox, gla, w_out, norm_xattn_g, xattn_wq, xattn_q_norm_g, km, vm, xattn_wo)
    return _mlp(h, norm_mlp_g, mlp_w1, mlp_w2)
```
